```python
import math
import jax
import jax.numpy as jnp
from jax import lax
import numpy as np


D_MODEL = 1024
BATCH = 8
SEQ = 2048
DEPTH = 4

GRID_W = 64
CTX_LEN = 256
N_HEADS = 8
HEAD_DIM = 64
V_DIM = 2 * HEAD_DIM
QK_W = N_HEADS * 2 * HEAD_DIM
ATTN_W = N_HEADS * V_DIM
Q_BLOCK = 128
ROPE_BASE = 10000.0
ROPE_AXIS_DIM = HEAD_DIM // 2
SSM_W = D_MODEL // 2
SSM_GROUP = 16
SSM_GROUPS = SSM_W // SSM_GROUP
SSM_STATE = 64
N_GROUPS = 4
EXPERTS_PER_GROUP = 8
N_EXPERTS = N_GROUPS * EXPERTS_PER_GROUP
TOP_K = 2
EXPERT_HIDDEN = D_MODEL // 2
MOE_BLOCK = 128
KVU_W = QK_W + ATTN_W + SSM_W
IN_W = KVU_W + QK_W + 2 * D_MODEL
ALPHA = (2.0 * DEPTH) ** 0.25
BETA = (8.0 * DEPTH) ** -0.25
LN_EPS = 1e-5
MOD_INIT = 0.5

kernel_name = 'hybrid_diffattn_s5_hmoe_prefix_trunk'


def _layer_norm(x, g, b):
    xf = x.astype(jnp.float32)
    xc = xf - xf.mean(-1, keepdims=True)
    var = (xc * xc).mean(-1, keepdims=True)
    return (xc * lax.rsqrt(var + LN_EPS) * g + b).astype(x.dtype)


def _axial_rope_tables(n_tokens):
    rows = n_tokens // GRID_W
    row = jnp.repeat(jnp.arange(rows, dtype=jnp.float32), GRID_W)
    col = jnp.tile(jnp.arange(GRID_W, dtype=jnp.float32), rows)
    inv = 1.0 / (ROPE_BASE ** (jnp.arange(0, ROPE_AXIS_DIM, 2, dtype=jnp.float32) / ROPE_AXIS_DIM))
    ang = jnp.concatenate([row[:, None] * inv, col[:, None] * inv], axis=-1)
    return jnp.cos(ang), jnp.sin(ang)


def _apply_axial_rope(t, cos, sin):
    half = ROPE_AXIS_DIM // 2
    tr = t.reshape(t.shape[:-1] + (2, 2, half))
    t1, t2 = tr[..., 0, :], tr[..., 1, :]
    c = cos.reshape(1, cos.shape[0], 1, 1, 2, half)
    s = sin.reshape(1, sin.shape[0], 1, 1, 2, half)
    return jnp.stack([t1 * c - t2 * s, t1 * s + t2 * c], axis=-2).reshape(t.shape)


def _diff_attention(q, k, v, lam):
    s = jnp.einsum('bqhmd,bkhmd->mbhqk', q, k)
    p = jax.nn.softmax(s, axis=-1)
    a = p[0] - lam * p[1]
    return jnp.einsum('bhqk,bkhe->bqhe', a, v)


def _head_norm(o, g, lam_init):
    on = o * lax.rsqrt(jnp.mean(o * o, axis=-1, keepdims=True) + LN_EPS) * g.astype(jnp.float32) * (1.0 - lam_init)
    return on.reshape(o.shape[0], o.shape[1], ATTN_W)


def _linear_scan(a_bar, bu, reverse):
    a = jnp.broadcast_to(a_bar, bu.shape)

    def combine(e1, e2):
        a1, b1 = e1
        a2, b2 = e2
        return a1 * a2, a2 * b1 + b2

    _, h = lax.associative_scan(combine, (a, bu), axis=1, reverse=reverse)
    return h


def _s5_bidirectional(uc, ux, a_re, a_im, log_dt, b_re, b_im, c_re, c_im, d_skip, need_ctx):
    f32 = jnp.float32
    bsz, n_ctx, _ = uc.shape
    n_lat = ux.shape[1]
    ucf, uxf = uc.astype(f32), ux.astype(f32)
    ucg = ucf.reshape(bsz, n_ctx, SSM_GROUPS, SSM_GROUP)
    uxg = uxf.reshape(bsz, n_lat, SSM_GROUPS, SSM_GROUP)
    d = d_skip.astype(f32)
    yx = uxf * d
    yc = ucf * d if need_ctx else None
    for di, rev in enumerate((False, True)):
        lam = lax.complex(a_re[di].astype(f32), a_im[di].astype(f32))
        dt = jnp.exp(log_dt[di].astype(f32))[:, None]
        a_bar = jnp.exp(lam * dt)
        b_bar = ((a_bar - 1.0) / lam)[..., None] * lax.complex(b_re[di].astype(f32), b_im[di].astype(f32))
        cm = lax.complex(c_re[di].astype(f32), c_im[di].astype(f32))
        h_ctx = _linear_scan(a_bar, jnp.einsum('btgc,gpc->btgp', ucg, b_bar), rev)
        h0 = h_ctx[:, 0] if rev else h_ctx[:, -1]
        bu_x = jnp.einsum('btgc,gpc->btgp', uxg, b_bar)
        first = n_lat - 1 if rev else 0
        bu_x = bu_x.at[:, first].add(a_bar * h0)
        h_lat = _linear_scan(a_bar, bu_x, rev)
        yx = yx + jnp.einsum('btgp,gcp->btgc', h_lat, cm).real.reshape(bsz, n_lat, SSM_W)
        if need_ctx:
            yc = yc + jnp.einsum('btgp,gcp->btgc', h_ctx, cm).real.reshape(bsz, n_ctx, SSM_W)
    return yx, yc


def _glu(y, w, b):
    y = jax.nn.gelu(y)
    return y * jax.nn.sigmoid(y @ w + b)


def _merge_branches(attn, ssm, ga, gs, w_pa, w_ps, w_o):
    m = jax.nn.sigmoid(ga.astype(jnp.float32)) * (attn @ w_pa) + jax.nn.sigmoid(gs.astype(jnp.float32)) * (ssm @ w_ps)
    return (m @ w_o).astype(ga.dtype)


def _token_mixer(hx, hc, lp, lam_init, need_ctx, rope_cos, rope_sin):
    f32 = jnp.float32
    bsz, n_lat, _ = hx.shape
    splits = [QK_W, QK_W + ATTN_W, KVU_W, KVU_W + QK_W, KVU_W + QK_W + D_MODEL]
    w_in = lp['w_in']
    kx, vx, ux, qx, gax, gsx = jnp.split(hx @ w_in, splits, axis=-1)
    if need_ctx:
        kc, vc, uc, qc, gac, gsc = jnp.split(hc @ w_in, splits, axis=-1)
    else:
        kc, vc, uc = jnp.split(hc @ w_in[:, :KVU_W], splits[:2], axis=-1)
    scale = HEAD_DIM ** -0.5

    def qk_heads(t):
        return t.astype(f32).reshape(bsz, t.shape[1], N_HEADS, 2, HEAD_DIM)

    def v_heads(t):
        return t.astype(f32).reshape(bsz, t.shape[1], N_HEADS, V_DIM)

    qx4 = _apply_axial_rope(qk_heads(qx), rope_cos, rope_sin) * scale
    kx4 = _apply_axial_rope(qk_heads(kx), rope_cos, rope_sin)
    kc4, vc4 = qk_heads(kc), v_heads(vc)
    k_all = jnp.concatenate([kc4, kx4], axis=1)
    v_all = jnp.concatenate([vc4, v_heads(vx)], axis=1)
    lq1, lk1 = lp['lam_q1'].astype(f32), lp['lam_k1'].astype(f32)
    lq2, lk2 = lp['lam_q2'].astype(f32), lp['lam_k2'].astype(f32)
    lam = jnp.exp(jnp.sum(lq1 * lk1)) - jnp.exp(jnp.sum(lq2 * lk2)) + lam_init
    n_blk = n_lat // Q_BLOCK
    q_blocks = qx4.reshape(bsz, n_blk, Q_BLOCK, N_HEADS, 2, HEAD_DIM).transpose(1, 0, 2, 3, 4, 5)
    o_x = lax.map(lambda qb: _diff_attention(qb, k_all, v_all, lam), q_blocks)
    o_x = o_x.transpose(1, 0, 2, 3, 4).reshape(bsz, n_lat, N_HEADS, V_DIM)
    attn_x = _head_norm(o_x, lp['subln_g'], lam_init)

    ssm_x, ssm_c = _s5_bidirectional(uc, ux, lp['ssm_a_re'], lp['ssm_a_im'], lp['ssm_log_dt'], lp['ssm_b_re'],
                                     lp['ssm_b_im'], lp['ssm_c_re'], lp['ssm_c_im'], lp['ssm_d'], need_ctx)
    yx = _merge_branches(attn_x, _glu(ssm_x, lp['w_glu'], lp['b_glu']), gax, gsx, lp['w_pa'], lp['w_ps'], lp['w_o'])
    yc = None
    if need_ctx:
        attn_c = _head_norm(_diff_attention(qk_heads(qc) * scale, kc4, vc4, lam), lp['subln_g'], lam_init)
        yc = _merge_branches(attn_c, _glu(ssm_c, lp['w_glu'], lp['b_glu']), gac, gsc, lp['w_pa'], lp['w_ps'], lp['w_o'])
    return yx, yc


def _hier_moe(h, wg, bg, we, be, w1, w3, w2):
    f32 = jnp.float32
    n_tok, d = h.shape
    hf = h.astype(f32)
    g_logits = hf @ wg.astype(f32) + bg.astype(f32)
    g_prob = jax.nn.softmax(g_logits, axis=-1)
    g_idx = jnp.argmax(g_logits, axis=-1).astype(jnp.int32)
    g_top = jnp.take_along_axis(g_prob, g_idx[:, None], axis=1)
    e_logits = (hf @ we.astype(f32) + be.astype(f32)).reshape(n_tok, N_GROUPS, EXPERTS_PER_GROUP)
    e_sel = jnp.take_along_axis(e_logits, g_idx[:, None, None], axis=1)[:, 0]
    top_val, top_idx = lax.top_k(e_sel, TOP_K)
    weights = jax.nn.softmax(top_val, axis=-1) * g_top
    expert = g_idx[:, None] * EXPERTS_PER_GROUP + top_idx.astype(jnp.int32)
    n_rows = n_tok * TOP_K
    n_blocks = (n_rows + N_EXPERTS * (MOE_BLOCK - 1) + MOE_BLOCK - 1) // MOE_BLOCK
    cap = n_blocks * MOE_BLOCK
    flat_e = expert.reshape(-1)
    flat_t = jnp.repeat(jnp.arange(n_tok, dtype=jnp.int32), TOP_K)
    flat_w = weights.reshape(-1)
    order = jnp.argsort(flat_e)
    se = flat_e[order]
    counts = jnp.bincount(flat_e, length=N_EXPERTS).astype(jnp.int32)
    padded = (counts + MOE_BLOCK - 1) // MOE_BLOCK * MOE_BLOCK
    pad_end = jnp.cumsum(padded)
    pad_start = pad_end - padded
    start = jnp.cumsum(counts) - counts
    dest = pad_start[se] + jnp.arange(n_rows, dtype=jnp.int32) - start[se]
    row_tok = jnp.full((cap,), n_tok, jnp.int32).at[dest].set(flat_t[order])
    row_w = jnp.zeros((cap,), f32).at[dest].set(flat_w[order])
    blk_expert = jnp.minimum(jnp.searchsorted(pad_end, jnp.arange(n_blocks, dtype=jnp.int32) * MOE_BLOCK, side='right'),
                             N_EXPERTS - 1).astype(jnp.int32)
    h_pad = jnp.concatenate([h, jnp.zeros((1, d), h.dtype)], axis=0)
    xs = h_pad[row_tok].reshape(n_blocks, MOE_BLOCK, d)

    def expert_block(args):
        xb, e = args
        return (jax.nn.silu(xb @ w1[e]) * (xb @ w3[e])) @ w2[e]

    ys = lax.map(expert_block, (xs, blk_expert)).reshape(cap, d)
    out = jnp.zeros((n_tok + 1, d), ys.dtype).at[row_tok].add(ys * row_w[:, None].astype(ys.dtype))
    return out[:n_tok]


def setup_inputs(seed: int = 0) -> dict:
    key = jax.random.key(seed)
    ks = jax.random.split(key, 40)
    f32 = jnp.float32

    def nrm(i, shape, scale):
        return jax.random.normal(ks[i], shape, f32) * scale

    D = D_MODEL
    G, P, E, F = SSM_GROUPS, SSM_STATE, N_EXPERTS, EXPERT_HIDDEN
    a_im = jnp.broadcast_to(jnp.pi * jnp.arange(P, dtype=f32), (DEPTH, 2, G, P))
    return {
        'x': nrm(0, (BATCH, SEQ, D), 1.0),
        'c': nrm(1, (BATCH, D), 1.0),
        'ctx': nrm(2, (BATCH, CTX_LEN, D), 1.0),
        'c_ctx': nrm(3, (D,), 1.0),
        'w_mod': nrm(4, (DEPTH, D, 6 * D), MOD_INIT * D ** -0.5),
        'b_mod': nrm(5, (DEPTH, 6 * D), 0.01),
        'w_in': nrm(6, (DEPTH, D, IN_W), D ** -0.5),
        'lam_q1': nrm(7, (DEPTH, HEAD_DIM), 0.1),
        'lam_k1': nrm(8, (DEPTH, HEAD_DIM), 0.1),
        'lam_q2': nrm(9, (DEPTH, HEAD_DIM), 0.1),
        'lam_k2': nrm(10, (DEPTH, HEAD_DIM), 0.1),
        'subln_g': 1.0 + nrm(11, (DEPTH, V_DIM), 0.02),
        'ssm_a_re': -0.5 * (1.0 + nrm(12, (DEPTH, 2, G, P), 0.01)),
        'ssm_a_im': a_im,
        'ssm_log_dt': jax.random.uniform(ks[13], (DEPTH, 2, G), f32, math.log(1e-3), math.log(1e-1)),
        'ssm_b_re': nrm(14, (DEPTH, 2, G, P, SSM_GROUP), (2 * SSM_GROUP) ** -0.5),
        'ssm_b_im': nrm(15, (DEPTH, 2, G, P, SSM_GROUP), (2 * SSM_GROUP) ** -0.5),
        'ssm_c_re': nrm(16, (DEPTH, 2, G, SSM_GROUP, P), P ** -0.5),
        'ssm_c_im': nrm(17, (DEPTH, 2, G, SSM_GROUP, P), P ** -0.5),
        'ssm_d': nrm(18, (DEPTH, SSM_W), 0.5),
        'w_glu': nrm(19, (DEPTH, SSM_W, SSM_W), SSM_W ** -0.5),
        'b_glu': nrm(20, (DEPTH, SSM_W), 0.01),
        'w_pa': nrm(21, (DEPTH, ATTN_W, D), ATTN_W ** -0.5),
        'w_ps': nrm(22, (DEPTH, SSM_W, D), SSM_W ** -0.5),
        'w_o': nrm(23, (DEPTH, D, D), BETA * D ** -0.5),
        'ln1_g': 1.0 + nrm(24, (DEPTH, D), 0.02),
        'ln1_b': nrm(25, (DEPTH, D), 0.01),
        'router_g_w': nrm(26, (DEPTH, D, N_GROUPS), D ** -0.5),
        'router_g_b': nrm(27, (DEPTH, N_GROUPS), 0.01),
        'router_e_w': nrm(28, (DEPTH, D, E), D ** -0.5),
        'router_e_b': nrm(29, (DEPTH, E), 0.01),
        'moe_w1': nrm(30, (DEPTH, E, D, F), D ** -0.5),
        'moe_w3': nrm(31, (DEPTH, E, D, F), D ** -0.5),
        'moe_w2': nrm(32, (DEPTH, E, F, D), BETA * F ** -0.5),
        'ln2_g': 1.0 + nrm(33, (DEPTH, D), 0.02),
        'ln2_b': nrm(34, (DEPTH, D), 0.01),
    }


def reference(x, c, ctx, c_ctx, w_mod, b_mod, w_in, lam_q1, lam_k1, lam_q2, lam_k2, subln_g,
              ssm_a_re, ssm_a_im, ssm_log_dt, ssm_b_re, ssm_b_im, ssm_c_re, ssm_c_im, ssm_d,
              w_glu, b_glu, w_pa, w_ps, w_o, ln1_g, ln1_b,
              router_g_w, router_g_b, router_e_w, router_e_b, moe_w1, moe_w3, moe_w2, ln2_g, ln2_b):
    bsz, n_lat, _ = x.shape
    rope_cos, rope_sin = _axial_rope_tables(n_lat)
    silu_c = jax.nn.silu(c)
    silu_cc = jax.nn.silu(c_ctx)
    xl, xc = x, ctx
    for l in range(DEPTH):
        last = l == DEPTH - 1
        lam_init = 0.8 - 0.6 * math.exp(-0.3 * l)
        lp = dict(w_in=w_in[l], lam_q1=lam_q1[l], lam_k1=lam_k1[l], lam_q2=lam_q2[l], lam_k2=lam_k2[l],
                  subln_g=subln_g[l], ssm_a_re=ssm_a_re[l], ssm_a_im=ssm_a_im[l], ssm_log_dt=ssm_log_dt[l],
                  ssm_b_re=ssm_b_re[l], ssm_b_im=ssm_b_im[l], ssm_c_re=ssm_c_re[l], ssm_c_im=ssm_c_im[l],
                  ssm_d=ssm_d[l], w_glu=w_glu[l], b_glu=b_glu[l], w_pa=w_pa[l], w_ps=w_ps[l], w_o=w_o[l])
        mod_x = (silu_c @ w_mod[l] + b_mod[l])[:, None, :]
        sh1, sc1, g1, sh2, sc2, g2 = jnp.split(mod_x, 6, axis=-1)
        n_mod_c = 2 if last else 6
        cmod = jnp.split(silu_cc @ w_mod[l][:, :n_mod_c * D_MODEL] + b_mod[l][:n_mod_c * D_MODEL], n_mod_c)
        hx = xl * (1.0 + sc1) + sh1
        hc = xc * (1.0 + cmod[1]) + cmod[0]
        yx, yc = _token_mixer(hx, hc, lp, lam_init, not last, rope_cos, rope_sin)
        xl = _layer_norm(ALPHA * xl + g1 * yx, ln1_g[l], ln1_b[l])
        hx = xl * (1.0 + sc2) + sh2
        if last:
            fx = _hier_moe(hx.reshape(-1, D_MODEL), router_g_w[l], router_g_b[l], router_e_w[l], router_e_b[l],
                           moe_w1[l], moe_w3[l], moe_w2[l]).reshape(hx.shape)
        else:
            xc = _layer_norm(ALPHA * xc + cmod[2] * yc, ln1_g[l], ln1_b[l])
            hc = xc * (1.0 + cmod[4]) + cmod[3]
            rows = jnp.concatenate([hx.reshape(-1, D_MODEL), hc.reshape(-1, D_MODEL)], axis=0)
            f = _hier_moe(rows, router_g_w[l], router_g_b[l], router_e_w[l], router_e_b[l],
                          moe_w1[l], moe_w3[l], moe_w2[l])
            fx = f[:bsz * n_lat].reshape(hx.shape)
            fc = f[bsz * n_lat:].reshape(hc.shape)
            xc = _layer_norm(ALPHA * xc + cmod[5] * fc, ln2_g[l], ln2_b[l])
        xl = _layer_norm(ALPHA * xl + g2 * fx, ln2_g[l], ln2_b[l])
    return xl
```

```python
import functools
import math

import jax
import jax.numpy as jnp
from jax import lax
from jax.experimental import pallas as pl
from jax.experimental.pallas import tpu as pltpu

F32 = jnp.float32
BF16 = jnp.bfloat16

D_MODEL = 1024
N_HEADS = 8
HEAD_DIM = 64
V_DIM = 2 * HEAD_DIM
QK_W = N_HEADS * 2 * HEAD_DIM
ATTN_W = N_HEADS * V_DIM
SSM_W = D_MODEL // 2
SSM_GROUP = 16
SSM_GROUPS = SSM_W // SSM_GROUP
SSM_STATE = 64
N_GROUPS = 4
EXPERTS_PER_GROUP = 8
N_EXPERTS = N_GROUPS * EXPERTS_PER_GROUP
EXPERT_HIDDEN = D_MODEL // 2
KVU_W = QK_W + ATTN_W + SSM_W
IN_W = KVU_W + QK_W + 2 * D_MODEL
GRID_W = 64
ROPE_BASE = 10000.0
ROPE_AXIS_DIM = HEAD_DIM // 2
LN_EPS = 1e-5

LANES = 128
SUBLANES = 8
VMEM_LIMIT = 56 * 1024 * 1024

ROW_TILE = 256
SCAN_CHUNK = 8
SCAN_TIME_TILE = 256
MOE_BLOCK = 256
ROUTE_TILE = 256
LANE_TILES = SSM_W // LANES
GROUPS_PER_TILE = LANES // SSM_GROUP
STATE_W = GROUPS_PER_TILE * SSM_STATE
NEG_BIG = -3.0e38


def _sigmoid(x):
    return 1.0 / (1.0 + jnp.exp(-x))


def _layer_norm(x, g, b):
    xc = x - jnp.mean(x, axis=-1, keepdims=True)
    var = jnp.mean(xc * xc, axis=-1, keepdims=True)
    return xc * lax.rsqrt(var + LN_EPS) * g + b


def _mod_kernel(c_ref, w_ref, b_ref, o_ref):
    c = c_ref[...]
    s = c * _sigmoid(c)
    o_ref[...] = jnp.dot(s, w_ref[...], preferred_element_type=F32, precision=lax.Precision.HIGHEST) + b_ref[...]


def _modulation(cvec, w_mod, b_mod):
    depth, d, w6 = w_mod.shape
    rows = cvec.shape[0]
    tn = 1024
    return pl.pallas_call(
        _mod_kernel,
        grid=(depth, w6 // tn),
        in_specs=[
            pl.BlockSpec((rows, d), lambda l, j: (0, 0)),
            pl.BlockSpec((None, d, tn), lambda l, j: (l, 0, j)),
            pl.BlockSpec((None, 1, tn), lambda l, j: (l, 0, j)),
        ],
        out_specs=pl.BlockSpec((None, rows, tn), lambda l, j: (l, 0, j)),
        out_shape=jax.ShapeDtypeStruct((depth, rows, w6), F32),
        name="modulation",
    )(cvec, w_mod, b_mod.reshape(depth, 1, w6))


def _inproj_kernel(x_ref, sh_ref, sc_ref, cos_ref, sa_ref, sb_ref, w_ref,
                   k_ref, v_ref, u_ref, q_ref, ga_ref, gs_ref):
    h = (x_ref[...] * (1.0 + sc_ref[...]) + sh_ref[...]).astype(BF16)
    cos, sa, sb = cos_ref[...], sa_ref[...], sb_ref[...]

    def rope(t):
        return t * cos + pltpu.roll(t, LANES - 16, 1) * sa + pltpu.roll(t, 16, 1) * sb

    def proj(lo, hi):
        return jnp.dot(h, w_ref[:, lo:hi], preferred_element_type=F32)

    kk = proj(0, QK_W)
    for c in range(QK_W // LANES):
        k_ref[:, c * LANES:(c + 1) * LANES] = rope(kk[:, c * LANES:(c + 1) * LANES]).astype(BF16)
    v_ref[...] = proj(QK_W, QK_W + ATTN_W).astype(BF16)
    u = proj(QK_W + ATTN_W, KVU_W)
    u_ref[...] = u.reshape(u_ref.shape)
    qq = proj(KVU_W, KVU_W + QK_W)
    scale = HEAD_DIM ** -0.5
    for c in range(QK_W // LANES):
        q_ref[:, c * LANES:(c + 1) * LANES] = (rope(qq[:, c * LANES:(c + 1) * LANES]) * scale).astype(BF16)
    ga_ref[...] = _sigmoid(proj(KVU_W + QK_W, KVU_W + QK_W + D_MODEL)).astype(BF16)
    gs_ref[...] = _sigmoid(proj(KVU_W + QK_W + D_MODEL, IN_W)).astype(BF16)


def _mod_spec(col, bsz, n_ctx_tiles, tile0):
    return pl.BlockSpec((None, 1, D_MODEL),
                        lambda b, i: (jnp.where(i + tile0 < n_ctx_tiles, bsz, b), 0, col))


def _inproj(xall, mods3, tabs, w_in_bf, n_ctx):
    bsz, t_all, d = xall.shape
    tm = ROW_TILE
    n_ctx_tiles = n_ctx // tm
    n_tiles = t_all // tm
    lc = SCAN_CHUNK
    tok_spec = lambda w: pl.BlockSpec((None, tm, w), lambda b, i: (b, i, 0))
    tab_spec = pl.BlockSpec((tm, LANES), lambda b, i: (i, 0))
    big = lambda w: jax.ShapeDtypeStruct((bsz, t_all, w), BF16)
    return pl.pallas_call(
        _inproj_kernel,
        grid=(bsz, n_tiles),
        in_specs=[
            tok_spec(d),
            _mod_spec(0, bsz, n_ctx_tiles, 0),
            _mod_spec(1, bsz, n_ctx_tiles, 0),
            tab_spec, tab_spec, tab_spec,
            pl.BlockSpec((d, IN_W), lambda b, i: (0, 0)),
        ],
        out_specs=[
            tok_spec(QK_W), tok_spec(ATTN_W),
            pl.BlockSpec((tm // lc, None, lc, SSM_W), lambda b, i: (i, b, 0, 0)),
            tok_spec(QK_W), tok_spec(D_MODEL), tok_spec(D_MODEL),
        ],
        out_shape=[
            big(QK_W), big(ATTN_W),
            jax.ShapeDtypeStruct((t_all // lc, bsz, lc, SSM_W), F32),
            big(QK_W), big(D_MODEL), big(D_MODEL),
        ],
        compiler_params=pltpu.CompilerParams(
            dimension_semantics=("parallel", "arbitrary"), vmem_limit_bytes=VMEM_LIMIT),
        name="inproj",
    )(xall, mods3, mods3, tabs[0], tabs[1], tabs[2], w_in_bf)


def _attn_kernel(q_ref, k_ref, v_ref, lq1_ref, lk1_ref, lq2_ref, lk2_ref, g_ref, o_ref,
                 *, n_ctx, n_ctx_tiles, tile0, lam_init):
    i = pl.program_id(2) + tile0
    lam = (jnp.exp(jnp.sum(lq1_ref[...] * lk1_ref[...], axis=1, keepdims=True))
           - jnp.exp(jnp.sum(lq2_ref[...] * lk2_ref[...], axis=1, keepdims=True)) + lam_init)
    q = q_ref[...]
    lane = lax.broadcasted_iota(jnp.int32, q.shape, 1)
    zero = jnp.zeros_like(q)
    q1 = jnp.where(lane < HEAD_DIM, q, zero)
    q2 = jnp.where(lane < HEAD_DIM, zero, q)
    g = g_ref[...]

    def attend(n_kv):
        k = k_ref[0:n_kv, :]
        v = v_ref[0:n_kv, :]

        def one(qm):
            s = lax.dot_general(qm, k, (((1,), (1,)), ((), ())), preferred_element_type=F32)
            p = jnp.exp(s - jnp.max(s, axis=-1, keepdims=True))
            l = jnp.sum(p, axis=-1, keepdims=True)
            return jnp.dot(p.astype(BF16), v, preferred_element_type=F32) / l

        o = one(q1) - lam * one(q2)
        o = o * lax.rsqrt(jnp.mean(o * o, axis=-1, keepdims=True) + LN_EPS) * g * (1.0 - lam_init)
        o_ref[...] = o.astype(BF16)

    if n_ctx_tiles > tile0:
        @pl.when(i < n_ctx_tiles)
        def _():
            attend(n_ctx)

        @pl.when(i >= n_ctx_tiles)
        def _():
            attend(k_ref.shape[0])
    else:
        attend(k_ref.shape[0])


def _attention(q, k, v, lam_vecs, subln_g, n_ctx, tile0, lam_init):
    bsz, t_all, _ = q.shape
    tq = ROW_TILE
    n_tiles = t_all // tq - tile0
    vec = pl.BlockSpec((1, HEAD_DIM), lambda b, h, i: (0, 0))
    kern = functools.partial(_attn_kernel, n_ctx=n_ctx, n_ctx_tiles=n_ctx // tq, tile0=tile0, lam_init=lam_init)
    return pl.pallas_call(
        kern,
        grid=(bsz, N_HEADS, n_tiles),
        in_specs=[
            pl.BlockSpec((None, tq, V_DIM), lambda b, h, i: (b, i + tile0, h)),
            pl.BlockSpec((None, t_all, V_DIM), lambda b, h, i: (b, 0, h)),
            pl.BlockSpec((None, t_all, V_DIM), lambda b, h, i: (b, 0, h)),
            vec, vec, vec, vec,
            pl.BlockSpec((1, V_DIM), lambda b, h, i: (0, 0)),
        ],
        out_specs=pl.BlockSpec((None, tq, V_DIM), lambda b, h, i: (b, i, h)),
        out_shape=jax.ShapeDtypeStruct((bsz, n_tiles * tq, ATTN_W), BF16),
        compiler_params=pltpu.CompilerParams(
            dimension_semantics=("parallel", "parallel", "arbitrary"), vmem_limit_bytes=VMEM_LIMIT),
        name="diff_attention",
    )(q, k, v, *lam_vecs, subln_g)


def _ssm_operators(a_re, a_im, log_dt, b_re, b_im, c_re, c_im, d_skip):
    lc = SCAN_CHUNK
    lam = lax.complex(a_re.astype(F32), a_im.astype(F32))
    dt = jnp.exp(log_dt.astype(F32))[..., None]
    ldt = lam * dt
    a_bar = jnp.exp(ldt)
    b_bar = ((a_bar - 1.0) / lam)[..., None] * lax.complex(b_re.astype(F32), b_im.astype(F32))
    cm = lax.complex(c_re.astype(F32), c_im.astype(F32))
    steps = jnp.arange(lc + 1, dtype=F32)
    apow = jnp.exp(ldt[None] * steps[:, None, None, None])
    eye = jnp.eye(GROUPS_PER_TILE, dtype=F32)
    s_idx = jnp.arange(lc)
    lag_f = s_idx[None, :] - s_idx[:, None]
    lk = lc * LANES
    intra, inject, readout = [], [], []
    for di in range(2):
        kern = jnp.einsum('gcp,jgp,gpi->jgci', cm[di], apow[:lc, di], b_bar[di]).real
        lag = lag_f if di == 0 else -lag_f
        toe = jnp.where((lag >= 0)[:, :, None, None, None], kern[jnp.clip(lag, 0, lc - 1)], 0.0)
        toe = toe.reshape(lc, lc, LANE_TILES, GROUPS_PER_TILE, SSM_GROUP, SSM_GROUP)
        op = jnp.einsum('stjgci,gh->jsgithc', toe, eye).reshape(LANE_TILES, lk, lk)
        if di == 0:
            dd = jnp.tile(d_skip.astype(F32).reshape(LANE_TILES, 1, LANES), (1, lc, 1)).reshape(LANE_TILES, lk)
            op = op + dd[:, :, None] * jnp.eye(lk, dtype=F32)[None]
        intra.append(op)
        pw_in = apow[lc - 1 - s_idx, di] if di == 0 else apow[s_idx, di]
        w = pw_in[..., None] * b_bar[di][None]
        w = jnp.stack([w.real, w.imag]).reshape(2, lc, LANE_TILES, GROUPS_PER_TILE, SSM_STATE, SSM_GROUP)
        inject.append(jnp.einsum('rsjgpi,gh->jsgirhp', w, eye).reshape(LANE_TILES, lk, 2 * STATE_W))
        pw_out = apow[s_idx + 1, di] if di == 0 else apow[lc - s_idx, di]
        vv = cm[di][None] * pw_out[:, :, None, :]
        vv = jnp.stack([vv.real, -vv.imag]).reshape(2, lc, LANE_TILES, GROUPS_PER_TILE, SSM_GROUP, SSM_STATE)
        readout.append(jnp.einsum('rtjgcp,gh->jrgpthc', vv, eye).reshape(LANE_TILES, 2 * STATE_W, lk))
    al = apow[lc].reshape(2, LANE_TILES, 1, STATE_W)
    return (jnp.stack(intra).astype(BF16), jnp.stack(inject).astype(BF16), jnp.stack(readout).astype(BF16),
            al.real, al.imag)


def _ssm_kernel(u_ref, a_ref, bc_ref, wc_ref, alr_ref, ali_ref, y_ref, s_scr, h_scr, st_scr, *, bsz):
    ph = pl.program_id(1)
    ti = pl.program_id(2)
    lc = SCAN_CHUNK
    n_rows = s_scr.shape[0]
    n_chunks = n_rows // bsz

    @pl.when(ti == 0)
    def _():
        st_scr[...] = jnp.zeros_like(st_scr)

    x = jnp.concatenate([u_ref[pl.ds(s, n_rows, stride=lc), :] for s in range(lc)], axis=1).astype(BF16)
    s_scr[...] = jnp.dot(x, bc_ref[...], preferred_element_type=F32)
    alr, ali = alr_ref[...], ali_ref[...]

    def step(c, carry):
        hr, hi = carry
        ce = jnp.where(ph == 0, c, n_chunks - 1 - c)
        r0 = pl.multiple_of(ce * bsz, bsz)
        h_scr[pl.ds(r0, bsz), 0:STATE_W] = hr
        h_scr[pl.ds(r0, bsz), STATE_W:2 * STATE_W] = hi
        sr = s_scr[pl.ds(r0, bsz), 0:STATE_W]
        si = s_scr[pl.ds(r0, bsz), STATE_W:2 * STATE_W]
        return alr * hr - ali * hi + sr, alr * hi + ali * hr + si

    hr, hi = lax.fori_loop(0, n_chunks, step, (st_scr[0], st_scr[1]))
    st_scr[0] = hr
    st_scr[1] = hi
    y = (jnp.dot(x, a_ref[...], preferred_element_type=F32)
         + jnp.dot(h_scr[...].astype(BF16), wc_ref[...], preferred_element_type=F32))
    for s in range(lc):
        y_ref[pl.ds(s, n_rows, stride=lc), :] = y[:, s * LANES:(s + 1) * LANES]


def _ssm(u4, ops, n_ctx):
    n_chunks_all, bsz, lc, _ = u4.shape
    t_all = n_chunks_all * lc
    tt = SCAN_TIME_TILE
    n_t = t_all // tt
    n_ctx_t = n_ctx // tt
    rows_tile = tt * bsz
    u2 = u4.reshape(t_all * bsz, SSM_W)
    intra, inject, readout, alr, ali = ops
    lk = lc * LANES

    def tile_of(ph, i):
        rev = jnp.where(i < n_ctx_t, n_ctx_t - 1 - i, n_t - 1 - (i - n_ctx_t))
        return jnp.where(ph == 0, i, rev)

    op_spec = lambda r, c: pl.BlockSpec((None, None, r, c), lambda j, ph, i: (ph, j, 0, 0))
    kern = functools.partial(_ssm_kernel, bsz=bsz)
    y = pl.pallas_call(
        kern,
        grid=(LANE_TILES, 2, n_t),
        in_specs=[
            pl.BlockSpec((rows_tile, LANES), lambda j, ph, i: (tile_of(ph, i), j)),
            op_spec(lk, lk), op_spec(lk, 2 * STATE_W), op_spec(2 * STATE_W, lk),
            op_spec(1, STATE_W), op_spec(1, STATE_W),
        ],
        out_specs=pl.BlockSpec((None, rows_tile, LANES), lambda j, ph, i: (ph, tile_of(ph, i), j)),
        out_shape=jax.ShapeDtypeStruct((2, t_all * bsz, SSM_W), F32),
        scratch_shapes=[
            pltpu.VMEM((rows_tile // lc, 2 * STATE_W), F32),
            pltpu.VMEM((rows_tile // lc, 2 * STATE_W), F32),
            pltpu.VMEM((2, bsz, STATE_W), F32),
        ],
        compiler_params=pltpu.CompilerParams(
            dimension_semantics=("parallel", "arbitrary", "arbitrary"), vmem_limit_bytes=VMEM_LIMIT),
        name="s5_scan",
    )(u2, intra, inject, readout, alr, ali)
    return y.reshape(2, n_chunks_all, bsz, lc, SSM_W)


def _merge_kernel(attn_ref, y_ref, ga_ref, gs_ref, x_ref, g1_ref, sh2_ref, sc2_ref,
                  wpa_ref, wglu_ref, bglu_ref, wps_ref, wo_ref, lng_ref, lnb_ref, wr_ref, br_ref,
                  x1_ref, h2_ref, lg_ref, *, alpha):
    a = jnp.dot(attn_ref[...], wpa_ref[...], preferred_element_type=F32)
    ys = (y_ref[0] + y_ref[1]).reshape(x_ref.shape[0], SSM_W)
    gl = ys * (0.5 * (1.0 + jnp.tanh(math.sqrt(2.0 / math.pi) * (ys + 0.044715 * (ys * ys * ys)))))
    z = jnp.dot(gl.astype(BF16), wglu_ref[...], preferred_element_type=F32) + bglu_ref[...]
    sg = gl * _sigmoid(z)
    s = jnp.dot(sg.astype(BF16), wps_ref[...], preferred_element_type=F32)
    m = ga_ref[...].astype(F32) * a + gs_ref[...].astype(F32) * s
    y = jnp.dot(m.astype(BF16), wo_ref[...], preferred_element_type=F32)
    x1 = _layer_norm(alpha * x_ref[...] + g1_ref[...] * y, lng_ref[...], lnb_ref[...])
    x1_ref[...] = x1
    h2 = x1 * (1.0 + sc2_ref[...]) + sh2_ref[...]
    h2_ref[...] = h2
    lg_ref[...] = jnp.dot(h2, wr_ref[...], preferred_element_type=F32,
                          precision=lax.Precision.HIGHEST) + br_ref[...]


def _merge(attn, y5, siga, sigs, xall, mods3, wts, n_ctx, tile0, alpha):
    bsz, t_all, d = xall.shape
    tm = ROW_TILE
    lc = SCAN_CHUNK
    n_ctx_tiles = n_ctx // tm
    n_tiles = t_all // tm - tile0
    t_out = n_tiles * tm
    tok = lambda w: pl.BlockSpec((None, tm, w), lambda b, i: (b, i + tile0, 0))
    own = lambda w: pl.BlockSpec((None, tm, w), lambda b, i: (b, i, 0))
    modv = lambda col: _mod_spec(col, bsz, n_ctx_tiles, tile0)
    full = lambda arr: pl.BlockSpec(arr.shape, lambda b, i: (0,) * arr.ndim)
    wpa, wglu, bglu, wps, wo, lng, lnb, wr, br = wts
    kern = functools.partial(_merge_kernel, alpha=alpha)
    return pl.pallas_call(
        kern,
        grid=(bsz, n_tiles),
        in_specs=[
            own(ATTN_W),
            pl.BlockSpec((2, tm // lc, None, lc, SSM_W), lambda b, i: (0, i + tile0, b, 0, 0)),
            tok(D_MODEL), tok(D_MODEL), tok(d),
            modv(2), modv(3), modv(4),
            full(wpa), full(wglu), full(bglu), full(wps), full(wo), full(lng), full(lnb), full(wr), full(br),
        ],
        out_specs=[
            own(d),
            pl.BlockSpec((tm, d), lambda b, i: (b * n_tiles + i, 0)),
            pl.BlockSpec((tm, LANES), lambda b, i: (b * n_tiles + i, 0)),
        ],
        out_shape=[
            jax.ShapeDtypeStruct((bsz, t_out, d), F32),
            jax.ShapeDtypeStruct((bsz * t_out, d), F32),
            jax.ShapeDtypeStruct((bsz * t_out, LANES), F32),
        ],
        compiler_params=pltpu.CompilerParams(
            dimension_semantics=("parallel", "arbitrary"), vmem_limit_bytes=VMEM_LIMIT),
        name="merge_ln1",
    )(attn, y5, siga, sigs, xall, mods3, mods3, mods3, wpa, wglu, bglu, wps, wo, lng, lnb, wr, br)


def _route_kernel(lg_ref, rec_ref, blk_ref, cnt_scr, start_scr, carry_scr):
    ph = pl.program_id(0)
    i = pl.program_id(1)
    lg = lg_ref[...]
    shape = lg.shape
    lane = lax.broadcasted_iota(jnp.int32, shape, 1).astype(F32)
    far = jnp.full(shape, 1.0e9, F32)

    def first_max(vals, mask):
        vm = jnp.where(mask, vals, NEG_BIG)
        mx = jnp.max(vm, axis=1, keepdims=True)
        idx = jnp.min(jnp.where(mask & (vm == mx), lane, far), axis=1, keepdims=True)
        return mx, idx

    gmask = lane < N_GROUPS
    gmax, gidx = first_max(lg, gmask)
    gtop = 1.0 / jnp.sum(jnp.where(gmask, jnp.exp(lg - gmax), 0.0), axis=1, keepdims=True)
    lo = N_GROUPS + EXPERTS_PER_GROUP * gidx
    emask = (lane >= lo) & (lane < lo + EXPERTS_PER_GROUP)
    v1, i1 = first_max(lg, emask)
    v2, i2 = first_max(lg, emask & (lane != i1))
    e2 = jnp.exp(v2 - v1)
    den = 1.0 + e2
    w1 = (1.0 / den) * gtop
    w2 = (e2 / den) * gtop
    oh1 = (lane == i1 - N_GROUPS).astype(F32)
    oh2 = (lane == i2 - N_GROUPS).astype(F32)
    c1 = jnp.sum(oh1, axis=0, keepdims=True)
    c2 = jnp.sum(oh2, axis=0, keepdims=True)

    @pl.when((ph == 0) & (i == 0))
    def _():
        cnt_scr[...] = jnp.zeros_like(cnt_scr)

    @pl.when(ph == 0)
    def _():
        cnt_scr[...] += c1 + c2

    @pl.when((ph == 1) & (i == 0))
    def _():
        cnt = cnt_scr[...]
        padded = jnp.floor((cnt + (MOE_BLOCK - 1)) * (1.0 / MOE_BLOCK)) * MOE_BLOCK
        r = lax.broadcasted_iota(jnp.int32, (LANES, LANES), 0)
        c = lax.broadcasted_iota(jnp.int32, (LANES, LANES), 1)
        upper = (r < c).astype(F32)
        start = jnp.dot(jnp.broadcast_to(padded, (SUBLANES, LANES)), upper, preferred_element_type=F32,
                        precision=lax.Precision.HIGHEST)[0:1]
        start_scr[...] = start
        carry_scr[...] = jnp.zeros_like(carry_scr)
        nb = blk_ref.shape[0]
        blk_start = (lax.broadcasted_iota(jnp.int32, (nb, LANES), 0) * MOE_BLOCK).astype(F32)
        elane = lax.broadcasted_iota(jnp.int32, (nb, LANES), 1) < N_EXPERTS
        done = jnp.sum(jnp.where(elane & ((start + padded) <= blk_start), 1.0, 0.0), axis=1, keepdims=True)
        blk_ref[...] = jnp.broadcast_to(jnp.minimum(done, N_EXPERTS - 1.0), (nb, LANES)).astype(jnp.int32)

    @pl.when(ph == 1)
    def _():
        tr = shape[0]
        r = lax.broadcasted_iota(jnp.int32, (tr, tr), 0)
        c = lax.broadcasted_iota(jnp.int32, (tr, tr), 1)
        tri = (c < r).astype(BF16)
        base = start_scr[...] + carry_scr[...]
        r1 = jnp.dot(tri, oh1.astype(BF16), preferred_element_type=F32)
        r2 = jnp.dot(tri, oh2.astype(BF16), preferred_element_type=F32) + c1
        d1 = jnp.sum(oh1 * (base + r1), axis=1, keepdims=True)
        d2 = jnp.sum(oh2 * (base + r2), axis=1, keepdims=True)
        carry_scr[...] += c1 + c2
        rec_ref[...] = jnp.where(lane == 0.0, d1, jnp.where(lane == 1.0, d2, jnp.where(lane == 2.0, w1, w2)))


def _route(logits, n_blocks):
    n_tok = logits.shape[0]
    tr = ROUTE_TILE
    nb_pad = -(-n_blocks // SUBLANES) * SUBLANES
    return pl.pallas_call(
        _route_kernel,
        grid=(2, n_tok // tr),
        in_specs=[pl.BlockSpec((tr, LANES), lambda ph, i: (i, 0))],
        out_specs=[
            pl.BlockSpec((tr, LANES), lambda ph, i: (i * ph, 0)),
            pl.BlockSpec((nb_pad, LANES), lambda ph, i: (0, 0)),
        ],
        out_shape=[
            jax.ShapeDtypeStruct((n_tok, LANES), F32),
            jax.ShapeDtypeStruct((nb_pad, LANES), jnp.int32),
        ],
        scratch_shapes=[pltpu.VMEM((1, LANES), F32)] * 3,
        compiler_params=pltpu.CompilerParams(dimension_semantics=("arbitrary", "arbitrary")),
        name="moe_route",
    )(logits)


def _row_copy(src, dst, sem):
    return pltpu.make_async_copy(src, dst, sem)


def _dispatch_kernel(slot_ref, h_ref, xs_in_ref, xs_ref, sem):
    del xs_in_ref
    tg = h_ref.shape[0]

    def issue(r, carry):
        for k in range(2):
            d = slot_ref[0, 0, k * tg + r]
            _row_copy(h_ref.at[pl.ds(r, 1)], xs_ref.at[pl.ds(d, 1)], sem).start()
        return carry

    lax.fori_loop(0, tg, issue, 0)

    def drain(r, carry):
        for k in range(2):
            _row_copy(h_ref.at[pl.ds(0, 1)], xs_ref.at[pl.ds(0, 1)], sem).wait()
        return carry

    lax.fori_loop(0, tg, drain, 0)


def _dispatch(h2, slots3, cap):
    n_tok, d = h2.shape
    tg = ROW_TILE
    xs0 = jnp.zeros((cap, d), F32)
    return pl.pallas_call(
        _dispatch_kernel,
        grid=(n_tok // tg,),
        in_specs=[
            pl.BlockSpec((1, 1, 2 * tg), lambda i: (i, 0, 0), memory_space=pltpu.SMEM),
            pl.BlockSpec((tg, d), lambda i: (i, 0)),
            pl.BlockSpec(memory_space=pl.ANY),
        ],
        out_specs=pl.BlockSpec(memory_space=pl.ANY),
        out_shape=jax.ShapeDtypeStruct((cap, d), F32),
        scratch_shapes=[pltpu.SemaphoreType.DMA(())],
        input_output_aliases={2: 0},
        compiler_params=pltpu.CompilerParams(dimension_semantics=("arbitrary",), has_side_effects=True),
        name="moe_dispatch",
    )(slots3, h2, xs0)


def _expert_kernel(be_ref, xs_ref, w1_ref, w3_ref, w2_ref, ys_ref, w1b, w3b, w2b):
    i = pl.program_id(0)
    fresh = (i == 0) | (be_ref[i] != be_ref[jnp.maximum(i - 1, 0)])

    @pl.when(fresh)
    def _():
        w1b[...] = w1_ref[...].astype(BF16)
        w3b[...] = w3_ref[...].astype(BF16)
        w2b[...] = w2_ref[...].astype(BF16)

    x = xs_ref[...].astype(BF16)
    a = jnp.dot(x, w1b[...], preferred_element_type=F32)
    b = jnp.dot(x, w3b[...], preferred_element_type=F32)
    hid = (a * _sigmoid(a)) * b
    ys_ref[...] = jnp.dot(hid.astype(BF16), w2b[...], preferred_element_type=F32)


def _experts(blk_expert, xs, w1, w3, w2):
    cap, d = xs.shape
    f = w1.shape[-1]
    n_blocks = cap // MOE_BLOCK
    grid_spec = pltpu.PrefetchScalarGridSpec(
        num_scalar_prefetch=1,
        grid=(n_blocks,),
        in_specs=[
            pl.BlockSpec((MOE_BLOCK, d), lambda i, be: (i, 0)),
            pl.BlockSpec((None, d, f), lambda i, be: (be[i], 0, 0)),
            pl.BlockSpec((None, d, f), lambda i, be: (be[i], 0, 0)),
            pl.BlockSpec((None, f, d), lambda i, be: (be[i], 0, 0)),
        ],
        out_specs=pl.BlockSpec((MOE_BLOCK, d), lambda i, be: (i, 0)),
        scratch_shapes=[pltpu.VMEM((d, f), BF16), pltpu.VMEM((d, f), BF16), pltpu.VMEM((f, d), BF16)],
    )
    return pl.pallas_call(
        _expert_kernel,
        grid_spec=grid_spec,
        out_shape=jax.ShapeDtypeStruct((cap, d), F32),
        compiler_params=pltpu.CompilerParams(
            dimension_semantics=("arbitrary",), vmem_limit_bytes=VMEM_LIMIT),
        name="moe_experts",
    )(blk_expert, xs, w1, w3, w2)


def _combine_kernel(slot_ref, rec_ref, x_ref, g2_ref, lng_ref, lnb_ref, ys_ref, o_ref, buf, sem, *, alpha):
    tm = x_ref.shape[0]

    def issue(r, carry):
        for k in range(2):
            d = slot_ref[0, 0, k * tm + r]
            _row_copy(ys_ref.at[pl.ds(d, 1)], buf.at[k, pl.ds(r, 1)], sem).start()
        return carry

    lax.fori_loop(0, tm, issue, 0)

    def drain(r, carry):
        for k in range(2):
            _row_copy(ys_ref.at[pl.ds(0, 1)], buf.at[0, pl.ds(0, 1)], sem).wait()
        return carry

    lax.fori_loop(0, tm, drain, 0)
    rec = rec_ref[...]
    f = rec[:, 2:3] * buf[0] + rec[:, 3:4] * buf[1]
    o_ref[...] = _layer_norm(alpha * x_ref[...] + g2_ref[...] * f, lng_ref[...], lnb_ref[...])


def _combine(slots3, rec, x1, mods3, lng, lnb, ys, n_ctx, tile0, alpha):
    bsz, t_out, d = x1.shape
    tm = ROW_TILE
    n_ctx_tiles = n_ctx // tm
    n_tiles = t_out // tm
    kern = functools.partial(_combine_kernel, alpha=alpha)
    return pl.pallas_call(
        kern,
        grid=(bsz, n_tiles),
        in_specs=[
            pl.BlockSpec((1, 1, 2 * tm), lambda b, i: (b * n_tiles + i, 0, 0), memory_space=pltpu.SMEM),
            pl.BlockSpec((tm, LANES), lambda b, i: (b * n_tiles + i, 0)),
            pl.BlockSpec((None, tm, d), lambda b, i: (b, i, 0)),
            _mod_spec(5, bsz, n_ctx_tiles, tile0),
            pl.BlockSpec(lng.shape, lambda b, i: (0, 0)),
            pl.BlockSpec(lnb.shape, lambda b, i: (0, 0)),
            pl.BlockSpec(memory_space=pl.ANY),
        ],
        out_specs=pl.BlockSpec((None, tm, d), lambda b, i: (b, i, 0)),
        out_shape=jax.ShapeDtypeStruct((bsz, n_tiles * tm, d), F32),
        scratch_shapes=[pltpu.VMEM((2, tm, d), F32), pltpu.SemaphoreType.DMA(())],
        compiler_params=pltpu.CompilerParams(dimension_semantics=("arbitrary", "arbitrary")),
        name="moe_combine_ln2",
    )(slots3, rec, x1, mods3, lng, lnb, ys)


def _rope_tables(n_ctx, n_lat):
    rows = n_lat // GRID_W
    row = jnp.repeat(jnp.arange(rows, dtype=F32), GRID_W)
    col = jnp.tile(jnp.arange(GRID_W, dtype=F32), rows)
    inv = 1.0 / (ROPE_BASE ** (jnp.arange(0, ROPE_AXIS_DIM, 2, dtype=F32) / ROPE_AXIS_DIM))
    half = ROPE_AXIS_DIM // 2
    zeros = jnp.zeros((n_lat, half), F32)
    cos, sa, sb = [], [], []
    for pos in (row, col):
        ang = pos[:, None] * inv
        c, s = jnp.cos(ang), jnp.sin(ang)
        cos += [c, c]
        sa += [-s, zeros]
        sb += [zeros, s]
    reps = LANES // HEAD_DIM

    def full(parts, ctx_fill):
        lat = jnp.tile(jnp.concatenate(parts, axis=1), (1, reps))
        return jnp.concatenate([jnp.full((n_ctx, LANES), ctx_fill, F32), lat], axis=0)

    return full(cos, 1.0), full(sa, 0.0), full(sb, 0.0)


def kernel(x, c, ctx, c_ctx, w_mod, b_mod, w_in, lam_q1, lam_k1, lam_q2, lam_k2, subln_g,
           ssm_a_re, ssm_a_im, ssm_log_dt, ssm_b_re, ssm_b_im, ssm_c_re, ssm_c_im, ssm_d,
           w_glu, b_glu, w_pa, w_ps, w_o, ln1_g, ln1_b,
           router_g_w, router_g_b, router_e_w, router_e_b, moe_w1, moe_w3, moe_w2, ln2_g, ln2_b):
    bsz, n_lat, d = x.shape
    n_ctx = ctx.shape[1]
    depth = w_mod.shape[0]
    assert d == D_MODEL and bsz % SUBLANES == 0
    assert n_ctx % ROW_TILE == 0 and n_lat % ROW_TILE == 0 and n_ctx % SCAN_TIME_TILE == 0
    alpha = (2.0 * depth) ** 0.25
    n_ctx_tiles = n_ctx // ROW_TILE

    mod_rows = -(-(bsz + 1) // SUBLANES) * SUBLANES
    cvec = jnp.concatenate([c, c_ctx[None, :], jnp.zeros((mod_rows - bsz - 1, d), F32)], axis=0)
    mods = _modulation(cvec, w_mod, b_mod)
    tabs = _rope_tables(n_ctx, n_lat)
    xall = jnp.concatenate([ctx, x], axis=1)
    row2 = lambda v: v.reshape(1, -1)

    for l in range(depth):
        last = l == depth - 1
        tile0 = n_ctx_tiles if last else 0
        lam_init = 0.8 - 0.6 * math.exp(-0.3 * l)
        mods3 = mods[l].reshape(mod_rows, 1, 6 * d)
        k, v, u4, q, siga, sigs = _inproj(xall, mods3, tabs, w_in[l].astype(BF16), n_ctx)
        attn = _attention(q, k, v, (row2(lam_q1[l]), row2(lam_k1[l]), row2(lam_q2[l]), row2(lam_k2[l])),
                          row2(subln_g[l]), n_ctx, tile0, lam_init)
        ops = _ssm_operators(ssm_a_re[l], ssm_a_im[l], ssm_log_dt[l], ssm_b_re[l], ssm_b_im[l],
                             ssm_c_re[l], ssm_c_im[l], ssm_d[l])
        y5 = _ssm(u4, ops, n_ctx)
        wr = jnp.concatenate([router_g_w[l], router_e_w[l],
                              jnp.zeros((d, LANES - N_GROUPS - N_EXPERTS), F32)], axis=1)
        br = jnp.concatenate([router_g_b[l], router_e_b[l],
                              jnp.zeros((LANES - N_GROUPS - N_EXPERTS,), F32)])
        wts = (w_pa[l].astype(BF16), w_glu[l].astype(BF16), row2(b_glu[l]), w_ps[l].astype(BF16),
               w_o[l].astype(BF16), row2(ln1_g[l]), row2(ln1_b[l]), wr, row2(br))
        x1, h2, logits = _merge(attn, y5, siga, sigs, xall, mods3, wts, n_ctx, tile0, alpha)

        n_tok = h2.shape[0]
        n_blocks = -(-(2 * n_tok + N_EXPERTS * (MOE_BLOCK - 1)) // MOE_BLOCK)
        rec, blk = _route(logits, n_blocks)
        n_tiles_tok = n_tok // ROW_TILE
        slots3 = (rec[:, 0:2].astype(jnp.int32).reshape(n_tiles_tok, ROW_TILE, 2)
                  .transpose(0, 2, 1).reshape(n_tiles_tok, 1, 2 * ROW_TILE))
        xs = _dispatch(h2, slots3, n_blocks * MOE_BLOCK)
        ys = _experts(blk[:n_blocks, 0], xs, moe_w1[l], moe_w3[l], moe_w2[l])
        xall = _combine(slots3, rec, x1, mods3, row2(ln2_g[l]), row2(ln2_b[l]), ys, n_ctx, tile0, alpha)
    return xall
```

```python
import functools
import math

import jax
import jax.numpy as jnp
from jax import lax
from jax.experimental import pallas as pl
from jax.experimental.pallas import tpu as pltpu

F32 = jnp.float32
BF16 = jnp.bfloat16

D_MODEL = 1024
N_HEADS = 8
HEAD_DIM = 64
V_DIM = 2 * HEAD_DIM
QK_W = N_HEADS * 2 * HEAD_DIM
ATTN_W = N_HEADS * V_DIM
SSM_W = D_MODEL // 2
SSM_GROUP = 16
SSM_GROUPS = SSM_W // SSM_GROUP
SSM_STATE = 64
N_GROUPS = 4
EXPERTS_PER_GROUP = 8
N_EXPERTS = N_GROUPS * EXPERTS_PER_GROUP
EXPERT_HIDDEN = D_MODEL // 2
KVU_W = QK_W + ATTN_W + SSM_W
IN_W = KVU_W + QK_W + 2 * D_MODEL
N_MOD = 6
GRID_W = 64
ROPE_BASE = 10000.0
ROPE_AXIS_DIM = HEAD_DIM // 2
LN_EPS = 1e-5

LANES = 128
SUBLANES = 8
VMEM_LIMIT = 56 * 1024 * 1024

ROW_TILE = 256
SCAN_CHUNK = 8
SCAN_TIME_TILE = 256
MOE_BLOCK = 256
ROUTE_TILE = 256
HEADS_PER_STEP = 2
ISSUE_UNROLL = 8
LANE_TILES = SSM_W // LANES
GROUPS_PER_TILE = LANES // SSM_GROUP
STATE_W = GROUPS_PER_TILE * SSM_STATE
NEG_BIG = -3.0e38


def _sigmoid(x):
    return 1.0 / (1.0 + jnp.exp(-x))


def _layer_norm(x, g, b):
    xc = x - jnp.mean(x, axis=-1, keepdims=True)
    var = jnp.mean(xc * xc, axis=-1, keepdims=True)
    return xc * lax.rsqrt(var + LN_EPS) * g + b


def _layer_vec(arr, l):
    return pl.BlockSpec((None, 1, arr.shape[-1]), lambda b, i: (l, 0, 0))


def _layer_mat(arr, l):
    return pl.BlockSpec((None,) + arr.shape[1:], lambda b, i: (l, 0, 0))


def _mod_spec(l, col, bsz, n_ctx_tiles, tile0):
    return pl.BlockSpec((None, None, 1, D_MODEL),
                        lambda b, i: (l, jnp.where(i + tile0 < n_ctx_tiles, bsz, b), 0, col))


def _mod_kernel(c_ref, w_ref, b_ref, o_ref):
    c = c_ref[...]
    s = c * _sigmoid(c)
    o_ref[...] = jnp.dot(s, w_ref[...], preferred_element_type=F32, precision=lax.Precision.HIGHEST) + b_ref[...]


def _modulation(cvec, w_mod, b_mod):
    depth, d, w6 = w_mod.shape
    rows = cvec.shape[0]
    tn = 1024
    return pl.pallas_call(
        _mod_kernel,
        grid=(depth, w6 // tn),
        in_specs=[
            pl.BlockSpec((rows, d), lambda l, j: (0, 0)),
            pl.BlockSpec((None, d, tn), lambda l, j: (l, 0, j)),
            pl.BlockSpec((None, 1, tn), lambda l, j: (l, 0, j)),
        ],
        out_specs=pl.BlockSpec((None, rows, tn), lambda l, j: (l, 0, j)),
        out_shape=jax.ShapeDtypeStruct((depth, rows, w6), F32),
        name="modulation",
    )(cvec, w_mod, b_mod.reshape(depth, 1, w6))


def _inproj_kernel(x_ref, sh_ref, sc_ref, cos_ref, sa_ref, sb_ref, w_ref,
                   k_ref, v_ref, u_ref, q_ref, ga_ref, gs_ref):
    h = (x_ref[...] * (1.0 + sc_ref[...]) + sh_ref[...]).astype(BF16)
    cos, sa, sb = cos_ref[...], sa_ref[...], sb_ref[...]

    def rope(t):
        return t * cos + pltpu.roll(t, LANES - 16, 1) * sa + pltpu.roll(t, 16, 1) * sb

    def proj(lo, hi):
        return jnp.dot(h, w_ref[:, lo:hi], preferred_element_type=F32)

    kk = proj(0, QK_W)
    for c in range(QK_W // LANES):
        k_ref[:, c * LANES:(c + 1) * LANES] = rope(kk[:, c * LANES:(c + 1) * LANES]).astype(BF16)
    v_ref[...] = proj(QK_W, QK_W + ATTN_W).astype(BF16)
    u_ref[...] = proj(QK_W + ATTN_W, KVU_W)
    qq = proj(KVU_W, KVU_W + QK_W)
    scale = HEAD_DIM ** -0.5 * math.log2(math.e)
    for c in range(QK_W // LANES):
        q_ref[:, c * LANES:(c + 1) * LANES] = (rope(qq[:, c * LANES:(c + 1) * LANES]) * scale).astype(BF16)
    ga_ref[...] = _sigmoid(proj(KVU_W + QK_W, KVU_W + QK_W + D_MODEL)).astype(BF16)
    gs_ref[...] = _sigmoid(proj(KVU_W + QK_W + D_MODEL, IN_W)).astype(BF16)


def _inproj(l, xall, mods4, tabs, w_in_bf, n_ctx):
    bsz, t_all, d = xall.shape
    tm = ROW_TILE
    n_ctx_tiles = n_ctx // tm
    n_tiles = t_all // tm
    tok_spec = lambda w: pl.BlockSpec((None, tm, w), lambda b, i: (b, i, 0))
    tab_spec = pl.BlockSpec((tm, LANES), lambda b, i: (i, 0))
    big = lambda w, dt: jax.ShapeDtypeStruct((bsz, t_all, w), dt)
    return pl.pallas_call(
        _inproj_kernel,
        grid=(bsz, n_tiles),
        in_specs=[
            tok_spec(d),
            _mod_spec(l, 0, bsz, n_ctx_tiles, 0),
            _mod_spec(l, 1, bsz, n_ctx_tiles, 0),
            tab_spec, tab_spec, tab_spec,
            _layer_mat(w_in_bf, l),
        ],
        out_specs=[tok_spec(QK_W), tok_spec(ATTN_W), tok_spec(SSM_W),
                   tok_spec(QK_W), tok_spec(D_MODEL), tok_spec(D_MODEL)],
        out_shape=[big(QK_W, BF16), big(ATTN_W, BF16), big(SSM_W, F32),
                   big(QK_W, BF16), big(D_MODEL, BF16), big(D_MODEL, BF16)],
        compiler_params=pltpu.CompilerParams(
            dimension_semantics=("parallel", "arbitrary"), vmem_limit_bytes=VMEM_LIMIT),
        name="inproj",
    )(xall, mods4, mods4, tabs[0], tabs[1], tabs[2], w_in_bf)


def _attn_kernel(q_ref, k_ref, v_ref, lq1_ref, lk1_ref, lq2_ref, lk2_ref, g_ref, o_ref,
                 *, n_ctx, n_ctx_tiles, tile0, lam_init):
    i = pl.program_id(2) + tile0
    lam = (jnp.exp(jnp.sum(lq1_ref[...] * lk1_ref[...], axis=1, keepdims=True))
           - jnp.exp(jnp.sum(lq2_ref[...] * lk2_ref[...], axis=1, keepdims=True)) + lam_init)
    g = g_ref[...]
    lane = lax.broadcasted_iota(jnp.int32, (q_ref.shape[0], V_DIM), 1)

    def head(hh, n_kv):
        cols = slice(hh * V_DIM, (hh + 1) * V_DIM)
        q = q_ref[:, cols]
        zero = jnp.zeros_like(q)
        k = k_ref[0:n_kv, cols]
        v = v_ref[0:n_kv, cols]

        def probs(qm):
            s = lax.dot_general(qm, k, (((1,), (1,)), ((), ())), preferred_element_type=F32)
            p = jnp.exp2(s - jnp.max(s, axis=-1, keepdims=True))
            return p, 1.0 / jnp.sum(p, axis=-1, keepdims=True)

        p1, r1 = probs(jnp.where(lane < HEAD_DIM, q, zero))
        p2, r2 = probs(jnp.where(lane < HEAD_DIM, zero, q))
        a = p1 * r1 - p2 * (lam * r2)
        o = jnp.dot(a.astype(BF16), v, preferred_element_type=F32)
        o = o * lax.rsqrt(jnp.mean(o * o, axis=-1, keepdims=True) + LN_EPS) * g * (1.0 - lam_init)
        o_ref[:, cols] = o.astype(BF16)

    def attend(n_kv):
        for hh in range(HEADS_PER_STEP):
            head(hh, n_kv)

    if n_ctx_tiles > tile0:
        @pl.when(i < n_ctx_tiles)
        def _():
            attend(n_ctx)

        @pl.when(i >= n_ctx_tiles)
        def _():
            attend(k_ref.shape[0])
    else:
        attend(k_ref.shape[0])


def _attention(l, q, k, v, lam_vecs, subln_g, n_ctx, tile0, lam_init):
    bsz, t_all, _ = q.shape
    tq = ROW_TILE
    hw = HEADS_PER_STEP * V_DIM
    n_tiles = t_all // tq - tile0
    vec = lambda arr: pl.BlockSpec((None, 1, arr.shape[-1]), lambda b, h, i: (l, 0, 0))
    kern = functools.partial(_attn_kernel, n_ctx=n_ctx, n_ctx_tiles=n_ctx // tq, tile0=tile0, lam_init=lam_init)
    return pl.pallas_call(
        kern,
        grid=(bsz, N_HEADS // HEADS_PER_STEP, n_tiles),
        in_specs=[
            pl.BlockSpec((None, tq, hw), lambda b, h, i: (b, i + tile0, h)),
            pl.BlockSpec((None, t_all, hw), lambda b, h, i: (b, 0, h)),
            pl.BlockSpec((None, t_all, hw), lambda b, h, i: (b, 0, h)),
            vec(lam_vecs[0]), vec(lam_vecs[1]), vec(lam_vecs[2]), vec(lam_vecs[3]), vec(subln_g),
        ],
        out_specs=pl.BlockSpec((None, tq, hw), lambda b, h, i: (b, i, h)),
        out_shape=jax.ShapeDtypeStruct((bsz, n_tiles * tq, ATTN_W), BF16),
        compiler_params=pltpu.CompilerParams(
            dimension_semantics=("parallel", "parallel", "arbitrary"), vmem_limit_bytes=VMEM_LIMIT),
        name="diff_attention",
    )(q, k, v, *lam_vecs, subln_g)


def _ssm_operators(a_re, a_im, log_dt, b_re, b_im, c_re, c_im, d_skip):
    lc = SCAN_CHUNK
    lam = lax.complex(a_re.astype(F32), a_im.astype(F32))
    dt = jnp.exp(log_dt.astype(F32))[..., None]
    ldt = lam * dt
    a_bar = jnp.exp(ldt)
    b_bar = ((a_bar - 1.0) / lam)[..., None] * lax.complex(b_re.astype(F32), b_im.astype(F32))
    cm = lax.complex(c_re.astype(F32), c_im.astype(F32))
    steps = jnp.arange(lc + 1, dtype=F32)
    apow = jnp.exp(ldt[None] * steps[:, None, None, None])
    eye = jnp.eye(GROUPS_PER_TILE, dtype=F32)
    s_idx = jnp.arange(lc)
    lag_f = s_idx[None, :] - s_idx[:, None]
    lk = lc * LANES
    intra, inject, readout = [], [], []
    for di in range(2):
        kern = jnp.einsum('gcp,jgp,gpi->jgci', cm[di], apow[:lc, di], b_bar[di]).real
        lag = lag_f if di == 0 else -lag_f
        toe = jnp.where((lag >= 0)[:, :, None, None, None], kern[jnp.clip(lag, 0, lc - 1)], 0.0)
        toe = toe.reshape(lc, lc, LANE_TILES, GROUPS_PER_TILE, SSM_GROUP, SSM_GROUP)
        op = jnp.einsum('stjgci,gh->jsgithc', toe, eye).reshape(LANE_TILES, lk, lk)
        if di == 0:
            dd = jnp.tile(d_skip.astype(F32).reshape(LANE_TILES, 1, LANES), (1, lc, 1)).reshape(LANE_TILES, lk)
            op = op + dd[:, :, None] * jnp.eye(lk, dtype=F32)[None]
        intra.append(op)
        pw_in = apow[lc - 1 - s_idx, di] if di == 0 else apow[s_idx, di]
        w = pw_in[..., None] * b_bar[di][None]
        w = jnp.stack([w.real, w.imag]).reshape(2, lc, LANE_TILES, GROUPS_PER_TILE, SSM_STATE, SSM_GROUP)
        inject.append(jnp.einsum('rsjgpi,gh->jsgirhp', w, eye).reshape(LANE_TILES, lk, 2 * STATE_W))
        pw_out = apow[s_idx + 1, di] if di == 0 else apow[lc - s_idx, di]
        vv = cm[di][None] * pw_out[:, :, None, :]
        vv = jnp.stack([vv.real, -vv.imag]).reshape(2, lc, LANE_TILES, GROUPS_PER_TILE, SSM_GROUP, SSM_STATE)
        readout.append(jnp.einsum('rtjgcp,gh->jrgpthc', vv, eye).reshape(LANE_TILES, 2 * STATE_W, lk))
    al = apow[lc].reshape(2, LANE_TILES, 1, STATE_W)
    return (jnp.stack(intra).astype(BF16), jnp.stack(inject).astype(BF16), jnp.stack(readout).astype(BF16),
            al.real, al.imag)


def _ssm_kernel(u_ref, a_ref, bc_ref, wc_ref, alr_ref, ali_ref, y_ref, s_scr, h_scr, st_scr):
    ph = pl.program_id(1)
    ti = pl.program_id(2)
    lc = SCAN_CHUNK
    bsz = u_ref.shape[0]
    n_chunks = u_ref.shape[1] // lc

    @pl.when(ti == 0)
    def _():
        st_scr[...] = jnp.zeros_like(st_scr)

    x = jnp.concatenate(
        [jnp.concatenate([u_ref[b, pl.ds(s, n_chunks, stride=lc), :] for s in range(lc)], axis=1)
         for b in range(bsz)], axis=0).astype(BF16)
    s_all = jnp.dot(x, bc_ref[...], preferred_element_type=F32)
    n_lt = 2 * STATE_W // LANES
    half = n_lt // 2
    for t in range(n_lt):
        s_scr[t] = s_all[:, t * LANES:(t + 1) * LANES]
    alr, ali = alr_ref[...], ali_ref[...]

    def step(c, carry):
        hr, hi = carry
        ce = jnp.where(ph == 0, c, n_chunks - 1 - c)
        rows = pl.ds(ce, bsz, stride=n_chunks)
        for t in range(half):
            h_scr[t, rows, :] = hr[:, t * LANES:(t + 1) * LANES]
            h_scr[half + t, rows, :] = hi[:, t * LANES:(t + 1) * LANES]
        sr = jnp.concatenate([s_scr[t, rows, :] for t in range(half)], axis=1)
        si = jnp.concatenate([s_scr[half + t, rows, :] for t in range(half)], axis=1)
        return alr * hr - ali * hi + sr, alr * hi + ali * hr + si

    hr, hi = lax.fori_loop(0, n_chunks, step, (st_scr[0], st_scr[1]))
    st_scr[0] = hr
    st_scr[1] = hi
    h_all = jnp.concatenate([h_scr[t] for t in range(n_lt)], axis=1).astype(BF16)
    y = (jnp.dot(x, a_ref[...], preferred_element_type=F32)
         + jnp.dot(h_all, wc_ref[...], preferred_element_type=F32))
    for b in range(bsz):
        for s in range(lc):
            y_ref[b, pl.ds(s, n_chunks, stride=lc), :] = y[b * n_chunks:(b + 1) * n_chunks,
                                                           s * LANES:(s + 1) * LANES]


def _ssm(l, u, ops, n_ctx):
    bsz, t_all, _ = u.shape
    lc = SCAN_CHUNK
    tt = SCAN_TIME_TILE
    n_t = t_all // tt
    n_ctx_t = n_ctx // tt
    rows = bsz * tt // lc
    intra, inject, readout, alr, ali = ops
    lk = lc * LANES

    def tile_of(ph, i):
        rev = jnp.where(i < n_ctx_t, n_ctx_t - 1 - i, n_t - 1 - (i - n_ctx_t))
        return jnp.where(ph == 0, i, rev)

    op_spec = lambda r, c: pl.BlockSpec((None, None, None, r, c), lambda j, ph, i: (l, ph, j, 0, 0))
    return pl.pallas_call(
        _ssm_kernel,
        grid=(LANE_TILES, 2, n_t),
        in_specs=[
            pl.BlockSpec((bsz, tt, LANES), lambda j, ph, i: (0, tile_of(ph, i), j)),
            op_spec(lk, lk), op_spec(lk, 2 * STATE_W), op_spec(2 * STATE_W, lk),
            op_spec(1, STATE_W), op_spec(1, STATE_W),
        ],
        out_specs=pl.BlockSpec((None, bsz, tt, LANES), lambda j, ph, i: (ph, 0, tile_of(ph, i), j)),
        out_shape=jax.ShapeDtypeStruct((2, bsz, t_all, SSM_W), F32),
        scratch_shapes=[
            pltpu.VMEM((2 * STATE_W // LANES, rows, LANES), F32),
            pltpu.VMEM((2 * STATE_W // LANES, rows, LANES), F32),
            pltpu.VMEM((2, bsz, STATE_W), F32),
        ],
        compiler_params=pltpu.CompilerParams(
            dimension_semantics=("parallel", "arbitrary", "arbitrary"), vmem_limit_bytes=VMEM_LIMIT),
        name="s5_scan",
    )(u, intra, inject, readout, alr, ali)


def _merge_kernel(attn_ref, y_ref, ga_ref, gs_ref, x_ref, g1_ref, sh2_ref, sc2_ref,
                  wpa_ref, wglu_ref, bglu_ref, wps_ref, wo_ref, lng_ref, lnb_ref, wr_ref, br_ref,
                  x1_ref, h2_ref, lg_ref, *, alpha):
    a = jnp.dot(attn_ref[...], wpa_ref[...], preferred_element_type=F32)
    ys = y_ref[0] + y_ref[1]
    gl = ys * (0.5 * (1.0 + jnp.tanh(math.sqrt(2.0 / math.pi) * (ys + 0.044715 * (ys * ys * ys)))))
    z = jnp.dot(gl.astype(BF16), wglu_ref[...], preferred_element_type=F32) + bglu_ref[...]
    sg = gl * _sigmoid(z)
    s = jnp.dot(sg.astype(BF16), wps_ref[...], preferred_element_type=F32)
    m = ga_ref[...].astype(F32) * a + gs_ref[...].astype(F32) * s
    y = jnp.dot(m.astype(BF16), wo_ref[...], preferred_element_type=F32)
    x1 = _layer_norm(alpha * x_ref[...] + g1_ref[...] * y, lng_ref[...], lnb_ref[...])
    x1_ref[...] = x1
    h2 = x1 * (1.0 + sc2_ref[...]) + sh2_ref[...]
    h2_ref[...] = h2
    lg_ref[...] = jnp.dot(h2, wr_ref[...], preferred_element_type=F32,
                          precision=lax.Precision.HIGHEST) + br_ref[...]


def _merge(l, attn, y, siga, sigs, xall, mods4, wts, n_ctx, tile0, alpha):
    bsz, t_all, d = xall.shape
    tm = ROW_TILE
    n_ctx_tiles = n_ctx // tm
    n_tiles = t_all // tm - tile0
    t_out = n_tiles * tm
    tok = lambda w: pl.BlockSpec((None, tm, w), lambda b, i: (b, i + tile0, 0))
    own = lambda w: pl.BlockSpec((None, tm, w), lambda b, i: (b, i, 0))
    modv = lambda col: _mod_spec(l, col, bsz, n_ctx_tiles, tile0)
    wpa, wglu, bglu, wps, wo, lng, lnb, wr, br = wts
    kern = functools.partial(_merge_kernel, alpha=alpha)
    return pl.pallas_call(
        kern,
        grid=(bsz, n_tiles),
        in_specs=[
            own(ATTN_W),
            pl.BlockSpec((2, None, tm, SSM_W), lambda b, i: (0, b, i + tile0, 0)),
            tok(D_MODEL), tok(D_MODEL), tok(d),
            modv(2), modv(3), modv(4),
            _layer_mat(wpa, l), _layer_mat(wglu, l), _layer_vec(bglu, l), _layer_mat(wps, l), _layer_mat(wo, l),
            _layer_vec(lng, l), _layer_vec(lnb, l), _layer_mat(wr, l), _layer_vec(br, l),
        ],
        out_specs=[
            own(d),
            pl.BlockSpec((tm, d), lambda b, i: (b * n_tiles + i, 0)),
            pl.BlockSpec((tm, LANES), lambda b, i: (b * n_tiles + i, 0)),
        ],
        out_shape=[
            jax.ShapeDtypeStruct((bsz, t_out, d), F32),
            jax.ShapeDtypeStruct((bsz * t_out, d), F32),
            jax.ShapeDtypeStruct((bsz * t_out, LANES), F32),
        ],
        compiler_params=pltpu.CompilerParams(
            dimension_semantics=("parallel", "arbitrary"), vmem_limit_bytes=VMEM_LIMIT),
        name="merge_ln1",
    )(attn, y, siga, sigs, xall, mods4, mods4, mods4, wpa, wglu, bglu, wps, wo, lng, lnb, wr, br)


def _route_kernel(lg_ref, rec_ref, blk_ref, cnt_scr, start_scr, carry_scr):
    ph = pl.program_id(0)
    i = pl.program_id(1)
    lg = lg_ref[...]
    shape = lg.shape
    lane = lax.broadcasted_iota(jnp.int32, shape, 1).astype(F32)
    far = jnp.full(shape, 1.0e9, F32)

    def first_max(vals, mask):
        vm = jnp.where(mask, vals, NEG_BIG)
        mx = jnp.max(vm, axis=1, keepdims=True)
        idx = jnp.min(jnp.where(mask & (vm == mx), lane, far), axis=1, keepdims=True)
        return mx, idx

    gmask = lane < N_GROUPS
    gmax, gidx = first_max(lg, gmask)
    gtop = 1.0 / jnp.sum(jnp.where(gmask, jnp.exp(lg - gmax), 0.0), axis=1, keepdims=True)
    lo = N_GROUPS + EXPERTS_PER_GROUP * gidx
    emask = (lane >= lo) & (lane < lo + EXPERTS_PER_GROUP)
    v1, i1 = first_max(lg, emask)
    v2, i2 = first_max(lg, emask & (lane != i1))
    e2 = jnp.exp(v2 - v1)
    den = 1.0 + e2
    w1 = (1.0 / den) * gtop
    w2 = (e2 / den) * gtop
    oh1 = (lane == i1 - N_GROUPS).astype(F32)
    oh2 = (lane == i2 - N_GROUPS).astype(F32)
    c1 = jnp.sum(oh1, axis=0, keepdims=True)
    c2 = jnp.sum(oh2, axis=0, keepdims=True)

    @pl.when((ph == 0) & (i == 0))
    def _():
        cnt_scr[...] = jnp.zeros_like(cnt_scr)

    @pl.when(ph == 0)
    def _():
        cnt_scr[...] += c1 + c2

    @pl.when((ph == 1) & (i == 0))
    def _():
        cnt = cnt_scr[...]
        padded = jnp.floor((cnt + (MOE_BLOCK - 1)) * (1.0 / MOE_BLOCK)) * MOE_BLOCK
        r = lax.broadcasted_iota(jnp.int32, (LANES, LANES), 0)
        c = lax.broadcasted_iota(jnp.int32, (LANES, LANES), 1)
        upper = (r < c).astype(F32)
        start = jnp.dot(jnp.broadcast_to(padded, (SUBLANES, LANES)), upper, preferred_element_type=F32,
                        precision=lax.Precision.HIGHEST)[0:1]
        start_scr[...] = start
        carry_scr[...] = jnp.zeros_like(carry_scr)
        nb = blk_ref.shape[0]
        blk_start = (lax.broadcasted_iota(jnp.int32, (nb, LANES), 0) * MOE_BLOCK).astype(F32)
        elane = lax.broadcasted_iota(jnp.int32, (nb, LANES), 1) < N_EXPERTS
        done = jnp.sum(jnp.where(elane & ((start + padded) <= blk_start), 1.0, 0.0), axis=1, keepdims=True)
        blk_ref[...] = jnp.broadcast_to(jnp.minimum(done, N_EXPERTS - 1.0), (nb, LANES)).astype(jnp.int32)

    @pl.when(ph == 1)
    def _():
        tr = shape[0]
        r = lax.broadcasted_iota(jnp.int32, (tr, tr), 0)
        c = lax.broadcasted_iota(jnp.int32, (tr, tr), 1)
        tri = (c < r).astype(BF16)
        base = start_scr[...] + carry_scr[...]
        r1 = jnp.dot(tri, oh1.astype(BF16), preferred_element_type=F32)
        r2 = jnp.dot(tri, oh2.astype(BF16), preferred_element_type=F32) + c1
        d1 = jnp.sum(oh1 * (base + r1), axis=1, keepdims=True)
        d2 = jnp.sum(oh2 * (base + r2), axis=1, keepdims=True)
        carry_scr[...] += c1 + c2
        rec_ref[...] = jnp.where(lane == 0.0, d1, jnp.where(lane == 1.0, d2, jnp.where(lane == 2.0, w1, w2)))


def _route(logits, n_blocks):
    n_tok = logits.shape[0]
    tr = ROUTE_TILE
    nb_pad = -(-n_blocks // SUBLANES) * SUBLANES
    return pl.pallas_call(
        _route_kernel,
        grid=(2, n_tok // tr),
        in_specs=[pl.BlockSpec((tr, LANES), lambda ph, i: (i, 0))],
        out_specs=[
            pl.BlockSpec((tr, LANES), lambda ph, i: (i * ph, 0)),
            pl.BlockSpec((nb_pad, LANES), lambda ph, i: (0, 0)),
        ],
        out_shape=[
            jax.ShapeDtypeStruct((n_tok, LANES), F32),
            jax.ShapeDtypeStruct((nb_pad, LANES), jnp.int32),
        ],
        scratch_shapes=[pltpu.VMEM((1, LANES), F32)] * 3,
        compiler_params=pltpu.CompilerParams(dimension_semantics=("arbitrary", "arbitrary")),
        name="moe_route",
    )(logits)


def _row_copy(src, dst, sem):
    return pltpu.make_async_copy(src, dst, sem)


def _dispatch_kernel(slot_ref, h_ref, xs_in_ref, xs_ref, sem):
    del xs_in_ref
    tg = h_ref.shape[0]

    def issue(r, carry):
        for k in range(2):
            d = slot_ref[0, 0, k * tg + r]
            _row_copy(h_ref.at[pl.ds(r, 1)], xs_ref.at[pl.ds(d, 1)], sem).start(priority=k)
        return carry

    lax.fori_loop(0, tg, issue, 0, unroll=ISSUE_UNROLL)

    def drain(r, carry):
        for k in range(2):
            _row_copy(h_ref.at[pl.ds(0, 1)], xs_ref.at[pl.ds(0, 1)], sem).wait()
        return carry

    lax.fori_loop(0, tg, drain, 0, unroll=ISSUE_UNROLL)


def _dispatch(h2, slots3, cap):
    n_tok, d = h2.shape
    tg = ROW_TILE
    xs0 = jnp.zeros((cap, d), F32)
    return pl.pallas_call(
        _dispatch_kernel,
        grid=(n_tok // tg,),
        in_specs=[
            pl.BlockSpec((1, 1, 2 * tg), lambda i: (i, 0, 0), memory_space=pltpu.SMEM),
            pl.BlockSpec((tg, d), lambda i: (i, 0)),
            pl.BlockSpec(memory_space=pl.ANY),
        ],
        out_specs=pl.BlockSpec(memory_space=pl.ANY),
        out_shape=jax.ShapeDtypeStruct((cap, d), F32),
        scratch_shapes=[pltpu.SemaphoreType.DMA(())],
        input_output_aliases={2: 0},
        compiler_params=pltpu.CompilerParams(dimension_semantics=("arbitrary",), has_side_effects=True),
        name="moe_dispatch",
    )(slots3, h2, xs0)


def _expert_kernel(be_ref, xs_ref, w1_ref, w3_ref, w2_ref, ys_ref, w1b, w3b, w2b):
    i = pl.program_id(0)
    fresh = (i == 0) | (be_ref[i] != be_ref[jnp.maximum(i - 1, 0)])

    @pl.when(fresh)
    def _():
        w1b[...] = w1_ref[...].astype(BF16)
        w3b[...] = w3_ref[...].astype(BF16)
        w2b[...] = w2_ref[...].astype(BF16)

    x = xs_ref[...].astype(BF16)
    a = jnp.dot(x, w1b[...], preferred_element_type=F32)
    b = jnp.dot(x, w3b[...], preferred_element_type=F32)
    hid = (a * _sigmoid(a)) * b
    ys_ref[...] = jnp.dot(hid.astype(BF16), w2b[...], preferred_element_type=F32)


def _experts(l, blk_expert, xs, w1, w3, w2):
    cap, d = xs.shape
    f = w1.shape[-1]
    n_blocks = cap // MOE_BLOCK
    wspec = lambda r, c: pl.BlockSpec((None, None, r, c), lambda i, be: (l, be[i], 0, 0))
    grid_spec = pltpu.PrefetchScalarGridSpec(
        num_scalar_prefetch=1,
        grid=(n_blocks,),
        in_specs=[pl.BlockSpec((MOE_BLOCK, d), lambda i, be: (i, 0)), wspec(d, f), wspec(d, f), wspec(f, d)],
        out_specs=pl.BlockSpec((MOE_BLOCK, d), lambda i, be: (i, 0)),
        scratch_shapes=[pltpu.VMEM((d, f), BF16), pltpu.VMEM((d, f), BF16), pltpu.VMEM((f, d), BF16)],
    )
    return pl.pallas_call(
        _expert_kernel,
        grid_spec=grid_spec,
        out_shape=jax.ShapeDtypeStruct((cap, d), F32),
        compiler_params=pltpu.CompilerParams(
            dimension_semantics=("arbitrary",), vmem_limit_bytes=VMEM_LIMIT),
        name="moe_experts",
    )(blk_expert, xs, w1, w3, w2)


def _combine_kernel(slot_ref, rec_ref, x_ref, g2_ref, lng_ref, lnb_ref, ys_ref, o_ref, buf, sem, *, alpha):
    tm = x_ref.shape[0]

    def issue(r, carry):
        for k in range(2):
            d = slot_ref[0, 0, k * tm + r]
            _row_copy(ys_ref.at[pl.ds(d, 1)], buf.at[k, pl.ds(r, 1)], sem).start(priority=k)
        return carry

    lax.fori_loop(0, tm, issue, 0, unroll=ISSUE_UNROLL)

    def drain(r, carry):
        for k in range(2):
            _row_copy(ys_ref.at[pl.ds(0, 1)], buf.at[0, pl.ds(0, 1)], sem).wait()
        return carry

    lax.fori_loop(0, tm, drain, 0, unroll=ISSUE_UNROLL)
    rec = rec_ref[...]
    f = rec[:, 2:3] * buf[0] + rec[:, 3:4] * buf[1]
    o_ref[...] = _layer_norm(alpha * x_ref[...] + g2_ref[...] * f, lng_ref[...], lnb_ref[...])


def _combine(l, slots3, rec, x1, mods4, lng, lnb, ys, n_ctx, tile0, alpha):
    bsz, t_out, d = x1.shape
    tm = ROW_TILE
    n_ctx_tiles = n_ctx // tm
    n_tiles = t_out // tm
    kern = functools.partial(_combine_kernel, alpha=alpha)
    return pl.pallas_call(
        kern,
        grid=(bsz, n_tiles),
        in_specs=[
            pl.BlockSpec((1, 1, 2 * tm), lambda b, i: (b * n_tiles + i, 0, 0), memory_space=pltpu.SMEM),
            pl.BlockSpec((tm, LANES), lambda b, i: (b * n_tiles + i, 0)),
            pl.BlockSpec((None, tm, d), lambda b, i: (b, i, 0)),
            _mod_spec(l, 5, bsz, n_ctx_tiles, tile0),
            _layer_vec(lng, l), _layer_vec(lnb, l),
            pl.BlockSpec(memory_space=pl.ANY),
        ],
        out_specs=pl.BlockSpec((None, tm, d), lambda b, i: (b, i, 0)),
        out_shape=jax.ShapeDtypeStruct((bsz, t_out, d), F32),
        scratch_shapes=[pltpu.VMEM((2, tm, d), F32), pltpu.SemaphoreType.DMA(())],
        compiler_params=pltpu.CompilerParams(dimension_semantics=("arbitrary", "arbitrary")),
        name="moe_combine_ln2",
    )(slots3, rec, x1, mods4, lng, lnb, ys)


def _rope_tables(n_ctx, n_lat):
    rows = n_lat // GRID_W
    row = jnp.repeat(jnp.arange(rows, dtype=F32), GRID_W)
    col = jnp.tile(jnp.arange(GRID_W, dtype=F32), rows)
    inv = 1.0 / (ROPE_BASE ** (jnp.arange(0, ROPE_AXIS_DIM, 2, dtype=F32) / ROPE_AXIS_DIM))
    half = ROPE_AXIS_DIM // 2
    zeros = jnp.zeros((n_lat, half), F32)
    cos, sa, sb = [], [], []
    for pos in (row, col):
        ang = pos[:, None] * inv
        c, s = jnp.cos(ang), jnp.sin(ang)
        cos += [c, c]
        sa += [-s, zeros]
        sb += [zeros, s]
    reps = LANES // HEAD_DIM

    def full(parts, ctx_fill):
        lat = jnp.tile(jnp.concatenate(parts, axis=1), (1, reps))
        return jnp.concatenate([jnp.full((n_ctx, LANES), ctx_fill, F32), lat], axis=0)

    return full(cos, 1.0), full(sa, 0.0), full(sb, 0.0)


def kernel(x, c, ctx, c_ctx, w_mod, b_mod, w_in, lam_q1, lam_k1, lam_q2, lam_k2, subln_g,
           ssm_a_re, ssm_a_im, ssm_log_dt, ssm_b_re, ssm_b_im, ssm_c_re, ssm_c_im, ssm_d,
           w_glu, b_glu, w_pa, w_ps, w_o, ln1_g, ln1_b,
           router_g_w, router_g_b, router_e_w, router_e_b, moe_w1, moe_w3, moe_w2, ln2_g, ln2_b):
    bsz, n_lat, d = x.shape
    n_ctx = ctx.shape[1]
    depth = w_mod.shape[0]
    assert d == D_MODEL and bsz % SUBLANES == 0
    assert n_ctx % ROW_TILE == 0 and n_lat % ROW_TILE == 0 and n_ctx % SCAN_TIME_TILE == 0
    alpha = (2.0 * depth) ** 0.25
    n_ctx_tiles = n_ctx // ROW_TILE

    mod_rows = -(-(bsz + 1) // SUBLANES) * SUBLANES
    cvec = jnp.concatenate([c, c_ctx[None, :], jnp.zeros((mod_rows - bsz - 1, d), F32)], axis=0)
    mods4 = _modulation(cvec, w_mod, b_mod).reshape(depth, mod_rows, 1, N_MOD * d)
    tabs = _rope_tables(n_ctx, n_lat)
    xall = jnp.concatenate([ctx, x], axis=1)
    vecs = lambda a: a.reshape(depth, 1, a.shape[-1])
    ops = jax.vmap(_ssm_operators)(ssm_a_re, ssm_a_im, ssm_log_dt, ssm_b_re, ssm_b_im, ssm_c_re, ssm_c_im, ssm_d)
    n_pad = LANES - N_GROUPS - N_EXPERTS
    wr = jnp.concatenate([router_g_w, router_e_w, jnp.zeros((depth, d, n_pad), F32)], axis=2)
    br = jnp.concatenate([router_g_b, router_e_b, jnp.zeros((depth, n_pad), F32)], axis=1)
    w_in_bf = w_in.astype(BF16)
    wts = (w_pa.astype(BF16), w_glu.astype(BF16), vecs(b_glu), w_ps.astype(BF16), w_o.astype(BF16),
           vecs(ln1_g), vecs(ln1_b), wr, vecs(br))
    lam_vecs = (vecs(lam_q1), vecs(lam_k1), vecs(lam_q2), vecs(lam_k2))
    ln2 = (vecs(ln2_g), vecs(ln2_b))
    subln = vecs(subln_g)

    for l in range(depth):
        last = l == depth - 1
        tile0 = n_ctx_tiles if last else 0
        lam_init = 0.8 - 0.6 * math.exp(-0.3 * l)
        k, v, u, q, siga, sigs = _inproj(l, xall, mods4, tabs, w_in_bf, n_ctx)
        attn = _attention(l, q, k, v, lam_vecs, subln, n_ctx, tile0, lam_init)
        y = _ssm(l, u, ops, n_ctx)
        x1, h2, logits = _merge(l, attn, y, siga, sigs, xall, mods4, wts, n_ctx, tile0, alpha)

        n_tok = h2.shape[0]
        n_blocks = -(-(2 * n_tok + N_EXPERTS * (MOE_BLOCK - 1)) // MOE_BLOCK)
        rec, blk = _route(logits, n_blocks)
        n_tiles_tok = n_tok // ROW_TILE
        slots3 = (rec[:, 0:2].astype(jnp.int32).reshape(n_tiles_tok, ROW_TILE, 2)
                  .transpose(0, 2, 1).reshape(n_tiles_tok, 1, 2 * ROW_TILE))
        xs = _dispatch(h2, slots3, n_blocks * MOE_BLOCK)
        ys = _experts(l, blk[:n_blocks, 0], xs, moe_w1, moe_w3, moe_w2)
        xall = _combine(l, slots3, rec, x1, mods4, ln2[0], ln2[1], ys, n_ctx, tile0, alpha)
    return xall
```

```python
import functools
import math

import jax
import jax.numpy as jnp
from jax import lax
from jax.experimental import pallas as pl
from jax.experimental.pallas import tpu as pltpu

F32 = jnp.float32
BF16 = jnp.bfloat16

D_MODEL = 1024
N_HEADS = 8
HEAD_DIM = 64
V_DIM = 2 * HEAD_DIM
QK_W = N_HEADS * 2 * HEAD_DIM
ATTN_W = N_HEADS * V_DIM
SSM_W = D_MODEL // 2
SSM_GROUP = 16
SSM_GROUPS = SSM_W // SSM_GROUP
SSM_STATE = 64
N_GROUPS = 4
EXPERTS_PER_GROUP = 8
N_EXPERTS = N_GROUPS * EXPERTS_PER_GROUP
EXPERT_HIDDEN = D_MODEL // 2
KVU_W = QK_W + ATTN_W + SSM_W
IN_W = KVU_W + QK_W + 2 * D_MODEL
N_MOD = 6
GRID_W = 64
ROPE_BASE = 10000.0
ROPE_AXIS_DIM = HEAD_DIM // 2
LN_EPS = 1e-5

LANES = 128
SUBLANES = 8
VMEM_LIMIT = 56 * 1024 * 1024

ROW_TILE = 256
SCAN_CHUNK = 8
SCAN_TIME_TILE = 256
MOE_BLOCK = 256
ROUTE_TILE = 256
HEADS_PER_STEP = 2
ISSUE_UNROLL = 8
LANE_TILES = SSM_W // LANES
GROUPS_PER_TILE = LANES // SSM_GROUP
STATE_W = GROUPS_PER_TILE * SSM_STATE
NEG_BIG = -3.0e38


def _sigmoid(x):
    return 1.0 / (1.0 + jnp.exp(-x))


def _layer_norm(x, g, b):
    xc = x - jnp.mean(x, axis=-1, keepdims=True)
    var = jnp.mean(xc * xc, axis=-1, keepdims=True)
    return xc * lax.rsqrt(var + LN_EPS) * g + b


def _layer_vec(arr, l):
    return pl.BlockSpec((None, 1, arr.shape[-1]), lambda b, i: (l, 0, 0))


def _layer_mat(arr, l):
    return pl.BlockSpec((None,) + arr.shape[1:], lambda b, i: (l, 0, 0))


def _mod_spec(l, col, bsz, n_ctx_tiles, tile0):
    return pl.BlockSpec((None, None, 1, D_MODEL),
                        lambda b, i: (l, jnp.where(i + tile0 < n_ctx_tiles, bsz, b), 0, col))


def _mod_kernel(c_ref, w_ref, b_ref, o_ref):
    c = c_ref[...]
    s = c * _sigmoid(c)
    o_ref[...] = jnp.dot(s, w_ref[...], preferred_element_type=F32, precision=lax.Precision.HIGHEST) + b_ref[...]


def _modulation(cvec, w_mod, b_mod):
    depth, d, w6 = w_mod.shape
    rows = cvec.shape[0]
    tn = 1024
    return pl.pallas_call(
        _mod_kernel,
        grid=(depth, w6 // tn),
        in_specs=[
            pl.BlockSpec((rows, d), lambda l, j: (0, 0)),
            pl.BlockSpec((None, d, tn), lambda l, j: (l, 0, j)),
            pl.BlockSpec((None, 1, tn), lambda l, j: (l, 0, j)),
        ],
        out_specs=pl.BlockSpec((None, rows, tn), lambda l, j: (l, 0, j)),
        out_shape=jax.ShapeDtypeStruct((depth, rows, w6), F32),
        name="modulation",
    )(cvec, w_mod, b_mod.reshape(depth, 1, w6))


def _inproj_kernel(x_ref, sh_ref, sc_ref, cos_ref, sa_ref, sb_ref, w_ref,
                   k_ref, v_ref, u_ref, q_ref, ga_ref, gs_ref):
    h = (x_ref[...] * (1.0 + sc_ref[...]) + sh_ref[...]).astype(BF16)
    cos, sa, sb = cos_ref[...], sa_ref[...], sb_ref[...]

    def rope(t):
        return t * cos + pltpu.roll(t, LANES - 16, 1) * sa + pltpu.roll(t, 16, 1) * sb

    def proj(lo, hi):
        return jnp.dot(h, w_ref[:, lo:hi], preferred_element_type=F32)

    kk = proj(0, QK_W)
    for c in range(QK_W // LANES):
        k_ref[:, c * LANES:(c + 1) * LANES] = rope(kk[:, c * LANES:(c + 1) * LANES]).astype(BF16)
    v_ref[...] = proj(QK_W, QK_W + ATTN_W).astype(BF16)
    u_ref[...] = proj(QK_W + ATTN_W, KVU_W)
    qq = proj(KVU_W, KVU_W + QK_W)
    scale = HEAD_DIM ** -0.5 * math.log2(math.e)
    for c in range(QK_W // LANES):
        q_ref[:, c * LANES:(c + 1) * LANES] = (rope(qq[:, c * LANES:(c + 1) * LANES]) * scale).astype(BF16)
    ga_ref[...] = _sigmoid(proj(KVU_W + QK_W, KVU_W + QK_W + D_MODEL)).astype(BF16)
    gs_ref[...] = _sigmoid(proj(KVU_W + QK_W + D_MODEL, IN_W)).astype(BF16)


def _inproj(l, xall, mods4, tabs, w_in_bf, n_ctx):
    bsz, t_all, d = xall.shape
    tm = ROW_TILE
    n_ctx_tiles = n_ctx // tm
    n_tiles = t_all // tm
    tok_spec = lambda w: pl.BlockSpec((None, tm, w), lambda b, i: (b, i, 0))
    tab_spec = pl.BlockSpec((tm, LANES), lambda b, i: (i, 0))
    big = lambda w, dt: jax.ShapeDtypeStruct((bsz, t_all, w), dt)
    return pl.pallas_call(
        _inproj_kernel,
        grid=(bsz, n_tiles),
        in_specs=[
            tok_spec(d),
            _mod_spec(l, 0, bsz, n_ctx_tiles, 0),
            _mod_spec(l, 1, bsz, n_ctx_tiles, 0),
            tab_spec, tab_spec, tab_spec,
            _layer_mat(w_in_bf, l),
        ],
        out_specs=[tok_spec(QK_W), tok_spec(ATTN_W), tok_spec(SSM_W),
                   tok_spec(QK_W), tok_spec(D_MODEL), tok_spec(D_MODEL)],
        out_shape=[big(QK_W, BF16), big(ATTN_W, BF16), big(SSM_W, F32),
                   big(QK_W, BF16), big(D_MODEL, BF16), big(D_MODEL, BF16)],
        compiler_params=pltpu.CompilerParams(
            dimension_semantics=("parallel", "arbitrary"), vmem_limit_bytes=VMEM_LIMIT),
        name="inproj",
    )(xall, mods4, mods4, tabs[0], tabs[1], tabs[2], w_in_bf)


def _attn_kernel(q_ref, k_ref, v_ref, lq1_ref, lk1_ref, lq2_ref, lk2_ref, g_ref, o_ref,
                 *, n_ctx, n_ctx_tiles, tile0, lam_init):
    i = pl.program_id(2) + tile0
    lam = (jnp.exp(jnp.sum(lq1_ref[...] * lk1_ref[...], axis=1, keepdims=True))
           - jnp.exp(jnp.sum(lq2_ref[...] * lk2_ref[...], axis=1, keepdims=True)) + lam_init)
    g = g_ref[...]
    lane = lax.broadcasted_iota(jnp.int32, (q_ref.shape[0], V_DIM), 1)

    def head(hh, n_kv):
        cols = slice(hh * V_DIM, (hh + 1) * V_DIM)
        q = q_ref[:, cols]
        zero = jnp.zeros_like(q)
        k = k_ref[0:n_kv, cols]
        v = v_ref[0:n_kv, cols]

        def probs(qm):
            s = lax.dot_general(qm, k, (((1,), (1,)), ((), ())), preferred_element_type=F32)
            p = jnp.exp2(s - jnp.max(s, axis=-1, keepdims=True))
            return p, 1.0 / jnp.sum(p, axis=-1, keepdims=True)

        p1, r1 = probs(jnp.where(lane < HEAD_DIM, q, zero))
        p2, r2 = probs(jnp.where(lane < HEAD_DIM, zero, q))
        a = p1 * r1 - p2 * (lam * r2)
        o = jnp.dot(a.astype(BF16), v, preferred_element_type=F32)
        o = o * lax.rsqrt(jnp.mean(o * o, axis=-1, keepdims=True) + LN_EPS) * g * (1.0 - lam_init)
        o_ref[:, cols] = o.astype(BF16)

    def attend(n_kv):
        for hh in range(HEADS_PER_STEP):
            head(hh, n_kv)

    if n_ctx_tiles > tile0:
        @pl.when(i < n_ctx_tiles)
        def _():
            attend(n_ctx)

        @pl.when(i >= n_ctx_tiles)
        def _():
            attend(k_ref.shape[0])
    else:
        attend(k_ref.shape[0])


def _attention(l, q, k, v, lam_vecs, subln_g, n_ctx, tile0, lam_init):
    bsz, t_all, _ = q.shape
    tq = ROW_TILE
    hw = HEADS_PER_STEP * V_DIM
    n_tiles = t_all // tq - tile0
    vec = lambda arr: pl.BlockSpec((None, 1, arr.shape[-1]), lambda b, h, i: (l, 0, 0))
    kern = functools.partial(_attn_kernel, n_ctx=n_ctx, n_ctx_tiles=n_ctx // tq, tile0=tile0, lam_init=lam_init)
    return pl.pallas_call(
        kern,
        grid=(bsz, N_HEADS // HEADS_PER_STEP, n_tiles),
        in_specs=[
            pl.BlockSpec((None, tq, hw), lambda b, h, i: (b, i + tile0, h)),
            pl.BlockSpec((None, t_all, hw), lambda b, h, i: (b, 0, h)),
            pl.BlockSpec((None, t_all, hw), lambda b, h, i: (b, 0, h)),
            vec(lam_vecs[0]), vec(lam_vecs[1]), vec(lam_vecs[2]), vec(lam_vecs[3]), vec(subln_g),
        ],
        out_specs=pl.BlockSpec((None, tq, hw), lambda b, h, i: (b, i, h)),
        out_shape=jax.ShapeDtypeStruct((bsz, n_tiles * tq, ATTN_W), BF16),
        compiler_params=pltpu.CompilerParams(
            dimension_semantics=("parallel", "parallel", "arbitrary"), vmem_limit_bytes=VMEM_LIMIT),
        name="diff_attention",
    )(q, k, v, *lam_vecs, subln_g)


def _ssm_tables(a_re, a_im, log_dt, b_re, b_im, c_re, c_im, d_skip):
    lc = SCAN_CHUNK
    lam = lax.complex(a_re.astype(F32), a_im.astype(F32))
    dt = jnp.exp(log_dt.astype(F32))[..., None]
    ldt = lam * dt
    a_bar = jnp.exp(ldt)
    b_bar = ((a_bar - 1.0) / lam)[..., None] * lax.complex(b_re.astype(F32), b_im.astype(F32))
    cm = lax.complex(c_re.astype(F32), c_im.astype(F32))
    steps = jnp.arange(lc + 1, dtype=F32)
    apow = jnp.exp(ldt[None] * steps[:, None, None, None])
    s_idx = jnp.arange(lc)
    lag_f = s_idx[None, :] - s_idx[:, None]
    lk = lc * LANES
    eye_c = jnp.eye(SSM_GROUP, dtype=F32)
    eye_t = jnp.eye(lc, dtype=F32)
    d_g = d_skip.astype(F32).reshape(SSM_GROUPS, SSM_GROUP)

    def rows_of(t):
        t = t.reshape(lc, LANE_TILES, GROUPS_PER_TILE, SSM_GROUP, LANES)
        return t.transpose(1, 0, 2, 3, 4).reshape(LANE_TILES, lk, LANES)

    inject, readout, intra = [], [], []
    for di in range(2):
        pw_in = apow[lc - 1 - s_idx, di] if di == 0 else apow[s_idx, di]
        w = jnp.einsum('sgp,gpi->sgip', pw_in, b_bar[di])
        inject.append(rows_of(jnp.concatenate([w.real, w.imag], axis=-1)))
        pw_out = apow[s_idx + 1, di] if di == 0 else apow[lc - s_idx, di]
        vv = cm[di][None] * pw_out[:, :, None, :]
        readout.append(rows_of(jnp.concatenate([vv.real, -vv.imag], axis=-1)))
        kern = jnp.einsum('gcp,jgp,gpi->jgci', cm[di], apow[:lc, di], b_bar[di]).real
        lag = lag_f if di == 0 else -lag_f
        toe = jnp.where((lag >= 0)[:, :, None, None, None], kern[jnp.clip(lag, 0, lc - 1)], 0.0)
        toe = toe.transpose(0, 2, 4, 1, 3)
        if di == 0:
            toe = toe + jnp.einsum('st,gi,ic->sgitc', eye_t, d_g, eye_c)
        intra.append(rows_of(toe.reshape(lc, SSM_GROUPS, SSM_GROUP, LANES)))
    al = apow[lc].reshape(2, LANE_TILES, 1, STATE_W)
    return jnp.stack(inject), jnp.stack(readout), jnp.stack(intra), al.real, al.imag


def _ssm_kernel(u_ref, wi_ref, wo_ref, wk_ref, alr_ref, ali_ref, y_ref,
                bc_scr, wct_scr, a_scr, s_scr, h_scr, st_scr):
    ph = pl.program_id(1)
    ti = pl.program_id(2)
    lc = SCAN_CHUNK
    bsz = u_ref.shape[0]
    n_chunks = u_ref.shape[1] // lc
    n_rows = bsz * n_chunks
    lk = lc * LANES

    @pl.when(ti == 0)
    def _():
        st_scr[...] = jnp.zeros_like(st_scr)
        q = lax.broadcasted_iota(jnp.int32, (LANES, lk), 0)
        c = lax.broadcasted_iota(jnp.int32, (LANES, lk), 1)
        sel_state = (q == ((c >> 9) << 6) + (c & (SSM_STATE - 1))).astype(BF16)
        sel_tok = (q == ((c >> 7) << 4) + (c & (SSM_GROUP - 1))).astype(BF16)
        row_g = (lax.broadcasted_iota(jnp.int32, (lk, lk), 0) >> 4) & (GROUPS_PER_TILE - 1)
        col = lax.broadcasted_iota(jnp.int32, (lk, lk), 1)
        same_state = row_g == ((col >> 6) & (GROUPS_PER_TILE - 1))
        same_tok = row_g == ((col >> 4) & (GROUPS_PER_TILE - 1))

        def spread(tab_ref, sel, same):
            full = jnp.dot(tab_ref[...].astype(BF16), sel, preferred_element_type=F32)
            return jnp.where(same, full, 0.0).astype(BF16)

        bc_scr[...] = spread(wi_ref, sel_state, same_state)
        wct_scr[...] = spread(wo_ref, sel_state, same_state)
        a_scr[...] = spread(wk_ref, sel_tok, same_tok)

    x = jnp.concatenate(
        [jnp.concatenate([u_ref[b, pl.ds(s, n_chunks, stride=lc), :] for s in range(lc)], axis=1)
         for b in range(bsz)], axis=0).astype(BF16)
    r = lax.broadcasted_iota(jnp.int32, (n_rows, n_rows), 0)
    cc = lax.broadcasted_iota(jnp.int32, (n_rows, n_rows), 1)
    sh_b, sh_c = bsz.bit_length() - 1, n_chunks.bit_length() - 1
    to_cb = (cc == ((r & (bsz - 1)) << sh_c) + (r >> sh_b)).astype(BF16)
    to_bc = (cc == ((r & (n_chunks - 1)) << sh_b) + (r >> sh_c)).astype(BF16)
    x_cb = jnp.dot(to_cb, x, preferred_element_type=F32).astype(BF16)
    s_scr[...] = jnp.dot(x_cb, bc_scr[...], preferred_element_type=F32)
    alr, ali = alr_ref[...], ali_ref[...]

    def step(c, carry):
        hr, hi = carry
        ce = jnp.where(ph == 0, c, n_chunks - 1 - c)
        rows = pl.ds(pl.multiple_of(ce * bsz, bsz), bsz)
        h_scr[rows, 0:STATE_W] = hr
        h_scr[rows, STATE_W:2 * STATE_W] = hi
        sr = s_scr[rows, 0:STATE_W]
        si = s_scr[rows, STATE_W:2 * STATE_W]
        return alr * hr - ali * hi + sr, alr * hi + ali * hr + si

    hr, hi = lax.fori_loop(0, n_chunks, step, (st_scr[0], st_scr[1]))
    st_scr[0] = hr
    st_scr[1] = hi
    h_bc = jnp.dot(to_bc, h_scr[...].astype(BF16), preferred_element_type=F32).astype(BF16)
    y = (jnp.dot(x, a_scr[...], preferred_element_type=F32)
         + lax.dot_general(h_bc, wct_scr[...], (((1,), (1,)), ((), ())), preferred_element_type=F32))
    for b in range(bsz):
        for s in range(lc):
            y_ref[b, pl.ds(s, n_chunks, stride=lc), :] = y[b * n_chunks:(b + 1) * n_chunks,
                                                           s * LANES:(s + 1) * LANES]


def _ssm(l, u, tables, n_ctx):
    bsz, t_all, _ = u.shape
    lc = SCAN_CHUNK
    tt = SCAN_TIME_TILE
    n_t = t_all // tt
    n_ctx_t = n_ctx // tt
    rows = bsz * tt // lc
    inject, readout, intra, alr, ali = tables
    lk = lc * LANES
    assert 2 * STATE_W == lk and SSM_STATE == 64 and SSM_GROUP == 16 and LANES == 128
    assert bsz & (bsz - 1) == 0 and (tt // lc) & (tt // lc - 1) == 0

    def tile_of(ph, i):
        rev = jnp.where(i < n_ctx_t, n_ctx_t - 1 - i, n_t - 1 - (i - n_ctx_t))
        return jnp.where(ph == 0, i, rev)

    op_spec = lambda r, c: pl.BlockSpec((None, None, None, r, c), lambda j, ph, i: (l, ph, j, 0, 0))
    return pl.pallas_call(
        _ssm_kernel,
        grid=(LANE_TILES, 2, n_t),
        in_specs=[
            pl.BlockSpec((bsz, tt, LANES), lambda j, ph, i: (0, tile_of(ph, i), j)),
            op_spec(lk, LANES), op_spec(lk, LANES), op_spec(lk, LANES),
            op_spec(1, STATE_W), op_spec(1, STATE_W),
        ],
        out_specs=pl.BlockSpec((None, bsz, tt, LANES), lambda j, ph, i: (ph, 0, tile_of(ph, i), j)),
        out_shape=jax.ShapeDtypeStruct((2, bsz, t_all, SSM_W), F32),
        scratch_shapes=[
            pltpu.VMEM((lk, 2 * STATE_W), BF16),
            pltpu.VMEM((lk, 2 * STATE_W), BF16),
            pltpu.VMEM((lk, lk), BF16),
            pltpu.VMEM((rows, 2 * STATE_W), F32),
            pltpu.VMEM((rows, 2 * STATE_W), F32),
            pltpu.VMEM((2, bsz, STATE_W), F32),
        ],
        compiler_params=pltpu.CompilerParams(
            dimension_semantics=("parallel", "arbitrary", "arbitrary"), vmem_limit_bytes=VMEM_LIMIT),
        name="s5_scan",
    )(u, inject, readout, intra, alr, ali)


def _merge_kernel(attn_ref, y_ref, ga_ref, gs_ref, x_ref, g1_ref, sh2_ref, sc2_ref,
                  wpa_ref, wglu_ref, bglu_ref, wps_ref, wo_ref, lng_ref, lnb_ref, wr_ref, br_ref,
                  x1_ref, h2_ref, lg_ref, *, alpha):
    a = jnp.dot(attn_ref[...], wpa_ref[...], preferred_element_type=F32)
    ys = y_ref[0] + y_ref[1]
    gl = ys * (0.5 * (1.0 + jnp.tanh(math.sqrt(2.0 / math.pi) * (ys + 0.044715 * (ys * ys * ys)))))
    z = jnp.dot(gl.astype(BF16), wglu_ref[...], preferred_element_type=F32) + bglu_ref[...]
    sg = gl * _sigmoid(z)
    s = jnp.dot(sg.astype(BF16), wps_ref[...], preferred_element_type=F32)
    m = ga_ref[...].astype(F32) * a + gs_ref[...].astype(F32) * s
    y = jnp.dot(m.astype(BF16), wo_ref[...], preferred_element_type=F32)
    x1 = _layer_norm(alpha * x_ref[...] + g1_ref[...] * y, lng_ref[...], lnb_ref[...])
    x1_ref[...] = x1
    h2 = x1 * (1.0 + sc2_ref[...]) + sh2_ref[...]
    h2_ref[...] = h2
    h_hi = h2.astype(BF16)
    h_lo = (h2 - h_hi.astype(F32)).astype(BF16)
    w_hi, w_lo = wr_ref[0], wr_ref[1]
    lg_ref[...] = (jnp.dot(h_hi, w_hi, preferred_element_type=F32)
                   + (jnp.dot(h_lo, w_hi, preferred_element_type=F32)
                      + jnp.dot(h_hi, w_lo, preferred_element_type=F32))) + br_ref[...]


def _merge(l, attn, y, siga, sigs, xall, mods4, wts, n_ctx, tile0, alpha):
    bsz, t_all, d = xall.shape
    tm = ROW_TILE
    n_ctx_tiles = n_ctx // tm
    n_tiles = t_all // tm - tile0
    t_out = n_tiles * tm
    tok = lambda w: pl.BlockSpec((None, tm, w), lambda b, i: (b, i + tile0, 0))
    own = lambda w: pl.BlockSpec((None, tm, w), lambda b, i: (b, i, 0))
    modv = lambda col: _mod_spec(l, col, bsz, n_ctx_tiles, tile0)
    wpa, wglu, bglu, wps, wo, lng, lnb, wr, br = wts
    kern = functools.partial(_merge_kernel, alpha=alpha)
    return pl.pallas_call(
        kern,
        grid=(bsz, n_tiles),
        in_specs=[
            own(ATTN_W),
            pl.BlockSpec((2, None, tm, SSM_W), lambda b, i: (0, b, i + tile0, 0)),
            tok(D_MODEL), tok(D_MODEL), tok(d),
            modv(2), modv(3), modv(4),
            _layer_mat(wpa, l), _layer_mat(wglu, l), _layer_vec(bglu, l), _layer_mat(wps, l), _layer_mat(wo, l),
            _layer_vec(lng, l), _layer_vec(lnb, l),
            pl.BlockSpec((None,) + wr.shape[1:], lambda b, i: (l, 0, 0, 0)), _layer_vec(br, l),
        ],
        out_specs=[
            own(d),
            pl.BlockSpec((tm, d), lambda b, i: (b * n_tiles + i, 0)),
            pl.BlockSpec((tm, LANES), lambda b, i: (b * n_tiles + i, 0)),
        ],
        out_shape=[
            jax.ShapeDtypeStruct((bsz, t_out, d), F32),
            jax.ShapeDtypeStruct((bsz * t_out, d), F32),
            jax.ShapeDtypeStruct((bsz * t_out, LANES), F32),
        ],
        compiler_params=pltpu.CompilerParams(
            dimension_semantics=("parallel", "arbitrary"), vmem_limit_bytes=VMEM_LIMIT),
        name="merge_ln1",
    )(attn, y, siga, sigs, xall, mods4, mods4, mods4, wpa, wglu, bglu, wps, wo, lng, lnb, wr, br)


def _route_kernel(lg_ref, rec_ref, blk_ref, cnt_scr, start_scr, carry_scr):
    ph = pl.program_id(0)
    i = pl.program_id(1)
    lg = lg_ref[...]
    shape = lg.shape
    lane = lax.broadcasted_iota(jnp.int32, shape, 1).astype(F32)
    far = jnp.full(shape, 1.0e9, F32)

    def first_max(vals, mask):
        vm = jnp.where(mask, vals, NEG_BIG)
        mx = jnp.max(vm, axis=1, keepdims=True)
        idx = jnp.min(jnp.where(mask & (vm == mx), lane, far), axis=1, keepdims=True)
        return mx, idx

    gmask = lane < N_GROUPS
    gmax, gidx = first_max(lg, gmask)
    gtop = 1.0 / jnp.sum(jnp.where(gmask, jnp.exp(lg - gmax), 0.0), axis=1, keepdims=True)
    lo = N_GROUPS + EXPERTS_PER_GROUP * gidx
    emask = (lane >= lo) & (lane < lo + EXPERTS_PER_GROUP)
    v1, i1 = first_max(lg, emask)
    v2, i2 = first_max(lg, emask & (lane != i1))
    e2 = jnp.exp(v2 - v1)
    den = 1.0 + e2
    w1 = (1.0 / den) * gtop
    w2 = (e2 / den) * gtop
    oh1 = (lane == i1 - N_GROUPS).astype(F32)
    oh2 = (lane == i2 - N_GROUPS).astype(F32)
    c1 = jnp.sum(oh1, axis=0, keepdims=True)
    c2 = jnp.sum(oh2, axis=0, keepdims=True)

    @pl.when((ph == 0) & (i == 0))
    def _():
        cnt_scr[...] = jnp.zeros_like(cnt_scr)

    @pl.when(ph == 0)
    def _():
        cnt_scr[...] += c1 + c2

    @pl.when((ph == 1) & (i == 0))
    def _():
        cnt = cnt_scr[...]
        padded = jnp.floor((cnt + (MOE_BLOCK - 1)) * (1.0 / MOE_BLOCK)) * MOE_BLOCK
        r = lax.broadcasted_iota(jnp.int32, (LANES, LANES), 0)
        c = lax.broadcasted_iota(jnp.int32, (LANES, LANES), 1)
        upper = (r < c).astype(F32)
        start = jnp.dot(jnp.broadcast_to(padded, (SUBLANES, LANES)), upper, preferred_element_type=F32,
                        precision=lax.Precision.HIGHEST)[0:1]
        start_scr[...] = start
        carry_scr[...] = jnp.zeros_like(carry_scr)
        nb = blk_ref.shape[0]
        blk_start = (lax.broadcasted_iota(jnp.int32, (nb, LANES), 0) * MOE_BLOCK).astype(F32)
        elane = lax.broadcasted_iota(jnp.int32, (nb, LANES), 1) < N_EXPERTS
        done = jnp.sum(jnp.where(elane & ((start + padded) <= blk_start), 1.0, 0.0), axis=1, keepdims=True)
        blk_ref[...] = jnp.broadcast_to(jnp.minimum(done, N_EXPERTS - 1.0), (nb, LANES)).astype(jnp.int32)

    @pl.when(ph == 1)
    def _():
        tr = shape[0]
        r = lax.broadcasted_iota(jnp.int32, (tr, tr), 0)
        c = lax.broadcasted_iota(jnp.int32, (tr, tr), 1)
        tri = (c < r).astype(BF16)
        base = start_scr[...] + carry_scr[...]
        r1 = jnp.dot(tri, oh1.astype(BF16), preferred_element_type=F32)
        r2 = jnp.dot(tri, oh2.astype(BF16), preferred_element_type=F32) + c1
        d1 = jnp.sum(oh1 * (base + r1), axis=1, keepdims=True)
        d2 = jnp.sum(oh2 * (base + r2), axis=1, keepdims=True)
        carry_scr[...] += c1 + c2
        rec_ref[...] = jnp.where(lane == 0.0, d1, jnp.where(lane == 1.0, d2, jnp.where(lane == 2.0, w1, w2)))


def _route(logits, n_blocks):
    n_tok = logits.shape[0]
    tr = ROUTE_TILE
    nb_pad = -(-n_blocks // SUBLANES) * SUBLANES
    return pl.pallas_call(
        _route_kernel,
        grid=(2, n_tok // tr),
        in_specs=[pl.BlockSpec((tr, LANES), lambda ph, i: (i, 0))],
        out_specs=[
            pl.BlockSpec((tr, LANES), lambda ph, i: (i * ph, 0)),
            pl.BlockSpec((nb_pad, LANES), lambda ph, i: (0, 0)),
        ],
        out_shape=[
            jax.ShapeDtypeStruct((n_tok, LANES), F32),
            jax.ShapeDtypeStruct((nb_pad, LANES), jnp.int32),
        ],
        scratch_shapes=[pltpu.VMEM((1, LANES), F32)] * 3,
        compiler_params=pltpu.CompilerParams(dimension_semantics=("arbitrary", "arbitrary")),
        name="moe_route",
    )(logits)


def _row_copy(src, dst, sem):
    return pltpu.make_async_copy(src, dst, sem)


def _dispatch_kernel(slot_ref, h_ref, xs_in_ref, xs_ref, sem):
    del xs_in_ref
    tg = h_ref.shape[0]

    def issue(r, carry):
        for k in range(2):
            d = slot_ref[0, 0, k * tg + r]
            _row_copy(h_ref.at[pl.ds(r, 1)], xs_ref.at[pl.ds(d, 1)], sem).start(priority=k)
        return carry

    lax.fori_loop(0, tg, issue, 0, unroll=ISSUE_UNROLL)

    def drain(r, carry):
        for k in range(2):
            _row_copy(h_ref.at[pl.ds(0, 1)], xs_ref.at[pl.ds(0, 1)], sem).wait()
        return carry

    lax.fori_loop(0, tg, drain, 0, unroll=ISSUE_UNROLL)


def _dispatch(h2, slots3, cap):
    n_tok, d = h2.shape
    tg = ROW_TILE
    xs0 = jnp.zeros((cap, d), F32)
    return pl.pallas_call(
        _dispatch_kernel,
        grid=(n_tok // tg,),
        in_specs=[
            pl.BlockSpec((1, 1, 2 * tg), lambda i: (i, 0, 0), memory_space=pltpu.SMEM),
            pl.BlockSpec((tg, d), lambda i: (i, 0)),
            pl.BlockSpec(memory_space=pl.ANY),
        ],
        out_specs=pl.BlockSpec(memory_space=pl.ANY),
        out_shape=jax.ShapeDtypeStruct((cap, d), F32),
        scratch_shapes=[pltpu.SemaphoreType.DMA(())],
        input_output_aliases={2: 0},
        compiler_params=pltpu.CompilerParams(dimension_semantics=("arbitrary",), has_side_effects=True),
        name="moe_dispatch",
    )(slots3, h2, xs0)


def _expert_kernel(be_ref, xs_ref, w1_ref, w3_ref, w2_ref, ys_ref, w1b, w3b, w2b):
    i = pl.program_id(0)
    fresh = (i == 0) | (be_ref[i] != be_ref[jnp.maximum(i - 1, 0)])

    @pl.when(fresh)
    def _():
        w1b[...] = w1_ref[...].astype(BF16)
        w3b[...] = w3_ref[...].astype(BF16)
        w2b[...] = w2_ref[...].astype(BF16)

    x = xs_ref[...].astype(BF16)
    a = jnp.dot(x, w1b[...], preferred_element_type=F32)
    b = jnp.dot(x, w3b[...], preferred_element_type=F32)
    hid = (a * _sigmoid(a)) * b
    ys_ref[...] = jnp.dot(hid.astype(BF16), w2b[...], preferred_element_type=F32)


def _experts(l, blk_expert, xs, w1, w3, w2):
    cap, d = xs.shape
    f = w1.shape[-1]
    n_blocks = cap // MOE_BLOCK
    wspec = lambda r, c: pl.BlockSpec((None, None, r, c), lambda i, be: (l, be[i], 0, 0))
    grid_spec = pltpu.PrefetchScalarGridSpec(
        num_scalar_prefetch=1,
        grid=(n_blocks,),
        in_specs=[pl.BlockSpec((MOE_BLOCK, d), lambda i, be: (i, 0)), wspec(d, f), wspec(d, f), wspec(f, d)],
        out_specs=pl.BlockSpec((MOE_BLOCK, d), lambda i, be: (i, 0)),
        scratch_shapes=[pltpu.VMEM((d, f), BF16), pltpu.VMEM((d, f), BF16), pltpu.VMEM((f, d), BF16)],
    )
    return pl.pallas_call(
        _expert_kernel,
        grid_spec=grid_spec,
        out_shape=jax.ShapeDtypeStruct((cap, d), F32),
        compiler_params=pltpu.CompilerParams(
            dimension_semantics=("arbitrary",), vmem_limit_bytes=VMEM_LIMIT),
        name="moe_experts",
    )(blk_expert, xs, w1, w3, w2)


def _combine_kernel(slot_ref, rec_ref, x_ref, g2_ref, lng_ref, lnb_ref, ys_ref, o_ref, buf, sem, *, alpha):
    tm = x_ref.shape[0]

    def issue(r, carry):
        for k in range(2):
            d = slot_ref[0, 0, k * tm + r]
            _row_copy(ys_ref.at[pl.ds(d, 1)], buf.at[k, pl.ds(r, 1)], sem).start(priority=k)
        return carry

    lax.fori_loop(0, tm, issue, 0, unroll=ISSUE_UNROLL)

    def drain(r, carry):
        for k in range(2):
            _row_copy(ys_ref.at[pl.ds(0, 1)], buf.at[0, pl.ds(0, 1)], sem).wait()
        return carry

    lax.fori_loop(0, tm, drain, 0, unroll=ISSUE_UNROLL)
    rec = rec_ref[...]
    f = rec[:, 2:3] * buf[0] + rec[:, 3:4] * buf[1]
    o_ref[...] = _layer_norm(alpha * x_ref[...] + g2_ref[...] * f, lng_ref[...], lnb_ref[...])


def _combine(l, slots3, rec, x1, mods4, lng, lnb, ys, n_ctx, tile0, alpha):
    bsz, t_out, d = x1.shape
    tm = ROW_TILE
    n_ctx_tiles = n_ctx // tm
    n_tiles = t_out // tm
    kern = functools.partial(_combine_kernel, alpha=alpha)
    return pl.pallas_call(
        kern,
        grid=(bsz, n_tiles),
        in_specs=[
            pl.BlockSpec((1, 1, 2 * tm), lambda b, i: (b * n_tiles + i, 0, 0), memory_space=pltpu.SMEM),
            pl.BlockSpec((tm, LANES), lambda b, i: (b * n_tiles + i, 0)),
            pl.BlockSpec((None, tm, d), lambda b, i: (b, i, 0)),
            _mod_spec(l, 5, bsz, n_ctx_tiles, tile0),
            _layer_vec(lng, l), _layer_vec(lnb, l),
            pl.BlockSpec(memory_space=pl.ANY),
        ],
        out_specs=pl.BlockSpec((None, tm, d), lambda b, i: (b, i, 0)),
        out_shape=jax.ShapeDtypeStruct((bsz, t_out, d), F32),
        scratch_shapes=[pltpu.VMEM((2, tm, d), F32), pltpu.SemaphoreType.DMA(())],
        compiler_params=pltpu.CompilerParams(dimension_semantics=("arbitrary", "arbitrary")),
        name="moe_combine_ln2",
    )(slots3, rec, x1, mods4, lng, lnb, ys)


def _rope_tables(n_ctx, n_lat):
    rows = n_lat // GRID_W
    row = jnp.repeat(jnp.arange(rows, dtype=F32), GRID_W)
    col = jnp.tile(jnp.arange(GRID_W, dtype=F32), rows)
    inv = 1.0 / (ROPE_BASE ** (jnp.arange(0, ROPE_AXIS_DIM, 2, dtype=F32) / ROPE_AXIS_DIM))
    half = ROPE_AXIS_DIM // 2
    zeros = jnp.zeros((n_lat, half), F32)
    cos, sa, sb = [], [], []
    for pos in (row, col):
        ang = pos[:, None] * inv
        c, s = jnp.cos(ang), jnp.sin(ang)
        cos += [c, c]
        sa += [-s, zeros]
        sb += [zeros, s]
    reps = LANES // HEAD_DIM

    def full(parts, ctx_fill):
        lat = jnp.tile(jnp.concatenate(parts, axis=1), (1, reps))
        return jnp.concatenate([jnp.full((n_ctx, LANES), ctx_fill, F32), lat], axis=0)

    return full(cos, 1.0), full(sa, 0.0), full(sb, 0.0)


def kernel(x, c, ctx, c_ctx, w_mod, b_mod, w_in, lam_q1, lam_k1, lam_q2, lam_k2, subln_g,
           ssm_a_re, ssm_a_im, ssm_log_dt, ssm_b_re, ssm_b_im, ssm_c_re, ssm_c_im, ssm_d,
           w_glu, b_glu, w_pa, w_ps, w_o, ln1_g, ln1_b,
           router_g_w, router_g_b, router_e_w, router_e_b, moe_w1, moe_w3, moe_w2, ln2_g, ln2_b):
    bsz, n_lat, d = x.shape
    n_ctx = ctx.shape[1]
    depth = w_mod.shape[0]
    assert d == D_MODEL and bsz % SUBLANES == 0
    assert n_ctx % ROW_TILE == 0 and n_lat % ROW_TILE == 0 and n_ctx % SCAN_TIME_TILE == 0
    alpha = (2.0 * depth) ** 0.25
    n_ctx_tiles = n_ctx // ROW_TILE

    mod_rows = -(-(bsz + 1) // SUBLANES) * SUBLANES
    cvec = jnp.concatenate([c, c_ctx[None, :], jnp.zeros((mod_rows - bsz - 1, d), F32)], axis=0)
    mods4 = _modulation(cvec, w_mod, b_mod).reshape(depth, mod_rows, 1, N_MOD * d)
    tabs = _rope_tables(n_ctx, n_lat)
    xall = jnp.concatenate([ctx, x], axis=1)
    vecs = lambda a: a.reshape(depth, 1, a.shape[-1])
    ops = jax.vmap(_ssm_tables)(ssm_a_re, ssm_a_im, ssm_log_dt, ssm_b_re, ssm_b_im, ssm_c_re, ssm_c_im, ssm_d)
    n_pad = LANES - N_GROUPS - N_EXPERTS
    wr = jnp.concatenate([router_g_w, router_e_w, jnp.zeros((depth, d, n_pad), F32)], axis=2)
    br = jnp.concatenate([router_g_b, router_e_b, jnp.zeros((depth, n_pad), F32)], axis=1)
    wr_hi = wr.astype(BF16)
    wr_split = jnp.stack([wr_hi, (wr - wr_hi.astype(F32)).astype(BF16)], axis=1)
    w_in_bf = w_in.astype(BF16)
    wts = (w_pa.astype(BF16), w_glu.astype(BF16), vecs(b_glu), w_ps.astype(BF16), w_o.astype(BF16),
           vecs(ln1_g), vecs(ln1_b), wr_split, vecs(br))
    lam_vecs = (vecs(lam_q1), vecs(lam_k1), vecs(lam_q2), vecs(lam_k2))
    ln2 = (vecs(ln2_g), vecs(ln2_b))
    subln = vecs(subln_g)

    for l in range(depth):
        last = l == depth - 1
        tile0 = n_ctx_tiles if last else 0
        lam_init = 0.8 - 0.6 * math.exp(-0.3 * l)
        k, v, u, q, siga, sigs = _inproj(l, xall, mods4, tabs, w_in_bf, n_ctx)
        attn = _attention(l, q, k, v, lam_vecs, subln, n_ctx, tile0, lam_init)
        y = _ssm(l, u, ops, n_ctx)
        x1, h2, logits = _merge(l, attn, y, siga, sigs, xall, mods4, wts, n_ctx, tile0, alpha)

        n_tok = h2.shape[0]
        n_blocks = -(-(2 * n_tok + N_EXPERTS * (MOE_BLOCK - 1)) // MOE_BLOCK)
        rec, blk = _route(logits, n_blocks)
        n_tiles_tok = n_tok // ROW_TILE
        slots3 = (rec[:, 0:2].astype(jnp.int32).reshape(n_tiles_tok, ROW_TILE, 2)
                  .transpose(0, 2, 1).reshape(n_tiles_tok, 1, 2 * ROW_TILE))
        xs = _dispatch(h2, slots3, n_blocks * MOE_BLOCK)
        ys = _experts(l, blk[:n_blocks, 0], xs, moe_w1, moe_w3, moe_w2)
        xall = _combine(l, slots3, rec, x1, mods4, ln2[0], ln2[1], ys, n_ctx, tile0, alpha)
    return xall
```

```python
import functools
import math

import jax
import jax.numpy as jnp
from jax import lax
from jax.experimental import pallas as pl
from jax.experimental.pallas import tpu as pltpu

F32 = jnp.float32
BF16 = jnp.bfloat16

D_MODEL = 1024
N_HEADS = 8
HEAD_DIM = 64
V_DIM = 2 * HEAD_DIM
QK_W = N_HEADS * 2 * HEAD_DIM
ATTN_W = N_HEADS * V_DIM
SSM_W = D_MODEL // 2
SSM_GROUP = 16
SSM_GROUPS = SSM_W // SSM_GROUP
SSM_STATE = 64
N_GROUPS = 4
EXPERTS_PER_GROUP = 8
N_EXPERTS = N_GROUPS * EXPERTS_PER_GROUP
EXPERT_HIDDEN = D_MODEL // 2
KVU_W = QK_W + ATTN_W + SSM_W
IN_W = KVU_W + QK_W + 2 * D_MODEL
N_MOD = 6
GRID_W = 64
ROPE_BASE = 10000.0
ROPE_AXIS_DIM = HEAD_DIM // 2
LN_EPS = 1e-5

LANES = 128
SUBLANES = 8
VMEM_LIMIT = 56 * 1024 * 1024

ROW_TILE = 256
SCAN_CHUNK = 8
SCAN_TIME_TILE = 256
MOE_BLOCK = 256
ROUTE_TILE = 512
HEADS_PER_STEP = 4
ISSUE_UNROLL = 8
LANE_TILES = SSM_W // LANES
GROUPS_PER_TILE = LANES // SSM_GROUP
STATE_W = GROUPS_PER_TILE * SSM_STATE
NEG_BIG = -3.0e38


def _sigmoid(x):
    return 1.0 / (1.0 + jnp.exp(-x))


def _pack_rows(x):
    w = x.shape[1] // 2
    bits = lax.bitcast_convert_type(x.astype(BF16).astype(F32), jnp.uint32)
    return bits[:, :w] | (bits[:, w:] >> 16)


def _unpack_rows(p):
    hi = lax.bitcast_convert_type(p & jnp.uint32(0xFFFF0000), F32)
    lo = lax.bitcast_convert_type(p << 16, F32)
    return jnp.concatenate([hi, lo], axis=1)


def _layer_norm(x, g, b):
    xc = x - jnp.mean(x, axis=-1, keepdims=True)
    var = jnp.mean(xc * xc, axis=-1, keepdims=True)
    return xc * lax.rsqrt(var + LN_EPS) * g + b


def _layer_vec(arr, l):
    return pl.BlockSpec((None, 1, arr.shape[-1]), lambda b, i: (l, 0, 0))


def _layer_mat(arr, l):
    return pl.BlockSpec((None,) + arr.shape[1:], lambda b, i: (l, 0, 0))


def _mod_spec(l, col, bsz, n_ctx_tiles, tile0):
    return pl.BlockSpec((None, None, 1, D_MODEL),
                        lambda b, i: (l, jnp.where(i + tile0 < n_ctx_tiles, bsz, b), 0, col))


def _mod_kernel(c_ref, w_ref, b_ref, o_ref):
    c = c_ref[...]
    s = c * _sigmoid(c)
    o_ref[...] = jnp.dot(s, w_ref[...], preferred_element_type=F32, precision=lax.Precision.HIGHEST) + b_ref[...]


def _modulation(cvec, w_mod, b_mod):
    depth, d, w6 = w_mod.shape
    rows = cvec.shape[0]
    tn = 1024
    return pl.pallas_call(
        _mod_kernel,
        grid=(depth, w6 // tn),
        in_specs=[
            pl.BlockSpec((rows, d), lambda l, j: (0, 0)),
            pl.BlockSpec((None, d, tn), lambda l, j: (l, 0, j)),
            pl.BlockSpec((None, 1, tn), lambda l, j: (l, 0, j)),
        ],
        out_specs=pl.BlockSpec((None, rows, tn), lambda l, j: (l, 0, j)),
        out_shape=jax.ShapeDtypeStruct((depth, rows, w6), F32),
        name="modulation",
    )(cvec, w_mod, b_mod.reshape(depth, 1, w6))


def _inproj_kernel(x_ref, sh_ref, sc_ref, cos_ref, sa_ref, sb_ref, w_ref,
                   k_ref, v_ref, u_ref, q_ref, ga_ref, gs_ref):
    h = (x_ref[...] * (1.0 + sc_ref[...]) + sh_ref[...]).astype(BF16)
    cos, sa, sb = cos_ref[...], sa_ref[...], sb_ref[...]

    def rope(t):
        return t * cos + pltpu.roll(t, LANES - 16, 1) * sa + pltpu.roll(t, 16, 1) * sb

    def proj(lo, hi):
        return jnp.dot(h, w_ref[:, lo:hi], preferred_element_type=F32)

    kk = proj(0, QK_W)
    for c in range(QK_W // LANES):
        k_ref[:, c * LANES:(c + 1) * LANES] = rope(kk[:, c * LANES:(c + 1) * LANES]).astype(BF16)
    v_ref[...] = proj(QK_W, QK_W + ATTN_W).astype(BF16)
    u_ref[...] = proj(QK_W + ATTN_W, KVU_W)
    qq = proj(KVU_W, KVU_W + QK_W)
    scale = HEAD_DIM ** -0.5 * math.log2(math.e)
    for c in range(QK_W // LANES):
        q_ref[:, c * LANES:(c + 1) * LANES] = (rope(qq[:, c * LANES:(c + 1) * LANES]) * scale).astype(BF16)
    ga_ref[...] = _sigmoid(proj(KVU_W + QK_W, KVU_W + QK_W + D_MODEL)).astype(BF16)
    gs_ref[...] = _sigmoid(proj(KVU_W + QK_W + D_MODEL, IN_W)).astype(BF16)


def _inproj(l, xall, mods4, tabs, w_in_bf, n_ctx):
    bsz, t_all, d = xall.shape
    tm = ROW_TILE
    n_ctx_tiles = n_ctx // tm
    n_tiles = t_all // tm
    tok_spec = lambda w: pl.BlockSpec((None, tm, w), lambda b, i: (b, i, 0))
    tab_spec = pl.BlockSpec((tm, LANES), lambda b, i: (i, 0))
    big = lambda w, dt: jax.ShapeDtypeStruct((bsz, t_all, w), dt)
    return pl.pallas_call(
        _inproj_kernel,
        grid=(bsz, n_tiles),
        in_specs=[
            tok_spec(d),
            _mod_spec(l, 0, bsz, n_ctx_tiles, 0),
            _mod_spec(l, 1, bsz, n_ctx_tiles, 0),
            tab_spec, tab_spec, tab_spec,
            _layer_mat(w_in_bf, l),
        ],
        out_specs=[tok_spec(QK_W), tok_spec(ATTN_W), tok_spec(SSM_W),
                   tok_spec(QK_W), tok_spec(D_MODEL), tok_spec(D_MODEL)],
        out_shape=[big(QK_W, BF16), big(ATTN_W, BF16), big(SSM_W, F32),
                   big(QK_W, BF16), big(D_MODEL, BF16), big(D_MODEL, BF16)],
        compiler_params=pltpu.CompilerParams(
            dimension_semantics=("parallel", "arbitrary"), vmem_limit_bytes=VMEM_LIMIT),
        name="inproj",
    )(xall, mods4, mods4, tabs[0], tabs[1], tabs[2], w_in_bf)


def _attn_kernel(q_ref, k_ref, v_ref, lq1_ref, lk1_ref, lq2_ref, lk2_ref, g_ref, o_ref,
                 *, n_ctx, n_ctx_tiles, tile0, lam_init):
    i = pl.program_id(2) + tile0
    lam = (jnp.exp(jnp.sum(lq1_ref[...] * lk1_ref[...], axis=1, keepdims=True))
           - jnp.exp(jnp.sum(lq2_ref[...] * lk2_ref[...], axis=1, keepdims=True)) + lam_init)
    g = g_ref[...]
    lane = lax.broadcasted_iota(jnp.int32, (q_ref.shape[0], V_DIM), 1)

    def scores(hh, n_kv):
        cols = slice(hh * V_DIM, (hh + 1) * V_DIM)
        q = q_ref[:, cols]
        zero = jnp.zeros_like(q)
        k = k_ref[0:n_kv, cols]
        nt = (((1,), (1,)), ((), ()))
        s1 = lax.dot_general(jnp.where(lane < HEAD_DIM, q, zero), k, nt, preferred_element_type=F32)
        s2 = lax.dot_general(jnp.where(lane < HEAD_DIM, zero, q), k, nt, preferred_element_type=F32)
        return s1, s2

    def finish(hh, n_kv, s1, s2):
        cols = slice(hh * V_DIM, (hh + 1) * V_DIM)

        def probs(s):
            p = jnp.exp2(s - jnp.max(s, axis=-1, keepdims=True))
            return p, 1.0 / jnp.sum(p, axis=-1, keepdims=True)

        p1, r1 = probs(s1)
        p2, r2 = probs(s2)
        a = p1 - p2 * (lam * r2 / r1)
        o = jnp.dot(a.astype(BF16), v_ref[0:n_kv, cols], preferred_element_type=F32) * r1
        o = o * lax.rsqrt(jnp.mean(o * o, axis=-1, keepdims=True) + LN_EPS) * g * (1.0 - lam_init)
        o_ref[:, cols] = o.astype(BF16)

    def attend(n_kv):
        pending = None
        for hh in range(HEADS_PER_STEP):
            cur = scores(hh, n_kv)
            if pending is not None:
                finish(hh - 1, n_kv, *pending)
            pending = cur
        finish(HEADS_PER_STEP - 1, n_kv, *pending)

    if n_ctx_tiles > tile0:
        @pl.when(i < n_ctx_tiles)
        def _():
            attend(n_ctx)

        @pl.when(i >= n_ctx_tiles)
        def _():
            attend(k_ref.shape[0])
    else:
        attend(k_ref.shape[0])


def _attention(l, q, k, v, lam_vecs, subln_g, n_ctx, tile0, lam_init):
    bsz, t_all, _ = q.shape
    tq = ROW_TILE
    hw = HEADS_PER_STEP * V_DIM
    n_tiles = t_all // tq - tile0
    vec = lambda arr: pl.BlockSpec((None, 1, arr.shape[-1]), lambda b, h, i: (l, 0, 0))
    kern = functools.partial(_attn_kernel, n_ctx=n_ctx, n_ctx_tiles=n_ctx // tq, tile0=tile0, lam_init=lam_init)
    return pl.pallas_call(
        kern,
        grid=(bsz, N_HEADS // HEADS_PER_STEP, n_tiles),
        in_specs=[
            pl.BlockSpec((None, tq, hw), lambda b, h, i: (b, i + tile0, h)),
            pl.BlockSpec((None, t_all, hw), lambda b, h, i: (b, 0, h)),
            pl.BlockSpec((None, t_all, hw), lambda b, h, i: (b, 0, h)),
            vec(lam_vecs[0]), vec(lam_vecs[1]), vec(lam_vecs[2]), vec(lam_vecs[3]), vec(subln_g),
        ],
        out_specs=pl.BlockSpec((None, tq, hw), lambda b, h, i: (b, i, h)),
        out_shape=jax.ShapeDtypeStruct((bsz, n_tiles * tq, ATTN_W), BF16),
        compiler_params=pltpu.CompilerParams(
            dimension_semantics=("parallel", "parallel", "arbitrary"), vmem_limit_bytes=VMEM_LIMIT),
        name="diff_attention",
    )(q, k, v, *lam_vecs, subln_g)


def _ssm_tables(a_re, a_im, log_dt, b_re, b_im, c_re, c_im, d_skip):
    lc = SCAN_CHUNK
    lam = lax.complex(a_re.astype(F32), a_im.astype(F32))
    dt = jnp.exp(log_dt.astype(F32))[..., None]
    ldt = lam * dt
    a_bar = jnp.exp(ldt)
    b_bar = ((a_bar - 1.0) / lam)[..., None] * lax.complex(b_re.astype(F32), b_im.astype(F32))
    cm = lax.complex(c_re.astype(F32), c_im.astype(F32))
    steps = jnp.arange(lc + 1, dtype=F32)
    apow = jnp.exp(ldt[None] * steps[:, None, None, None])
    s_idx = jnp.arange(lc)
    lag_f = s_idx[None, :] - s_idx[:, None]
    lk = lc * LANES
    eye_c = jnp.eye(SSM_GROUP, dtype=F32)
    eye_t = jnp.eye(lc, dtype=F32)
    d_g = d_skip.astype(F32).reshape(SSM_GROUPS, SSM_GROUP)

    def rows_of(t):
        t = t.reshape(lc, LANE_TILES, GROUPS_PER_TILE, SSM_GROUP, LANES)
        return t.transpose(1, 0, 2, 3, 4).reshape(LANE_TILES, lk, LANES)

    inject, readout, intra = [], [], []
    for di in range(2):
        pw_in = apow[lc - 1 - s_idx, di] if di == 0 else apow[s_idx, di]
        w = jnp.einsum('sgp,gpi->sgip', pw_in, b_bar[di])
        inject.append(rows_of(jnp.concatenate([w.real, w.imag], axis=-1)))
        pw_out = apow[s_idx + 1, di] if di == 0 else apow[lc - s_idx, di]
        vv = cm[di][None] * pw_out[:, :, None, :]
        readout.append(rows_of(jnp.concatenate([vv.real, -vv.imag], axis=-1)))
        kern = jnp.einsum('gcp,jgp,gpi->jgci', cm[di], apow[:lc, di], b_bar[di]).real
        lag = lag_f if di == 0 else -lag_f
        toe = jnp.where((lag >= 0)[:, :, None, None, None], kern[jnp.clip(lag, 0, lc - 1)], 0.0)
        toe = toe.transpose(0, 2, 4, 1, 3)
        if di == 0:
            toe = toe + jnp.einsum('st,gi,ic->sgitc', eye_t, d_g, eye_c)
        intra.append(rows_of(toe.reshape(lc, SSM_GROUPS, SSM_GROUP, LANES)))
    al = apow[lc].reshape(2, LANE_TILES, 1, STATE_W)
    return jnp.stack(inject), jnp.stack(readout), jnp.stack(intra), al.real, al.imag


def _ssm_kernel(u_ref, wi_ref, wo_ref, wk_ref, alr_ref, ali_ref, y_ref,
                bc_scr, wct_scr, a_scr, s_scr, h_scr, st_scr):
    ph = pl.program_id(1)
    ti = pl.program_id(2)
    lc = SCAN_CHUNK
    bsz = u_ref.shape[0]
    n_chunks = u_ref.shape[1] // lc
    n_rows = bsz * n_chunks
    lk = lc * LANES

    @pl.when(ti == 0)
    def _():
        st_scr[...] = jnp.zeros_like(st_scr)
        q = lax.broadcasted_iota(jnp.int32, (LANES, lk), 0)
        c = lax.broadcasted_iota(jnp.int32, (LANES, lk), 1)
        sel_state = (q == ((c >> 9) << 6) + (c & (SSM_STATE - 1))).astype(BF16)
        sel_tok = (q == ((c >> 7) << 4) + (c & (SSM_GROUP - 1))).astype(BF16)
        row_g = (lax.broadcasted_iota(jnp.int32, (lk, lk), 0) >> 4) & (GROUPS_PER_TILE - 1)
        col = lax.broadcasted_iota(jnp.int32, (lk, lk), 1)
        same_state = row_g == ((col >> 6) & (GROUPS_PER_TILE - 1))
        same_tok = row_g == ((col >> 4) & (GROUPS_PER_TILE - 1))

        def spread(tab_ref, sel, same):
            full = jnp.dot(tab_ref[...].astype(BF16), sel, preferred_element_type=F32)
            return jnp.where(same, full, 0.0).astype(BF16)

        bc_scr[...] = spread(wi_ref, sel_state, same_state)
        wct_scr[...] = spread(wo_ref, sel_state, same_state)
        a_scr[...] = spread(wk_ref, sel_tok, same_tok)

    x = jnp.concatenate(
        [jnp.concatenate([u_ref[b, pl.ds(s, n_chunks, stride=lc), :] for s in range(lc)], axis=1)
         for b in range(bsz)], axis=0).astype(BF16)
    r = lax.broadcasted_iota(jnp.int32, (n_rows, n_rows), 0)
    cc = lax.broadcasted_iota(jnp.int32, (n_rows, n_rows), 1)
    sh_b, sh_c = bsz.bit_length() - 1, n_chunks.bit_length() - 1
    to_cb = (cc == ((r & (bsz - 1)) << sh_c) + (r >> sh_b)).astype(BF16)
    to_bc = (cc == ((r & (n_chunks - 1)) << sh_b) + (r >> sh_c)).astype(BF16)
    x_cb = jnp.dot(to_cb, x, preferred_element_type=F32).astype(BF16)
    s_scr[...] = jnp.dot(x_cb, bc_scr[...], preferred_element_type=F32)
    alr, ali = alr_ref[...], ali_ref[...]

    def step(c, carry):
        hr, hi = carry
        ce = jnp.where(ph == 0, c, n_chunks - 1 - c)
        rows = pl.ds(pl.multiple_of(ce * bsz, bsz), bsz)
        h_scr[rows, 0:STATE_W] = hr
        h_scr[rows, STATE_W:2 * STATE_W] = hi
        sr = s_scr[rows, 0:STATE_W]
        si = s_scr[rows, STATE_W:2 * STATE_W]
        return alr * hr - ali * hi + sr, alr * hi + ali * hr + si

    hr, hi = lax.fori_loop(0, n_chunks, step, (st_scr[0], st_scr[1]))
    st_scr[0] = hr
    st_scr[1] = hi
    h_bc = jnp.dot(to_bc, h_scr[...].astype(BF16), preferred_element_type=F32).astype(BF16)
    y = (jnp.dot(x, a_scr[...], preferred_element_type=F32)
         + lax.dot_general(h_bc, wct_scr[...], (((1,), (1,)), ((), ())), preferred_element_type=F32))
    for b in range(bsz):
        for s in range(lc):
            y_ref[b, pl.ds(s, n_chunks, stride=lc), :] = y[b * n_chunks:(b + 1) * n_chunks,
                                                           s * LANES:(s + 1) * LANES]


def _ssm(l, u, tables, n_ctx):
    bsz, t_all, _ = u.shape
    lc = SCAN_CHUNK
    tt = SCAN_TIME_TILE
    n_t = t_all // tt
    n_ctx_t = n_ctx // tt
    rows = bsz * tt // lc
    inject, readout, intra, alr, ali = tables
    lk = lc * LANES
    assert 2 * STATE_W == lk and SSM_STATE == 64 and SSM_GROUP == 16 and LANES == 128
    assert bsz & (bsz - 1) == 0 and (tt // lc) & (tt // lc - 1) == 0

    def tile_of(ph, i):
        rev = jnp.where(i < n_ctx_t, n_ctx_t - 1 - i, n_t - 1 - (i - n_ctx_t))
        return jnp.where(ph == 0, i, rev)

    op_spec = lambda r, c: pl.BlockSpec((None, None, None, r, c), lambda j, ph, i: (l, ph, j, 0, 0))
    return pl.pallas_call(
        _ssm_kernel,
        grid=(LANE_TILES, 2, n_t),
        in_specs=[
            pl.BlockSpec((bsz, tt, LANES), lambda j, ph, i: (0, tile_of(ph, i), j)),
            op_spec(lk, LANES), op_spec(lk, LANES), op_spec(lk, LANES),
            op_spec(1, STATE_W), op_spec(1, STATE_W),
        ],
        out_specs=pl.BlockSpec((None, bsz, tt, LANES), lambda j, ph, i: (ph, 0, tile_of(ph, i), j)),
        out_shape=jax.ShapeDtypeStruct((2, bsz, t_all, SSM_W), F32),
        scratch_shapes=[
            pltpu.VMEM((lk, 2 * STATE_W), BF16),
            pltpu.VMEM((lk, 2 * STATE_W), BF16),
            pltpu.VMEM((lk, lk), BF16),
            pltpu.VMEM((rows, 2 * STATE_W), F32),
            pltpu.VMEM((rows, 2 * STATE_W), F32),
            pltpu.VMEM((2, bsz, STATE_W), F32),
        ],
        compiler_params=pltpu.CompilerParams(
            dimension_semantics=("parallel", "arbitrary", "arbitrary"), vmem_limit_bytes=VMEM_LIMIT),
        name="s5_scan",
    )(u, inject, readout, intra, alr, ali)


def _merge_kernel(attn_ref, y_ref, ga_ref, gs_ref, x_ref, g1_ref, sh2_ref, sc2_ref,
                  wpa_ref, wglu_ref, bglu_ref, wps_ref, wo_ref, lng_ref, lnb_ref, wr_ref, br_ref,
                  x1_ref, h2_ref, lg_ref, *, alpha):
    a = jnp.dot(attn_ref[...], wpa_ref[...], preferred_element_type=F32)
    ys = y_ref[0] + y_ref[1]
    gl = ys * (0.5 * (1.0 + jnp.tanh(math.sqrt(2.0 / math.pi) * (ys + 0.044715 * (ys * ys * ys)))))
    z = jnp.dot(gl.astype(BF16), wglu_ref[...], preferred_element_type=F32) + bglu_ref[...]
    sg = gl * _sigmoid(z)
    s = jnp.dot(sg.astype(BF16), wps_ref[...], preferred_element_type=F32)
    m = ga_ref[...].astype(F32) * a + gs_ref[...].astype(F32) * s
    y = jnp.dot(m.astype(BF16), wo_ref[...], preferred_element_type=F32)
    x1 = _layer_norm(alpha * x_ref[...] + g1_ref[...] * y, lng_ref[...], lnb_ref[...])
    x1_ref[...] = x1
    h2 = x1 * (1.0 + sc2_ref[...]) + sh2_ref[...]
    h2_ref[...] = _pack_rows(h2)
    h_hi = h2.astype(BF16)
    h_lo = (h2 - h_hi.astype(F32)).astype(BF16)
    w_hi, w_lo = wr_ref[0], wr_ref[1]
    lg_ref[...] = (jnp.dot(h_hi, w_hi, preferred_element_type=F32)
                   + (jnp.dot(h_lo, w_hi, preferred_element_type=F32)
                      + jnp.dot(h_hi, w_lo, preferred_element_type=F32))) + br_ref[...]


def _merge(l, attn, y, siga, sigs, xall, mods4, wts, n_ctx, tile0, alpha):
    bsz, t_all, d = xall.shape
    tm = ROW_TILE
    n_ctx_tiles = n_ctx // tm
    n_tiles = t_all // tm - tile0
    t_out = n_tiles * tm
    tok = lambda w: pl.BlockSpec((None, tm, w), lambda b, i: (b, i + tile0, 0))
    own = lambda w: pl.BlockSpec((None, tm, w), lambda b, i: (b, i, 0))
    modv = lambda col: _mod_spec(l, col, bsz, n_ctx_tiles, tile0)
    wpa, wglu, bglu, wps, wo, lng, lnb, wr, br = wts
    kern = functools.partial(_merge_kernel, alpha=alpha)
    return pl.pallas_call(
        kern,
        grid=(bsz, n_tiles),
        in_specs=[
            own(ATTN_W),
            pl.BlockSpec((2, None, tm, SSM_W), lambda b, i: (0, b, i + tile0, 0)),
            tok(D_MODEL), tok(D_MODEL), tok(d),
            modv(2), modv(3), modv(4),
            _layer_mat(wpa, l), _layer_mat(wglu, l), _layer_vec(bglu, l), _layer_mat(wps, l), _layer_mat(wo, l),
            _layer_vec(lng, l), _layer_vec(lnb, l),
            pl.BlockSpec((None,) + wr.shape[1:], lambda b, i: (l, 0, 0, 0)), _layer_vec(br, l),
        ],
        out_specs=[
            own(d),
            pl.BlockSpec((tm, d // 2), lambda b, i: (b * n_tiles + i, 0)),
            pl.BlockSpec((tm, LANES), lambda b, i: (b * n_tiles + i, 0)),
        ],
        out_shape=[
            jax.ShapeDtypeStruct((bsz, t_out, d), F32),
            jax.ShapeDtypeStruct((bsz * t_out, d // 2), jnp.uint32),
            jax.ShapeDtypeStruct((bsz * t_out, LANES), F32),
        ],
        compiler_params=pltpu.CompilerParams(
            dimension_semantics=("parallel", "arbitrary"), vmem_limit_bytes=VMEM_LIMIT),
        name="merge_ln1",
    )(attn, y, siga, sigs, xall, mods4, mods4, mods4, wpa, wglu, bglu, wps, wo, lng, lnb, wr, br)


def _route_kernel(lg_ref, rec_ref, blk_ref, cnt_scr, start_scr, carry_scr):
    ph = pl.program_id(0)
    i = pl.program_id(1)
    lg = lg_ref[...]
    shape = lg.shape
    lane = lax.broadcasted_iota(jnp.int32, shape, 1).astype(F32)
    far = jnp.full(shape, 1.0e9, F32)

    def first_max(vals, mask):
        vm = jnp.where(mask, vals, NEG_BIG)
        mx = jnp.max(vm, axis=1, keepdims=True)
        idx = jnp.min(jnp.where(mask & (vm == mx), lane, far), axis=1, keepdims=True)
        return mx, idx

    gmask = lane < N_GROUPS
    gmax, gidx = first_max(lg, gmask)
    gtop = 1.0 / jnp.sum(jnp.where(gmask, jnp.exp(lg - gmax), 0.0), axis=1, keepdims=True)
    lo = N_GROUPS + EXPERTS_PER_GROUP * gidx
    emask = (lane >= lo) & (lane < lo + EXPERTS_PER_GROUP)
    v1, i1 = first_max(lg, emask)
    v2, i2 = first_max(lg, emask & (lane != i1))
    e2 = jnp.exp(v2 - v1)
    den = 1.0 + e2
    w1 = (1.0 / den) * gtop
    w2 = (e2 / den) * gtop
    oh1 = (lane == i1 - N_GROUPS).astype(F32)
    oh2 = (lane == i2 - N_GROUPS).astype(F32)
    c1 = jnp.sum(oh1, axis=0, keepdims=True)
    c2 = jnp.sum(oh2, axis=0, keepdims=True)

    @pl.when((ph == 0) & (i == 0))
    def _():
        cnt_scr[...] = jnp.zeros_like(cnt_scr)

    @pl.when(ph == 0)
    def _():
        cnt_scr[...] += c1 + c2

    @pl.when((ph == 1) & (i == 0))
    def _():
        cnt = cnt_scr[...]
        padded = jnp.floor((cnt + (MOE_BLOCK - 1)) * (1.0 / MOE_BLOCK)) * MOE_BLOCK
        r = lax.broadcasted_iota(jnp.int32, (LANES, LANES), 0)
        c = lax.broadcasted_iota(jnp.int32, (LANES, LANES), 1)
        upper = (r < c).astype(F32)
        start = jnp.dot(jnp.broadcast_to(padded, (SUBLANES, LANES)), upper, preferred_element_type=F32,
                        precision=lax.Precision.HIGHEST)[0:1]
        start_scr[...] = start
        carry_scr[...] = jnp.zeros_like(carry_scr)
        nb = blk_ref.shape[0]
        blk_start = (lax.broadcasted_iota(jnp.int32, (nb, LANES), 0) * MOE_BLOCK).astype(F32)
        elane = lax.broadcasted_iota(jnp.int32, (nb, LANES), 1) < N_EXPERTS
        done = jnp.sum(jnp.where(elane & ((start + padded) <= blk_start), 1.0, 0.0), axis=1, keepdims=True)
        blk_ref[...] = jnp.broadcast_to(jnp.minimum(done, N_EXPERTS - 1.0), (nb, LANES)).astype(jnp.int32)

    @pl.when(ph == 1)
    def _():
        tr = shape[0]
        r = lax.broadcasted_iota(jnp.int32, (tr, tr), 0)
        c = lax.broadcasted_iota(jnp.int32, (tr, tr), 1)
        tri = (c < r).astype(BF16)
        base = start_scr[...] + carry_scr[...]
        r1 = jnp.dot(tri, oh1.astype(BF16), preferred_element_type=F32)
        r2 = jnp.dot(tri, oh2.astype(BF16), preferred_element_type=F32) + c1
        d1 = jnp.sum(oh1 * (base + r1), axis=1, keepdims=True)
        d2 = jnp.sum(oh2 * (base + r2), axis=1, keepdims=True)
        carry_scr[...] += c1 + c2
        rec_ref[...] = jnp.where(lane == 0.0, d1, jnp.where(lane == 1.0, d2, jnp.where(lane == 2.0, w1, w2)))


def _route(logits, n_blocks):
    n_tok = logits.shape[0]
    tr = ROUTE_TILE
    nb_pad = -(-n_blocks // SUBLANES) * SUBLANES
    return pl.pallas_call(
        _route_kernel,
        grid=(2, n_tok // tr),
        in_specs=[pl.BlockSpec((tr, LANES), lambda ph, i: (i, 0))],
        out_specs=[
            pl.BlockSpec((tr, LANES), lambda ph, i: (i * ph, 0)),
            pl.BlockSpec((nb_pad, LANES), lambda ph, i: (0, 0)),
        ],
        out_shape=[
            jax.ShapeDtypeStruct((n_tok, LANES), F32),
            jax.ShapeDtypeStruct((nb_pad, LANES), jnp.int32),
        ],
        scratch_shapes=[pltpu.VMEM((1, LANES), F32)] * 3,
        compiler_params=pltpu.CompilerParams(dimension_semantics=("arbitrary", "arbitrary")),
        name="moe_route",
    )(logits)


def _row_copy(src, dst, sem):
    return pltpu.make_async_copy(src, dst, sem)


def _dispatch_kernel(slot_ref, h_ref, xs_in_ref, xs_ref, sem):
    del xs_in_ref
    tg = h_ref.shape[0]

    def issue(r, carry):
        for k in range(2):
            d = slot_ref[0, 0, k * tg + r]
            _row_copy(h_ref.at[pl.ds(r, 1)], xs_ref.at[pl.ds(d, 1)], sem).start(priority=k)
        return carry

    lax.fori_loop(0, tg, issue, 0, unroll=ISSUE_UNROLL)

    def drain(r, carry):
        for k in range(2):
            _row_copy(h_ref.at[pl.ds(0, 1)], xs_ref.at[pl.ds(0, 1)], sem).wait()
        return carry

    lax.fori_loop(0, tg, drain, 0, unroll=ISSUE_UNROLL)


def _dispatch(h2, slots3, cap):
    n_tok, d = h2.shape
    tg = ROW_TILE
    xs0 = jnp.zeros((cap, d), h2.dtype)
    return pl.pallas_call(
        _dispatch_kernel,
        grid=(n_tok // tg,),
        in_specs=[
            pl.BlockSpec((1, 1, 2 * tg), lambda i: (i, 0, 0), memory_space=pltpu.SMEM),
            pl.BlockSpec((tg, d), lambda i: (i, 0)),
            pl.BlockSpec(memory_space=pl.ANY),
        ],
        out_specs=pl.BlockSpec(memory_space=pl.ANY),
        out_shape=jax.ShapeDtypeStruct((cap, d), h2.dtype),
        scratch_shapes=[pltpu.SemaphoreType.DMA(())],
        input_output_aliases={2: 0},
        compiler_params=pltpu.CompilerParams(dimension_semantics=("arbitrary",), has_side_effects=True),
        name="moe_dispatch",
    )(slots3, h2, xs0)


def _expert_kernel(be_ref, xs_ref, w1_ref, w3_ref, w2_ref, ys_ref, w1b, w3b, w2b):
    i = pl.program_id(0)
    fresh = (i == 0) | (be_ref[i] != be_ref[jnp.maximum(i - 1, 0)])

    @pl.when(fresh)
    def _():
        w1b[...] = w1_ref[...].astype(BF16)
        w3b[...] = w3_ref[...].astype(BF16)
        w2b[...] = w2_ref[...].astype(BF16)

    x = _unpack_rows(xs_ref[...]).astype(BF16)
    a = jnp.dot(x, w1b[...], preferred_element_type=F32)
    b = jnp.dot(x, w3b[...], preferred_element_type=F32)
    hid = (a * _sigmoid(a)) * b
    ys_ref[...] = _pack_rows(jnp.dot(hid.astype(BF16), w2b[...], preferred_element_type=F32))


def _experts(l, blk_expert, xs, w1, w3, w2):
    cap, dp = xs.shape
    d, f = w1.shape[-2:]
    n_blocks = cap // MOE_BLOCK
    wspec = lambda r, c: pl.BlockSpec((None, None, r, c), lambda i, be: (l, be[i], 0, 0))
    grid_spec = pltpu.PrefetchScalarGridSpec(
        num_scalar_prefetch=1,
        grid=(n_blocks,),
        in_specs=[pl.BlockSpec((MOE_BLOCK, dp), lambda i, be: (i, 0)), wspec(d, f), wspec(d, f), wspec(f, d)],
        out_specs=pl.BlockSpec((MOE_BLOCK, dp), lambda i, be: (i, 0)),
        scratch_shapes=[pltpu.VMEM((d, f), BF16), pltpu.VMEM((d, f), BF16), pltpu.VMEM((f, d), BF16)],
    )
    return pl.pallas_call(
        _expert_kernel,
        grid_spec=grid_spec,
        out_shape=jax.ShapeDtypeStruct((cap, dp), xs.dtype),
        compiler_params=pltpu.CompilerParams(
            dimension_semantics=("arbitrary",), vmem_limit_bytes=VMEM_LIMIT),
        name="moe_experts",
    )(blk_expert, xs, w1, w3, w2)


def _combine_kernel(slot_ref, rec_ref, x_ref, g2_ref, lng_ref, lnb_ref, ys_ref, o_ref, buf, sem, *, alpha):
    tm = x_ref.shape[0]

    def issue(r, carry):
        for k in range(2):
            d = slot_ref[0, 0, k * tm + r]
            _row_copy(ys_ref.at[pl.ds(d, 1)], buf.at[k, pl.ds(r, 1)], sem).start(priority=k)
        return carry

    lax.fori_loop(0, tm, issue, 0, unroll=ISSUE_UNROLL)

    def drain(r, carry):
        for k in range(2):
            _row_copy(ys_ref.at[pl.ds(0, 1)], buf.at[0, pl.ds(0, 1)], sem).wait()
        return carry

    lax.fori_loop(0, tm, drain, 0, unroll=ISSUE_UNROLL)
    rec = rec_ref[...]
    f = rec[:, 2:3] * _unpack_rows(buf[0]) + rec[:, 3:4] * _unpack_rows(buf[1])
    o_ref[...] = _layer_norm(alpha * x_ref[...] + g2_ref[...] * f, lng_ref[...], lnb_ref[...])


def _combine(l, slots3, rec, x1, mods4, lng, lnb, ys, n_ctx, tile0, alpha):
    bsz, t_out, d = x1.shape
    tm = ROW_TILE
    n_ctx_tiles = n_ctx // tm
    n_tiles = t_out // tm
    kern = functools.partial(_combine_kernel, alpha=alpha)
    return pl.pallas_call(
        kern,
        grid=(bsz, n_tiles),
        in_specs=[
            pl.BlockSpec((1, 1, 2 * tm), lambda b, i: (b * n_tiles + i, 0, 0), memory_space=pltpu.SMEM),
            pl.BlockSpec((tm, LANES), lambda b, i: (b * n_tiles + i, 0)),
            pl.BlockSpec((None, tm, d), lambda b, i: (b, i, 0)),
            _mod_spec(l, 5, bsz, n_ctx_tiles, tile0),
            _layer_vec(lng, l), _layer_vec(lnb, l),
            pl.BlockSpec(memory_space=pl.ANY),
        ],
        out_specs=pl.BlockSpec((None, tm, d), lambda b, i: (b, i, 0)),
        out_shape=jax.ShapeDtypeStruct((bsz, t_out, d), F32),
        scratch_shapes=[pltpu.VMEM((2, tm) + ys.shape[1:], ys.dtype), pltpu.SemaphoreType.DMA(())],
        compiler_params=pltpu.CompilerParams(dimension_semantics=("arbitrary", "arbitrary")),
        name="moe_combine_ln2",
    )(slots3, rec, x1, mods4, lng, lnb, ys)


def _rope_tables(n_ctx, n_lat):
    rows = n_lat // GRID_W
    row = jnp.repeat(jnp.arange(rows, dtype=F32), GRID_W)
    col = jnp.tile(jnp.arange(GRID_W, dtype=F32), rows)
    inv = 1.0 / (ROPE_BASE ** (jnp.arange(0, ROPE_AXIS_DIM, 2, dtype=F32) / ROPE_AXIS_DIM))
    half = ROPE_AXIS_DIM // 2
    zeros = jnp.zeros((n_lat, half), F32)
    cos, sa, sb = [], [], []
    for pos in (row, col):
        ang = pos[:, None] * inv
        c, s = jnp.cos(ang), jnp.sin(ang)
        cos += [c, c]
        sa += [-s, zeros]
        sb += [zeros, s]
    reps = LANES // HEAD_DIM

    def full(parts, ctx_fill):
        lat = jnp.tile(jnp.concatenate(parts, axis=1), (1, reps))
        return jnp.concatenate([jnp.full((n_ctx, LANES), ctx_fill, F32), lat], axis=0)

    return full(cos, 1.0), full(sa, 0.0), full(sb, 0.0)


def kernel(x, c, ctx, c_ctx, w_mod, b_mod, w_in, lam_q1, lam_k1, lam_q2, lam_k2, subln_g,
           ssm_a_re, ssm_a_im, ssm_log_dt, ssm_b_re, ssm_b_im, ssm_c_re, ssm_c_im, ssm_d,
           w_glu, b_glu, w_pa, w_ps, w_o, ln1_g, ln1_b,
           router_g_w, router_g_b, router_e_w, router_e_b, moe_w1, moe_w3, moe_w2, ln2_g, ln2_b):
    bsz, n_lat, d = x.shape
    n_ctx = ctx.shape[1]
    depth = w_mod.shape[0]
    assert d == D_MODEL and bsz % SUBLANES == 0
    assert n_ctx % ROW_TILE == 0 and n_lat % ROW_TILE == 0 and n_ctx % SCAN_TIME_TILE == 0
    alpha = (2.0 * depth) ** 0.25
    n_ctx_tiles = n_ctx // ROW_TILE

    mod_rows = -(-(bsz + 1) // SUBLANES) * SUBLANES
    cvec = jnp.concatenate([c, c_ctx[None, :], jnp.zeros((mod_rows - bsz - 1, d), F32)], axis=0)
    mods4 = _modulation(cvec, w_mod, b_mod).reshape(depth, mod_rows, 1, N_MOD * d)
    tabs = _rope_tables(n_ctx, n_lat)
    xall = jnp.concatenate([ctx, x], axis=1)
    vecs = lambda a: a.reshape(depth, 1, a.shape[-1])
    ops = jax.vmap(_ssm_tables)(ssm_a_re, ssm_a_im, ssm_log_dt, ssm_b_re, ssm_b_im, ssm_c_re, ssm_c_im, ssm_d)
    n_pad = LANES - N_GROUPS - N_EXPERTS
    wr = jnp.concatenate([router_g_w, router_e_w, jnp.zeros((depth, d, n_pad), F32)], axis=2)
    br = jnp.concatenate([router_g_b, router_e_b, jnp.zeros((depth, n_pad), F32)], axis=1)
    wr_hi = wr.astype(BF16)
    wr_split = jnp.stack([wr_hi, (wr - wr_hi.astype(F32)).astype(BF16)], axis=1)
    w_in_bf = w_in.astype(BF16)
    wts = (w_pa.astype(BF16), w_glu.astype(BF16), vecs(b_glu), w_ps.astype(BF16), w_o.astype(BF16),
           vecs(ln1_g), vecs(ln1_b), wr_split, vecs(br))
    lam_vecs = (vecs(lam_q1), vecs(lam_k1), vecs(lam_q2), vecs(lam_k2))
    ln2 = (vecs(ln2_g), vecs(ln2_b))
    subln = vecs(subln_g)

    for l in range(depth):
        last = l == depth - 1
        tile0 = n_ctx_tiles if last else 0
        lam_init = 0.8 - 0.6 * math.exp(-0.3 * l)
        k, v, u, q, siga, sigs = _inproj(l, xall, mods4, tabs, w_in_bf, n_ctx)
        attn = _attention(l, q, k, v, lam_vecs, subln, n_ctx, tile0, lam_init)
        y = _ssm(l, u, ops, n_ctx)
        x1, h2, logits = _merge(l, attn, y, siga, sigs, xall, mods4, wts, n_ctx, tile0, alpha)

        n_tok = h2.shape[0]
        n_blocks = -(-(2 * n_tok + N_EXPERTS * (MOE_BLOCK - 1)) // MOE_BLOCK)
        rec, blk = _route(logits, n_blocks)
        n_tiles_tok = n_tok // ROW_TILE
        slots3 = (rec[:, 0:2].astype(jnp.int32).reshape(n_tiles_tok, ROW_TILE, 2)
                  .transpose(0, 2, 1).reshape(n_tiles_tok, 1, 2 * ROW_TILE))
        xs = _dispatch(h2, slots3, n_blocks * MOE_BLOCK)
        ys = _experts(l, blk[:n_blocks, 0], xs, moe_w1, moe_w3, moe_w2)
        xall = _combine(l, slots3, rec, x1, mods4, ln2[0], ln2[1], ys, n_ctx, tile0, alpha)
    return xall
```

```python
import functools
import math

import jax
import jax.numpy as jnp
from jax import lax
from jax.experimental import pallas as pl
from jax.experimental.pallas import tpu as pltpu

F32 = jnp.float32
BF16 = jnp.bfloat16

D_MODEL = 1024
N_HEADS = 8
HEAD_DIM = 64
V_DIM = 2 * HEAD_DIM
QK_W = N_HEADS * 2 * HEAD_DIM
ATTN_W = N_HEADS * V_DIM
SSM_W = D_MODEL // 2
SSM_GROUP = 16
SSM_GROUPS = SSM_W // SSM_GROUP
SSM_STATE = 64
N_GROUPS = 4
EXPERTS_PER_GROUP = 8
N_EXPERTS = N_GROUPS * EXPERTS_PER_GROUP
EXPERT_HIDDEN = D_MODEL // 2
KVU_W = QK_W + ATTN_W + SSM_W
IN_W = KVU_W + QK_W + 2 * D_MODEL
N_MOD = 6
GRID_W = 64
ROPE_BASE = 10000.0
ROPE_AXIS_DIM = HEAD_DIM // 2
LN_EPS = 1e-5

LANES = 128
SUBLANES = 8
VMEM_LIMIT = 56 * 1024 * 1024

ROW_TILE = 256
SCAN_CHUNK = 8
SCAN_TIME_TILE = 256
MOE_BLOCK = 256
ROUTE_TILE = 512
HEADS_PER_STEP = 4
ISSUE_UNROLL = 8
LANE_TILES = SSM_W // LANES
GROUPS_PER_TILE = LANES // SSM_GROUP
STATE_W = GROUPS_PER_TILE * SSM_STATE
NEG_BIG = -3.0e38


def _sigmoid(x):
    return 1.0 / (1.0 + jnp.exp(-x))


def _pack_rows(x):
    w = x.shape[1] // 2
    bits = lax.bitcast_convert_type(x.astype(BF16).astype(F32), jnp.uint32)
    return bits[:, :w] | (bits[:, w:] >> 16)


def _unpack_rows(p):
    hi = lax.bitcast_convert_type(p & jnp.uint32(0xFFFF0000), F32)
    lo = lax.bitcast_convert_type(p << 16, F32)
    return jnp.concatenate([hi, lo], axis=1)


def _layer_norm(x, g, b):
    xc = x - jnp.mean(x, axis=-1, keepdims=True)
    var = jnp.mean(xc * xc, axis=-1, keepdims=True)
    return xc * lax.rsqrt(var + LN_EPS) * g + b


def _layer_vec(arr, l):
    return pl.BlockSpec((None, 1, arr.shape[-1]), lambda b, i: (l, 0, 0))


def _layer_mat(arr, l):
    return pl.BlockSpec((None,) + arr.shape[1:], lambda b, i: (l, 0, 0))


def _mod_spec(l, col, bsz, n_ctx_tiles, tile0):
    return pl.BlockSpec((None, None, 1, D_MODEL),
                        lambda b, i: (l, jnp.where(i + tile0 < n_ctx_tiles, bsz, b), 0, col))


def _mod_kernel(c_ref, w_ref, b_ref, o_ref):
    c = c_ref[...]
    s = c * _sigmoid(c)
    o_ref[...] = jnp.dot(s, w_ref[...], preferred_element_type=F32, precision=lax.Precision.HIGHEST) + b_ref[...]


def _modulation(cvec, w_mod, b_mod):
    depth, d, w6 = w_mod.shape
    rows = cvec.shape[0]
    tn = 1024
    return pl.pallas_call(
        _mod_kernel,
        grid=(depth, w6 // tn),
        in_specs=[
            pl.BlockSpec((rows, d), lambda l, j: (0, 0)),
            pl.BlockSpec((None, d, tn), lambda l, j: (l, 0, j)),
            pl.BlockSpec((None, 1, tn), lambda l, j: (l, 0, j)),
        ],
        out_specs=pl.BlockSpec((None, rows, tn), lambda l, j: (l, 0, j)),
        out_shape=jax.ShapeDtypeStruct((depth, rows, w6), F32),
        name="modulation",
    )(cvec, w_mod, b_mod.reshape(depth, 1, w6))


def _inproj_kernel(x_ref, sh_ref, sc_ref, cos_ref, sa_ref, sb_ref, w_ref,
                   k_ref, v_ref, u_ref, q_ref, ga_ref, gs_ref):
    _inproj_body(x_ref[...], sh_ref, sc_ref, cos_ref, sa_ref, sb_ref, w_ref,
                 k_ref, v_ref, u_ref, q_ref, ga_ref, gs_ref)


def _inproj_body(x, sh_ref, sc_ref, cos_ref, sa_ref, sb_ref, w_ref, k_ref, v_ref, u_ref, q_ref, ga_ref, gs_ref):
    h = (x * (1.0 + sc_ref[...]) + sh_ref[...]).astype(BF16)
    cos, sa, sb = cos_ref[...], sa_ref[...], sb_ref[...]

    def rope(t):
        return t * cos + pltpu.roll(t, LANES - 16, 1) * sa + pltpu.roll(t, 16, 1) * sb

    def proj(lo, hi):
        return jnp.dot(h, w_ref[:, lo:hi], preferred_element_type=F32)

    kk = proj(0, QK_W)
    for c in range(QK_W // LANES):
        k_ref[:, c * LANES:(c + 1) * LANES] = rope(kk[:, c * LANES:(c + 1) * LANES]).astype(BF16)
    v_ref[...] = proj(QK_W, QK_W + ATTN_W).astype(BF16)
    u_ref[...] = proj(QK_W + ATTN_W, KVU_W)
    qq = proj(KVU_W, KVU_W + QK_W)
    scale = HEAD_DIM ** -0.5 * math.log2(math.e)
    for c in range(QK_W // LANES):
        q_ref[:, c * LANES:(c + 1) * LANES] = (rope(qq[:, c * LANES:(c + 1) * LANES]) * scale).astype(BF16)
    ga_ref[...] = _sigmoid(proj(KVU_W + QK_W, KVU_W + QK_W + D_MODEL)).astype(BF16)
    gs_ref[...] = _sigmoid(proj(KVU_W + QK_W + D_MODEL, IN_W)).astype(BF16)


def _inproj_specs(l, bsz, t_all, mods4, tabs, w_in_bf, n_ctx):
    tm = ROW_TILE
    n_ctx_tiles = n_ctx // tm
    tok_spec = lambda w: pl.BlockSpec((None, tm, w), lambda b, i: (b, i, 0))
    tab_spec = pl.BlockSpec((tm, LANES), lambda b, i: (i, 0))
    big = lambda w, dt: jax.ShapeDtypeStruct((bsz, t_all, w), dt)
    in_specs = [_mod_spec(l, 0, bsz, n_ctx_tiles, 0), _mod_spec(l, 1, bsz, n_ctx_tiles, 0),
                tab_spec, tab_spec, tab_spec, _layer_mat(w_in_bf, l)]
    out_specs = [tok_spec(QK_W), tok_spec(ATTN_W), tok_spec(SSM_W),
                 tok_spec(QK_W), tok_spec(D_MODEL), tok_spec(D_MODEL)]
    out_shape = [big(QK_W, BF16), big(ATTN_W, BF16), big(SSM_W, F32),
                 big(QK_W, BF16), big(D_MODEL, BF16), big(D_MODEL, BF16)]
    return in_specs, out_specs, out_shape, (mods4, mods4, tabs[0], tabs[1], tabs[2], w_in_bf)


def _inproj(l, xall, mods4, tabs, w_in_bf, n_ctx):
    bsz, t_all, d = xall.shape
    tm = ROW_TILE
    in_specs, out_specs, out_shape, args = _inproj_specs(l, bsz, t_all, mods4, tabs, w_in_bf, n_ctx)
    return pl.pallas_call(
        _inproj_kernel,
        grid=(bsz, t_all // tm),
        in_specs=[pl.BlockSpec((None, tm, d), lambda b, i: (b, i, 0))] + in_specs,
        out_specs=out_specs,
        out_shape=out_shape,
        compiler_params=pltpu.CompilerParams(
            dimension_semantics=("parallel", "arbitrary"), vmem_limit_bytes=VMEM_LIMIT),
        name="inproj",
    )(xall, *args)


def _attn_kernel(q_ref, k_ref, v_ref, lq1_ref, lk1_ref, lq2_ref, lk2_ref, g_ref, o_ref,
                 *, n_ctx, n_ctx_tiles, tile0, lam_init):
    i = pl.program_id(2) + tile0
    lam = (jnp.exp(jnp.sum(lq1_ref[...] * lk1_ref[...], axis=1, keepdims=True))
           - jnp.exp(jnp.sum(lq2_ref[...] * lk2_ref[...], axis=1, keepdims=True)) + lam_init)
    g = g_ref[...]
    lane = lax.broadcasted_iota(jnp.int32, (q_ref.shape[0], V_DIM), 1)

    def scores(hh, n_kv):
        cols = slice(hh * V_DIM, (hh + 1) * V_DIM)
        q = q_ref[:, cols]
        zero = jnp.zeros_like(q)
        k = k_ref[0:n_kv, cols]
        nt = (((1,), (1,)), ((), ()))
        s1 = lax.dot_general(jnp.where(lane < HEAD_DIM, q, zero), k, nt, preferred_element_type=F32)
        s2 = lax.dot_general(jnp.where(lane < HEAD_DIM, zero, q), k, nt, preferred_element_type=F32)
        return s1, s2

    def finish(hh, n_kv, s1, s2):
        cols = slice(hh * V_DIM, (hh + 1) * V_DIM)

        def probs(s):
            p = jnp.exp2(s - jnp.max(s, axis=-1, keepdims=True))
            return p, 1.0 / jnp.sum(p, axis=-1, keepdims=True)

        p1, r1 = probs(s1)
        p2, r2 = probs(s2)
        a = p1 - p2 * (lam * r2 / r1)
        o = jnp.dot(a.astype(BF16), v_ref[0:n_kv, cols], preferred_element_type=F32) * r1
        o = o * lax.rsqrt(jnp.mean(o * o, axis=-1, keepdims=True) + LN_EPS) * g * (1.0 - lam_init)
        o_ref[:, cols] = o.astype(BF16)

    def attend(n_kv):
        pending = None
        for hh in range(HEADS_PER_STEP):
            cur = scores(hh, n_kv)
            if pending is not None:
                finish(hh - 1, n_kv, *pending)
            pending = cur
        finish(HEADS_PER_STEP - 1, n_kv, *pending)

    if n_ctx_tiles > tile0:
        @pl.when(i < n_ctx_tiles)
        def _():
            attend(n_ctx)

        @pl.when(i >= n_ctx_tiles)
        def _():
            attend(k_ref.shape[0])
    else:
        attend(k_ref.shape[0])


def _attention(l, q, k, v, lam_vecs, subln_g, n_ctx, tile0, lam_init):
    bsz, t_all, _ = q.shape
    tq = ROW_TILE
    hw = HEADS_PER_STEP * V_DIM
    n_tiles = t_all // tq - tile0
    vec = lambda arr: pl.BlockSpec((None, 1, arr.shape[-1]), lambda b, h, i: (l, 0, 0))
    kern = functools.partial(_attn_kernel, n_ctx=n_ctx, n_ctx_tiles=n_ctx // tq, tile0=tile0, lam_init=lam_init)
    return pl.pallas_call(
        kern,
        grid=(bsz, N_HEADS // HEADS_PER_STEP, n_tiles),
        in_specs=[
            pl.BlockSpec((None, tq, hw), lambda b, h, i: (b, i + tile0, h)),
            pl.BlockSpec((None, t_all, hw), lambda b, h, i: (b, 0, h)),
            pl.BlockSpec((None, t_all, hw), lambda b, h, i: (b, 0, h)),
            vec(lam_vecs[0]), vec(lam_vecs[1]), vec(lam_vecs[2]), vec(lam_vecs[3]), vec(subln_g),
        ],
        out_specs=pl.BlockSpec((None, tq, hw), lambda b, h, i: (b, i, h)),
        out_shape=jax.ShapeDtypeStruct((bsz, n_tiles * tq, ATTN_W), BF16),
        compiler_params=pltpu.CompilerParams(
            dimension_semantics=("parallel", "parallel", "arbitrary"), vmem_limit_bytes=VMEM_LIMIT),
        name="diff_attention",
    )(q, k, v, *lam_vecs, subln_g)


def _ssm_tables(a_re, a_im, log_dt, b_re, b_im, c_re, c_im, d_skip):
    lc = SCAN_CHUNK
    lam = lax.complex(a_re.astype(F32), a_im.astype(F32))
    dt = jnp.exp(log_dt.astype(F32))[..., None]
    ldt = lam * dt
    a_bar = jnp.exp(ldt)
    b_bar = ((a_bar - 1.0) / lam)[..., None] * lax.complex(b_re.astype(F32), b_im.astype(F32))
    cm = lax.complex(c_re.astype(F32), c_im.astype(F32))
    steps = jnp.arange(lc + 1, dtype=F32)
    apow = jnp.exp(ldt[None] * steps[:, None, None, None])
    s_idx = jnp.arange(lc)
    lag_f = s_idx[None, :] - s_idx[:, None]
    lk = lc * LANES
    eye_c = jnp.eye(SSM_GROUP, dtype=F32)
    eye_t = jnp.eye(lc, dtype=F32)
    d_g = d_skip.astype(F32).reshape(SSM_GROUPS, SSM_GROUP)

    def rows_of(t):
        t = t.reshape(lc, LANE_TILES, GROUPS_PER_TILE, SSM_GROUP, LANES)
        return t.transpose(1, 0, 2, 3, 4).reshape(LANE_TILES, lk, LANES)

    inject, readout, intra = [], [], []
    for di in range(2):
        pw_in = apow[lc - 1 - s_idx, di] if di == 0 else apow[s_idx, di]
        w = jnp.einsum('sgp,gpi->sgip', pw_in, b_bar[di])
        inject.append(rows_of(jnp.concatenate([w.real, w.imag], axis=-1)))
        pw_out = apow[s_idx + 1, di] if di == 0 else apow[lc - s_idx, di]
        vv = cm[di][None] * pw_out[:, :, None, :]
        readout.append(rows_of(jnp.concatenate([vv.real, -vv.imag], axis=-1)))
        kern = jnp.einsum('gcp,jgp,gpi->jgci', cm[di], apow[:lc, di], b_bar[di]).real
        lag = lag_f if di == 0 else -lag_f
        toe = jnp.where((lag >= 0)[:, :, None, None, None], kern[jnp.clip(lag, 0, lc - 1)], 0.0)
        toe = toe.transpose(0, 2, 4, 1, 3)
        if di == 0:
            toe = toe + jnp.einsum('st,gi,ic->sgitc', eye_t, d_g, eye_c)
        intra.append(rows_of(toe.reshape(lc, SSM_GROUPS, SSM_GROUP, LANES)))
    al = apow[lc].reshape(2, LANE_TILES, 1, STATE_W)
    return jnp.stack(inject), jnp.stack(readout), jnp.stack(intra), al.real, al.imag


def _ssm_kernel(u_ref, wi_ref, wo_ref, wk_ref, alr_ref, ali_ref, y_ref,
                bc_scr, wct_scr, a_scr, s_scr, h_scr, st_scr):
    ph = pl.program_id(1)
    ti = pl.program_id(2)
    lc = SCAN_CHUNK
    bsz = u_ref.shape[0]
    n_chunks = u_ref.shape[1] // lc
    n_rows = bsz * n_chunks
    lk = lc * LANES

    @pl.when(ti == 0)
    def _():
        st_scr[...] = jnp.zeros_like(st_scr)
        q = lax.broadcasted_iota(jnp.int32, (LANES, lk), 0)
        c = lax.broadcasted_iota(jnp.int32, (LANES, lk), 1)
        sel_state = (q == ((c >> 9) << 6) + (c & (SSM_STATE - 1))).astype(BF16)
        sel_tok = (q == ((c >> 7) << 4) + (c & (SSM_GROUP - 1))).astype(BF16)
        row_g = (lax.broadcasted_iota(jnp.int32, (lk, lk), 0) >> 4) & (GROUPS_PER_TILE - 1)
        col = lax.broadcasted_iota(jnp.int32, (lk, lk), 1)
        same_state = row_g == ((col >> 6) & (GROUPS_PER_TILE - 1))
        same_tok = row_g == ((col >> 4) & (GROUPS_PER_TILE - 1))

        def spread(tab_ref, sel, same):
            full = jnp.dot(tab_ref[...].astype(BF16), sel, preferred_element_type=F32)
            return jnp.where(same, full, 0.0).astype(BF16)

        bc_scr[...] = spread(wi_ref, sel_state, same_state)
        wct_scr[...] = spread(wo_ref, sel_state, same_state)
        a_scr[...] = spread(wk_ref, sel_tok, same_tok)

    x = jnp.concatenate(
        [jnp.concatenate([u_ref[b, pl.ds(s, n_chunks, stride=lc), :] for s in range(lc)], axis=1)
         for b in range(bsz)], axis=0).astype(BF16)
    r = lax.broadcasted_iota(jnp.int32, (n_rows, n_rows), 0)
    cc = lax.broadcasted_iota(jnp.int32, (n_rows, n_rows), 1)
    sh_b, sh_c = bsz.bit_length() - 1, n_chunks.bit_length() - 1
    to_cb = (cc == ((r & (bsz - 1)) << sh_c) + (r >> sh_b)).astype(BF16)
    to_bc = (cc == ((r & (n_chunks - 1)) << sh_b) + (r >> sh_c)).astype(BF16)
    x_cb = jnp.dot(to_cb, x, preferred_element_type=F32).astype(BF16)
    s_scr[...] = jnp.dot(x_cb, bc_scr[...], preferred_element_type=F32)
    alr, ali = alr_ref[...], ali_ref[...]

    def step(c, carry):
        hr, hi = carry
        ce = jnp.where(ph == 0, c, n_chunks - 1 - c)
        rows = pl.ds(pl.multiple_of(ce * bsz, bsz), bsz)
        h_scr[rows, 0:STATE_W] = hr
        h_scr[rows, STATE_W:2 * STATE_W] = hi
        sr = s_scr[rows, 0:STATE_W]
        si = s_scr[rows, STATE_W:2 * STATE_W]
        return alr * hr - ali * hi + sr, alr * hi + ali * hr + si

    hr, hi = lax.fori_loop(0, n_chunks, step, (st_scr[0], st_scr[1]))
    st_scr[0] = hr
    st_scr[1] = hi
    h_bc = jnp.dot(to_bc, h_scr[...].astype(BF16), preferred_element_type=F32).astype(BF16)
    y = (jnp.dot(x, a_scr[...], preferred_element_type=F32)
         + lax.dot_general(h_bc, wct_scr[...], (((1,), (1,)), ((), ())), preferred_element_type=F32))
    for b in range(bsz):
        for s in range(lc):
            y_ref[b, pl.ds(s, n_chunks, stride=lc), :] = y[b * n_chunks:(b + 1) * n_chunks,
                                                           s * LANES:(s + 1) * LANES]


def _ssm(l, u, tables, n_ctx):
    bsz, t_all, _ = u.shape
    lc = SCAN_CHUNK
    tt = SCAN_TIME_TILE
    n_t = t_all // tt
    n_ctx_t = n_ctx // tt
    rows = bsz * tt // lc
    inject, readout, intra, alr, ali = tables
    lk = lc * LANES
    assert 2 * STATE_W == lk and SSM_STATE == 64 and SSM_GROUP == 16 and LANES == 128
    assert bsz & (bsz - 1) == 0 and (tt // lc) & (tt // lc - 1) == 0

    def tile_of(ph, i):
        rev = jnp.where(i < n_ctx_t, n_ctx_t - 1 - i, n_t - 1 - (i - n_ctx_t))
        return jnp.where(ph == 0, i, rev)

    op_spec = lambda r, c: pl.BlockSpec((None, None, None, r, c), lambda j, ph, i: (l, ph, j, 0, 0))
    return pl.pallas_call(
        _ssm_kernel,
        grid=(LANE_TILES, 2, n_t),
        in_specs=[
            pl.BlockSpec((bsz, tt, LANES), lambda j, ph, i: (0, tile_of(ph, i), j)),
            op_spec(lk, LANES), op_spec(lk, LANES), op_spec(lk, LANES),
            op_spec(1, STATE_W), op_spec(1, STATE_W),
        ],
        out_specs=pl.BlockSpec((None, bsz, tt, LANES), lambda j, ph, i: (ph, 0, tile_of(ph, i), j)),
        out_shape=jax.ShapeDtypeStruct((2, bsz, t_all, SSM_W), F32),
        scratch_shapes=[
            pltpu.VMEM((lk, 2 * STATE_W), BF16),
            pltpu.VMEM((lk, 2 * STATE_W), BF16),
            pltpu.VMEM((lk, lk), BF16),
            pltpu.VMEM((rows, 2 * STATE_W), F32),
            pltpu.VMEM((rows, 2 * STATE_W), F32),
            pltpu.VMEM((2, bsz, STATE_W), F32),
        ],
        compiler_params=pltpu.CompilerParams(
            dimension_semantics=("parallel", "arbitrary", "arbitrary"), vmem_limit_bytes=VMEM_LIMIT),
        name="s5_scan",
    )(u, inject, readout, intra, alr, ali)


def _merge_kernel(attn_ref, y_ref, ga_ref, gs_ref, x_ref, g1_ref, sh2_ref, sc2_ref,
                  wpa_ref, wglu_ref, bglu_ref, wps_ref, wo_ref, lng_ref, lnb_ref, wr_ref, br_ref,
                  x1_ref, h2_ref, lg_ref, *, alpha):
    a = jnp.dot(attn_ref[...], wpa_ref[...], preferred_element_type=F32)
    ys = y_ref[0] + y_ref[1]
    gl = ys * (0.5 * (1.0 + jnp.tanh(math.sqrt(2.0 / math.pi) * (ys + 0.044715 * (ys * ys * ys)))))
    z = jnp.dot(gl.astype(BF16), wglu_ref[...], preferred_element_type=F32) + bglu_ref[...]
    sg = gl * _sigmoid(z)
    s = jnp.dot(sg.astype(BF16), wps_ref[...], preferred_element_type=F32)
    m = ga_ref[...].astype(F32) * a + gs_ref[...].astype(F32) * s
    y = jnp.dot(m.astype(BF16), wo_ref[...], preferred_element_type=F32)
    x1 = _layer_norm(alpha * x_ref[...] + g1_ref[...] * y, lng_ref[...], lnb_ref[...])
    x1_ref[...] = x1
    h2 = x1 * (1.0 + sc2_ref[...]) + sh2_ref[...]
    h2_ref[...] = _pack_rows(h2)
    h_hi = h2.astype(BF16)
    h_lo = (h2 - h_hi.astype(F32)).astype(BF16)
    w_hi, w_lo = wr_ref[0], wr_ref[1]
    lg_ref[...] = (jnp.dot(h_hi, w_hi, preferred_element_type=F32)
                   + (jnp.dot(h_lo, w_hi, preferred_element_type=F32)
                      + jnp.dot(h_hi, w_lo, preferred_element_type=F32))) + br_ref[...]


def _merge(l, attn, y, siga, sigs, xall, mods4, wts, n_ctx, tile0, alpha):
    bsz, t_all, d = xall.shape
    tm = ROW_TILE
    n_ctx_tiles = n_ctx // tm
    n_tiles = t_all // tm - tile0
    t_out = n_tiles * tm
    tok = lambda w: pl.BlockSpec((None, tm, w), lambda b, i: (b, i + tile0, 0))
    own = lambda w: pl.BlockSpec((None, tm, w), lambda b, i: (b, i, 0))
    modv = lambda col: _mod_spec(l, col, bsz, n_ctx_tiles, tile0)
    wpa, wglu, bglu, wps, wo, lng, lnb, wr, br = wts
    kern = functools.partial(_merge_kernel, alpha=alpha)
    return pl.pallas_call(
        kern,
        grid=(bsz, n_tiles),
        in_specs=[
            own(ATTN_W),
            pl.BlockSpec((2, None, tm, SSM_W), lambda b, i: (0, b, i + tile0, 0)),
            tok(D_MODEL), tok(D_MODEL), tok(d),
            modv(2), modv(3), modv(4),
            _layer_mat(wpa, l), _layer_mat(wglu, l), _layer_vec(bglu, l), _layer_mat(wps, l), _layer_mat(wo, l),
            _layer_vec(lng, l), _layer_vec(lnb, l),
            pl.BlockSpec((None,) + wr.shape[1:], lambda b, i: (l, 0, 0, 0)), _layer_vec(br, l),
        ],
        out_specs=[
            own(d),
            pl.BlockSpec((tm, d // 2), lambda b, i: (b * n_tiles + i, 0)),
            pl.BlockSpec((tm, LANES), lambda b, i: (b * n_tiles + i, 0)),
        ],
        out_shape=[
            jax.ShapeDtypeStruct((bsz, t_out, d), F32),
            jax.ShapeDtypeStruct((bsz * t_out, d // 2), jnp.uint32),
            jax.ShapeDtypeStruct((bsz * t_out, LANES), F32),
        ],
        compiler_params=pltpu.CompilerParams(
            dimension_semantics=("parallel", "arbitrary"), vmem_limit_bytes=VMEM_LIMIT),
        name="merge_ln1",
    )(attn, y, siga, sigs, xall, mods4, mods4, mods4, wpa, wglu, bglu, wps, wo, lng, lnb, wr, br)


def _route_kernel(lg_ref, rec_ref, blk_ref, cnt_scr, start_scr, carry_scr):
    ph = pl.program_id(0)
    i = pl.program_id(1)
    lg = lg_ref[...]
    shape = lg.shape
    lane = lax.broadcasted_iota(jnp.int32, shape, 1).astype(F32)
    far = jnp.full(shape, 1.0e9, F32)

    def first_max(vals, mask):
        vm = jnp.where(mask, vals, NEG_BIG)
        mx = jnp.max(vm, axis=1, keepdims=True)
        idx = jnp.min(jnp.where(mask & (vm == mx), lane, far), axis=1, keepdims=True)
        return mx, idx

    gmask = lane < N_GROUPS
    gmax, gidx = first_max(lg, gmask)
    gtop = 1.0 / jnp.sum(jnp.where(gmask, jnp.exp(lg - gmax), 0.0), axis=1, keepdims=True)
    lo = N_GROUPS + EXPERTS_PER_GROUP * gidx
    emask = (lane >= lo) & (lane < lo + EXPERTS_PER_GROUP)
    v1, i1 = first_max(lg, emask)
    v2, i2 = first_max(lg, emask & (lane != i1))
    e2 = jnp.exp(v2 - v1)
    den = 1.0 + e2
    w1 = (1.0 / den) * gtop
    w2 = (e2 / den) * gtop
    oh1 = (lane == i1 - N_GROUPS).astype(F32)
    oh2 = (lane == i2 - N_GROUPS).astype(F32)
    c1 = jnp.sum(oh1, axis=0, keepdims=True)
    c2 = jnp.sum(oh2, axis=0, keepdims=True)

    @pl.when((ph == 0) & (i == 0))
    def _():
        cnt_scr[...] = jnp.zeros_like(cnt_scr)

    @pl.when(ph == 0)
    def _():
        cnt_scr[...] += c1 + c2

    @pl.when((ph == 1) & (i == 0))
    def _():
        cnt = cnt_scr[...]
        padded = jnp.floor((cnt + (MOE_BLOCK - 1)) * (1.0 / MOE_BLOCK)) * MOE_BLOCK
        r = lax.broadcasted_iota(jnp.int32, (LANES, LANES), 0)
        c = lax.broadcasted_iota(jnp.int32, (LANES, LANES), 1)
        upper = (r < c).astype(F32)
        start = jnp.dot(jnp.broadcast_to(padded, (SUBLANES, LANES)), upper, preferred_element_type=F32,
                        precision=lax.Precision.HIGHEST)[0:1]
        start_scr[...] = start
        carry_scr[...] = jnp.zeros_like(carry_scr)
        nb = blk_ref.shape[0]
        blk_start = (lax.broadcasted_iota(jnp.int32, (nb, LANES), 0) * MOE_BLOCK).astype(F32)
        elane = lax.broadcasted_iota(jnp.int32, (nb, LANES), 1) < N_EXPERTS
        done = jnp.sum(jnp.where(elane & ((start + padded) <= blk_start), 1.0, 0.0), axis=1, keepdims=True)
        blk_ref[...] = jnp.broadcast_to(jnp.minimum(done, N_EXPERTS - 1.0), (nb, LANES)).astype(jnp.int32)

    @pl.when(ph == 1)
    def _():
        tr = shape[0]
        r = lax.broadcasted_iota(jnp.int32, (tr, tr), 0)
        c = lax.broadcasted_iota(jnp.int32, (tr, tr), 1)
        tri = (c < r).astype(BF16)
        base = start_scr[...] + carry_scr[...]
        r1 = jnp.dot(tri, oh1.astype(BF16), preferred_element_type=F32)
        r2 = jnp.dot(tri, oh2.astype(BF16), preferred_element_type=F32) + c1
        d1 = jnp.sum(oh1 * (base + r1), axis=1, keepdims=True)
        d2 = jnp.sum(oh2 * (base + r2), axis=1, keepdims=True)
        carry_scr[...] += c1 + c2
        rec_ref[...] = jnp.where(lane == 0.0, d1, jnp.where(lane == 1.0, d2, jnp.where(lane == 2.0, w1, w2)))


def _route(logits, n_blocks):
    n_tok = logits.shape[0]
    tr = ROUTE_TILE
    nb_pad = -(-n_blocks // SUBLANES) * SUBLANES
    return pl.pallas_call(
        _route_kernel,
        grid=(2, n_tok // tr),
        in_specs=[pl.BlockSpec((tr, LANES), lambda ph, i: (i, 0))],
        out_specs=[
            pl.BlockSpec((tr, LANES), lambda ph, i: (i * ph, 0)),
            pl.BlockSpec((nb_pad, LANES), lambda ph, i: (0, 0)),
        ],
        out_shape=[
            jax.ShapeDtypeStruct((n_tok, LANES), F32),
            jax.ShapeDtypeStruct((nb_pad, LANES), jnp.int32),
        ],
        scratch_shapes=[pltpu.VMEM((1, LANES), F32)] * 3,
        compiler_params=pltpu.CompilerParams(dimension_semantics=("arbitrary", "arbitrary")),
        name="moe_route",
    )(logits)


def _row_copy(src, dst, sem):
    return pltpu.make_async_copy(src, dst, sem)


def _dispatch_kernel(slot_ref, h_ref, xs_in_ref, xs_ref, sem):
    del xs_in_ref
    tg = h_ref.shape[0]

    def issue(r, carry):
        for k in range(2):
            d = slot_ref[0, 0, k * tg + r]
            _row_copy(h_ref.at[pl.ds(r, 1)], xs_ref.at[pl.ds(d, 1)], sem).start(priority=k)
        return carry

    lax.fori_loop(0, tg, issue, 0, unroll=ISSUE_UNROLL)

    def drain(r, carry):
        for k in range(2):
            _row_copy(h_ref.at[pl.ds(0, 1)], xs_ref.at[pl.ds(0, 1)], sem).wait()
        return carry

    lax.fori_loop(0, tg, drain, 0, unroll=ISSUE_UNROLL)


def _dispatch(h2, slots3, cap):
    n_tok, d = h2.shape
    tg = ROW_TILE
    xs0 = jnp.zeros((cap, d), h2.dtype)
    return pl.pallas_call(
        _dispatch_kernel,
        grid=(n_tok // tg,),
        in_specs=[
            pl.BlockSpec((1, 1, 2 * tg), lambda i: (i, 0, 0), memory_space=pltpu.SMEM),
            pl.BlockSpec((tg, d), lambda i: (i, 0)),
            pl.BlockSpec(memory_space=pl.ANY),
        ],
        out_specs=pl.BlockSpec(memory_space=pl.ANY),
        out_shape=jax.ShapeDtypeStruct((cap, d), h2.dtype),
        scratch_shapes=[pltpu.SemaphoreType.DMA(())],
        input_output_aliases={2: 0},
        compiler_params=pltpu.CompilerParams(dimension_semantics=("arbitrary",), has_side_effects=True),
        name="moe_dispatch",
    )(slots3, h2, xs0)


def _expert_kernel(be_ref, xs_ref, w1_ref, w3_ref, w2_ref, ys_ref, w1b, w3b, w2b):
    i = pl.program_id(0)
    fresh = (i == 0) | (be_ref[i] != be_ref[jnp.maximum(i - 1, 0)])

    @pl.when(fresh)
    def _():
        w1b[...] = w1_ref[...].astype(BF16)
        w3b[...] = w3_ref[...].astype(BF16)
        w2b[...] = w2_ref[...].astype(BF16)

    x = _unpack_rows(xs_ref[...]).astype(BF16)
    a = jnp.dot(x, w1b[...], preferred_element_type=F32)
    b = jnp.dot(x, w3b[...], preferred_element_type=F32)
    hid = (a * _sigmoid(a)) * b
    ys_ref[...] = _pack_rows(jnp.dot(hid.astype(BF16), w2b[...], preferred_element_type=F32))


def _experts(l, blk_expert, xs, w1, w3, w2):
    cap, dp = xs.shape
    d, f = w1.shape[-2:]
    n_blocks = cap // MOE_BLOCK
    wspec = lambda r, c: pl.BlockSpec((None, None, r, c), lambda i, be: (l, be[i], 0, 0))
    grid_spec = pltpu.PrefetchScalarGridSpec(
        num_scalar_prefetch=1,
        grid=(n_blocks,),
        in_specs=[pl.BlockSpec((MOE_BLOCK, dp), lambda i, be: (i, 0)), wspec(d, f), wspec(d, f), wspec(f, d)],
        out_specs=pl.BlockSpec((MOE_BLOCK, dp), lambda i, be: (i, 0)),
        scratch_shapes=[pltpu.VMEM((d, f), BF16), pltpu.VMEM((d, f), BF16), pltpu.VMEM((f, d), BF16)],
    )
    return pl.pallas_call(
        _expert_kernel,
        grid_spec=grid_spec,
        out_shape=jax.ShapeDtypeStruct((cap, dp), xs.dtype),
        compiler_params=pltpu.CompilerParams(
            dimension_semantics=("arbitrary",), vmem_limit_bytes=VMEM_LIMIT),
        name="moe_experts",
    )(blk_expert, xs, w1, w3, w2)


def _gather_expert_rows(slot_ref, slot_nxt_ref, ys_ref, buf, sem):
    tm = buf.shape[2]
    g = pl.program_id(0) * pl.num_programs(1) + pl.program_id(1)
    total = pl.num_programs(0) * pl.num_programs(1)
    cur = g & 1

    def issue(ref, slot):
        def body(r, carry):
            for k in range(2):
                d = ref[0, 0, k * tm + r]
                _row_copy(ys_ref.at[pl.ds(d, 1)], buf.at[slot, k, pl.ds(r, 1)], sem.at[slot]).start(priority=k)
            return carry

        lax.fori_loop(0, tm, body, 0, unroll=ISSUE_UNROLL)

    @pl.when(g == 0)
    def _():
        issue(slot_ref, 0)

    @pl.when(g + 1 < total)
    def _():
        issue(slot_nxt_ref, 1 - cur)

    def drain(r, carry):
        for k in range(2):
            _row_copy(ys_ref.at[pl.ds(0, 1)], buf.at[cur, 0, pl.ds(0, 1)], sem.at[cur]).wait()
        return carry

    lax.fori_loop(0, tm, drain, 0, unroll=ISSUE_UNROLL)
    return buf[cur, 0], buf[cur, 1]


def _combine_rows(rec, rows1, rows2, x1, g2, lng, lnb, alpha):
    f = rec[:, 2:3] * _unpack_rows(rows1) + rec[:, 3:4] * _unpack_rows(rows2)
    return _layer_norm(alpha * x1 + g2 * f, lng, lnb)


def _combine_kernel(slot_ref, slot_nxt_ref, rec_ref, x_ref, g2_ref, lng_ref, lnb_ref, ys_ref, o_ref,
                    buf, sem, *, alpha):
    rows1, rows2 = _gather_expert_rows(slot_ref, slot_nxt_ref, ys_ref, buf, sem)
    o_ref[...] = _combine_rows(rec_ref[...], rows1, rows2, x_ref[...], g2_ref[...], lng_ref[...], lnb_ref[...],
                               alpha)


def _combine_inproj_kernel(slot_ref, slot_nxt_ref, rec_ref, x_ref, g2_ref, lng_ref, lnb_ref, ys_ref,
                           sh_ref, sc_ref, cos_ref, sa_ref, sb_ref, w_ref,
                           x2_ref, k_ref, v_ref, u_ref, q_ref, ga_ref, gs_ref, buf, sem, *, alpha):
    rows1, rows2 = _gather_expert_rows(slot_ref, slot_nxt_ref, ys_ref, buf, sem)
    x2 = _combine_rows(rec_ref[...], rows1, rows2, x_ref[...], g2_ref[...], lng_ref[...], lnb_ref[...], alpha)
    x2_ref[...] = x2
    _inproj_body(x2, sh_ref, sc_ref, cos_ref, sa_ref, sb_ref, w_ref, k_ref, v_ref, u_ref, q_ref, ga_ref, gs_ref)


def _combine_specs(l, slots3, rec, x1, mods4, lng, lnb, ys, n_ctx, tile0):
    bsz, t_out, d = x1.shape
    tm = ROW_TILE
    n_tiles = t_out // tm
    total = bsz * n_tiles
    step = lambda b, i: b * n_tiles + i
    in_specs = [
        pl.BlockSpec((1, 1, 2 * tm), lambda b, i: (step(b, i), 0, 0), memory_space=pltpu.SMEM),
        pl.BlockSpec((1, 1, 2 * tm), lambda b, i: (jnp.minimum(step(b, i) + 1, total - 1), 0, 0),
                     memory_space=pltpu.SMEM),
        pl.BlockSpec((tm, LANES), lambda b, i: (step(b, i), 0)),
        pl.BlockSpec((None, tm, d), lambda b, i: (b, i, 0)),
        _mod_spec(l, 5, bsz, n_ctx // tm, tile0),
        _layer_vec(lng, l), _layer_vec(lnb, l),
        pl.BlockSpec(memory_space=pl.ANY),
    ]
    scratch = [pltpu.VMEM((2, 2, tm) + ys.shape[1:], ys.dtype), pltpu.SemaphoreType.DMA((2,))]
    return in_specs, scratch, (slots3, slots3, rec, x1, mods4, lng, lnb, ys)


def _combine(l, slots3, rec, x1, mods4, lng, lnb, ys, n_ctx, tile0, alpha):
    bsz, t_out, d = x1.shape
    tm = ROW_TILE
    in_specs, scratch, args = _combine_specs(l, slots3, rec, x1, mods4, lng, lnb, ys, n_ctx, tile0)
    return pl.pallas_call(
        functools.partial(_combine_kernel, alpha=alpha),
        grid=(bsz, t_out // tm),
        in_specs=in_specs,
        out_specs=pl.BlockSpec((None, tm, d), lambda b, i: (b, i, 0)),
        out_shape=jax.ShapeDtypeStruct((bsz, t_out, d), F32),
        scratch_shapes=scratch,
        compiler_params=pltpu.CompilerParams(dimension_semantics=("arbitrary", "arbitrary")),
        name="moe_combine_ln2",
    )(*args)


def _combine_inproj(l, slots3, rec, x1, mods4, lng, lnb, ys, tabs, w_in_bf, n_ctx, alpha):
    bsz, t_all, d = x1.shape
    tm = ROW_TILE
    in_specs, scratch, args = _combine_specs(l, slots3, rec, x1, mods4, lng, lnb, ys, n_ctx, 0)
    p_specs, p_out_specs, p_out_shape, p_args = _inproj_specs(l + 1, bsz, t_all, mods4, tabs, w_in_bf, n_ctx)
    return pl.pallas_call(
        functools.partial(_combine_inproj_kernel, alpha=alpha),
        grid=(bsz, t_all // tm),
        in_specs=in_specs + p_specs,
        out_specs=[pl.BlockSpec((None, tm, d), lambda b, i: (b, i, 0))] + p_out_specs,
        out_shape=[jax.ShapeDtypeStruct((bsz, t_all, d), F32)] + p_out_shape,
        scratch_shapes=scratch,
        compiler_params=pltpu.CompilerParams(
            dimension_semantics=("arbitrary", "arbitrary"), vmem_limit_bytes=VMEM_LIMIT),
        name="moe_combine_ln2_inproj",
    )(*args, *p_args)


def _rope_tables(n_ctx, n_lat):
    rows = n_lat // GRID_W
    row = jnp.repeat(jnp.arange(rows, dtype=F32), GRID_W)
    col = jnp.tile(jnp.arange(GRID_W, dtype=F32), rows)
    inv = 1.0 / (ROPE_BASE ** (jnp.arange(0, ROPE_AXIS_DIM, 2, dtype=F32) / ROPE_AXIS_DIM))
    half = ROPE_AXIS_DIM // 2
    zeros = jnp.zeros((n_lat, half), F32)
    cos, sa, sb = [], [], []
    for pos in (row, col):
        ang = pos[:, None] * inv
        c, s = jnp.cos(ang), jnp.sin(ang)
        cos += [c, c]
        sa += [-s, zeros]
        sb += [zeros, s]
    reps = LANES // HEAD_DIM

    def full(parts, ctx_fill):
        lat = jnp.tile(jnp.concatenate(parts, axis=1), (1, reps))
        return jnp.concatenate([jnp.full((n_ctx, LANES), ctx_fill, F32), lat], axis=0)

    return full(cos, 1.0), full(sa, 0.0), full(sb, 0.0)


def kernel(x, c, ctx, c_ctx, w_mod, b_mod, w_in, lam_q1, lam_k1, lam_q2, lam_k2, subln_g,
           ssm_a_re, ssm_a_im, ssm_log_dt, ssm_b_re, ssm_b_im, ssm_c_re, ssm_c_im, ssm_d,
           w_glu, b_glu, w_pa, w_ps, w_o, ln1_g, ln1_b,
           router_g_w, router_g_b, router_e_w, router_e_b, moe_w1, moe_w3, moe_w2, ln2_g, ln2_b):
    bsz, n_lat, d = x.shape
    n_ctx = ctx.shape[1]
    depth = w_mod.shape[0]
    assert d == D_MODEL and bsz % SUBLANES == 0
    assert n_ctx % ROW_TILE == 0 and n_lat % ROW_TILE == 0 and n_ctx % SCAN_TIME_TILE == 0
    alpha = (2.0 * depth) ** 0.25
    n_ctx_tiles = n_ctx // ROW_TILE

    mod_rows = -(-(bsz + 1) // SUBLANES) * SUBLANES
    cvec = jnp.concatenate([c, c_ctx[None, :], jnp.zeros((mod_rows - bsz - 1, d), F32)], axis=0)
    mods4 = _modulation(cvec, w_mod, b_mod).reshape(depth, mod_rows, 1, N_MOD * d)
    tabs = _rope_tables(n_ctx, n_lat)
    xall = jnp.concatenate([ctx, x], axis=1)
    vecs = lambda a: a.reshape(depth, 1, a.shape[-1])
    ops = jax.vmap(_ssm_tables)(ssm_a_re, ssm_a_im, ssm_log_dt, ssm_b_re, ssm_b_im, ssm_c_re, ssm_c_im, ssm_d)
    n_pad = LANES - N_GROUPS - N_EXPERTS
    wr = jnp.concatenate([router_g_w, router_e_w, jnp.zeros((depth, d, n_pad), F32)], axis=2)
    br = jnp.concatenate([router_g_b, router_e_b, jnp.zeros((depth, n_pad), F32)], axis=1)
    wr_hi = wr.astype(BF16)
    wr_split = jnp.stack([wr_hi, (wr - wr_hi.astype(F32)).astype(BF16)], axis=1)
    w_in_bf = w_in.astype(BF16)
    wts = (w_pa.astype(BF16), w_glu.astype(BF16), vecs(b_glu), w_ps.astype(BF16), w_o.astype(BF16),
           vecs(ln1_g), vecs(ln1_b), wr_split, vecs(br))
    lam_vecs = (vecs(lam_q1), vecs(lam_k1), vecs(lam_q2), vecs(lam_k2))
    ln2 = (vecs(ln2_g), vecs(ln2_b))
    subln = vecs(subln_g)

    proj = _inproj(0, xall, mods4, tabs, w_in_bf, n_ctx)
    for l in range(depth):
        last = l == depth - 1
        tile0 = n_ctx_tiles if last else 0
        lam_init = 0.8 - 0.6 * math.exp(-0.3 * l)
        k, v, u, q, siga, sigs = proj
        attn = _attention(l, q, k, v, lam_vecs, subln, n_ctx, tile0, lam_init)
        y = _ssm(l, u, ops, n_ctx)
        x1, h2, logits = _merge(l, attn, y, siga, sigs, xall, mods4, wts, n_ctx, tile0, alpha)

        n_tok = h2.shape[0]
        n_blocks = -(-(2 * n_tok + N_EXPERTS * (MOE_BLOCK - 1)) // MOE_BLOCK)
        rec, blk = _route(logits, n_blocks)
        n_tiles_tok = n_tok // ROW_TILE
        slots3 = (rec[:, 0:2].astype(jnp.int32).reshape(n_tiles_tok, ROW_TILE, 2)
                  .transpose(0, 2, 1).reshape(n_tiles_tok, 1, 2 * ROW_TILE))
        xs = _dispatch(h2, slots3, n_blocks * MOE_BLOCK)
        ys = _experts(l, blk[:n_blocks, 0], xs, moe_w1, moe_w3, moe_w2)
        if last:
            xall = _combine(l, slots3, rec, x1, mods4, ln2[0], ln2[1], ys, n_ctx, tile0, alpha)
        else:
            xall, *proj = _combine_inproj(l, slots3, rec, x1, mods4, ln2[0], ln2[1], ys, tabs, w_in_bf, n_ctx,
                                          alpha)
    return xall
```

```python
import functools
import math

import jax
import jax.numpy as jnp
from jax import lax
from jax.experimental import pallas as pl
from jax.experimental.pallas import tpu as pltpu

F32 = jnp.float32
BF16 = jnp.bfloat16

D_MODEL = 1024
N_HEADS = 8
HEAD_DIM = 64
V_DIM = 2 * HEAD_DIM
QK_W = N_HEADS * 2 * HEAD_DIM
ATTN_W = N_HEADS * V_DIM
SSM_W = D_MODEL // 2
SSM_GROUP = 16
SSM_GROUPS = SSM_W // SSM_GROUP
SSM_STATE = 64
N_GROUPS = 4
EXPERTS_PER_GROUP = 8
N_EXPERTS = N_GROUPS * EXPERTS_PER_GROUP
EXPERT_HIDDEN = D_MODEL // 2
KVU_W = QK_W + ATTN_W + SSM_W
IN_W = KVU_W + QK_W + 2 * D_MODEL
N_MOD = 6
GRID_W = 64
ROPE_BASE = 10000.0
ROPE_AXIS_DIM = HEAD_DIM // 2
LN_EPS = 1e-5

LANES = 128
SUBLANES = 8
VMEM_LIMIT = 56 * 1024 * 1024

ROW_TILE = 256
SCAN_CHUNK = 8
SCAN_TIME_TILE = 256
MOE_BLOCK = 256
ROUTE_TILE = 512
HEADS_PER_STEP = 4
ISSUE_UNROLL = 8
LANE_TILES = SSM_W // LANES
GROUPS_PER_TILE = LANES // SSM_GROUP
STATE_W = GROUPS_PER_TILE * SSM_STATE
NEG_BIG = -3.0e38


def _sigmoid(x):
    return 1.0 / (1.0 + jnp.exp(-x))


def _pack_rows(x):
    w = x.shape[1] // 2
    bits = lax.bitcast_convert_type(x.astype(BF16).astype(F32), jnp.uint32)
    return bits[:, :w] | (bits[:, w:] >> 16)


def _unpack_rows(p):
    hi = lax.bitcast_convert_type(p & jnp.uint32(0xFFFF0000), F32)
    lo = lax.bitcast_convert_type(p << 16, F32)
    return jnp.concatenate([hi, lo], axis=1)


def _layer_norm(x, g, b):
    xc = x - jnp.mean(x, axis=-1, keepdims=True)
    var = jnp.mean(xc * xc, axis=-1, keepdims=True)
    return xc * lax.rsqrt(var + LN_EPS) * g + b


def _layer_vec(arr, l):
    return pl.BlockSpec((None, 1, arr.shape[-1]), lambda b, i: (l, 0, 0))


def _layer_mat(arr, l):
    return pl.BlockSpec((None,) + arr.shape[1:], lambda b, i: (l, 0, 0))


def _mod_spec(l, col, bsz, n_ctx_tiles, tile0):
    return pl.BlockSpec((None, None, 1, D_MODEL),
                        lambda b, i: (l, jnp.where(i + tile0 < n_ctx_tiles, bsz, b), 0, col))


def _mod_kernel(c_ref, w_ref, b_ref, o_ref):
    c = c_ref[...]
    s = c * _sigmoid(c)
    o_ref[...] = jnp.dot(s, w_ref[...], preferred_element_type=F32, precision=lax.Precision.HIGHEST) + b_ref[...]


def _modulation(cvec, w_mod, b_mod):
    depth, d, w6 = w_mod.shape
    rows = cvec.shape[0]
    tn = 1024
    return pl.pallas_call(
        _mod_kernel,
        grid=(depth, w6 // tn),
        in_specs=[
            pl.BlockSpec((rows, d), lambda l, j: (0, 0)),
            pl.BlockSpec((None, d, tn), lambda l, j: (l, 0, j)),
            pl.BlockSpec((None, 1, tn), lambda l, j: (l, 0, j)),
        ],
        out_specs=pl.BlockSpec((None, rows, tn), lambda l, j: (l, 0, j)),
        out_shape=jax.ShapeDtypeStruct((depth, rows, w6), F32),
        name="modulation",
    )(cvec, w_mod, b_mod.reshape(depth, 1, w6))


def _inproj_kernel(x_ref, sh_ref, sc_ref, cos_ref, sa_ref, sb_ref, w_ref,
                   k_ref, v_ref, u_ref, q_ref, ga_ref, gs_ref):
    _inproj_body(x_ref[...], sh_ref, sc_ref, cos_ref, sa_ref, sb_ref, w_ref,
                 k_ref, v_ref, u_ref, q_ref, ga_ref, gs_ref)


def _inproj_body(x, sh_ref, sc_ref, cos_ref, sa_ref, sb_ref, w_ref, k_ref, v_ref, u_ref, q_ref, ga_ref, gs_ref):
    h = (x * (1.0 + sc_ref[...]) + sh_ref[...]).astype(BF16)
    cos, sa, sb = cos_ref[...], sa_ref[...], sb_ref[...]

    def rope(t):
        return t * cos + pltpu.roll(t, LANES - 16, 1) * sa + pltpu.roll(t, 16, 1) * sb

    def proj(lo, hi):
        return jnp.dot(h, w_ref[:, lo:hi], preferred_element_type=F32)

    kk = proj(0, QK_W)
    for c in range(QK_W // LANES):
        k_ref[:, c * LANES:(c + 1) * LANES] = rope(kk[:, c * LANES:(c + 1) * LANES]).astype(BF16)
    v_ref[...] = proj(QK_W, QK_W + ATTN_W).astype(BF16)
    u_ref[...] = proj(QK_W + ATTN_W, KVU_W)
    qq = proj(KVU_W, KVU_W + QK_W)
    scale = HEAD_DIM ** -0.5 * math.log2(math.e)
    for c in range(QK_W // LANES):
        q_ref[:, c * LANES:(c + 1) * LANES] = (rope(qq[:, c * LANES:(c + 1) * LANES]) * scale).astype(BF16)
    ga_ref[...] = _sigmoid(proj(KVU_W + QK_W, KVU_W + QK_W + D_MODEL)).astype(BF16)
    gs_ref[...] = _sigmoid(proj(KVU_W + QK_W + D_MODEL, IN_W)).astype(BF16)


def _inproj_specs(l, bsz, t_all, mods4, tabs, w_in_bf, n_ctx):
    tm = ROW_TILE
    n_ctx_tiles = n_ctx // tm
    tok_spec = lambda w: pl.BlockSpec((None, tm, w), lambda b, i: (b, i, 0))
    tab_spec = pl.BlockSpec((tm, LANES), lambda b, i: (i, 0))
    big = lambda w, dt: jax.ShapeDtypeStruct((bsz, t_all, w), dt)
    in_specs = [_mod_spec(l, 0, bsz, n_ctx_tiles, 0), _mod_spec(l, 1, bsz, n_ctx_tiles, 0),
                tab_spec, tab_spec, tab_spec, _layer_mat(w_in_bf, l)]
    out_specs = [tok_spec(QK_W), tok_spec(ATTN_W), tok_spec(SSM_W),
                 tok_spec(QK_W), tok_spec(D_MODEL), tok_spec(D_MODEL)]
    out_shape = [big(QK_W, BF16), big(ATTN_W, BF16), big(SSM_W, F32),
                 big(QK_W, BF16), big(D_MODEL, BF16), big(D_MODEL, BF16)]
    return in_specs, out_specs, out_shape, (mods4, mods4, tabs[0], tabs[1], tabs[2], w_in_bf)


def _inproj(l, xall, mods4, tabs, w_in_bf, n_ctx):
    bsz, t_all, d = xall.shape
    tm = ROW_TILE
    in_specs, out_specs, out_shape, args = _inproj_specs(l, bsz, t_all, mods4, tabs, w_in_bf, n_ctx)
    return pl.pallas_call(
        _inproj_kernel,
        grid=(bsz, t_all // tm),
        in_specs=[pl.BlockSpec((None, tm, d), lambda b, i: (b, i, 0))] + in_specs,
        out_specs=out_specs,
        out_shape=out_shape,
        compiler_params=pltpu.CompilerParams(
            dimension_semantics=("parallel", "arbitrary"), vmem_limit_bytes=VMEM_LIMIT),
        name="inproj",
    )(xall, *args)


def _attn_kernel(q_ref, k_ref, v_ref, lq1_ref, lk1_ref, lq2_ref, lk2_ref, g_ref, o_ref,
                 *, n_ctx, n_ctx_tiles, tile0, lam_init):
    i = pl.program_id(2) + tile0
    lam = (jnp.exp(jnp.sum(lq1_ref[...] * lk1_ref[...], axis=1, keepdims=True))
           - jnp.exp(jnp.sum(lq2_ref[...] * lk2_ref[...], axis=1, keepdims=True)) + lam_init)
    g = g_ref[...]
    lane = lax.broadcasted_iota(jnp.int32, (q_ref.shape[0], V_DIM), 1)

    def scores(hh, n_kv):
        cols = slice(hh * V_DIM, (hh + 1) * V_DIM)
        q = q_ref[:, cols]
        zero = jnp.zeros_like(q)
        k = k_ref[0:n_kv, cols]
        nt = (((1,), (1,)), ((), ()))
        s1 = lax.dot_general(jnp.where(lane < HEAD_DIM, q, zero), k, nt, preferred_element_type=F32)
        s2 = lax.dot_general(jnp.where(lane < HEAD_DIM, zero, q), k, nt, preferred_element_type=F32)
        return s1, s2

    def finish(hh, n_kv, s1, s2):
        cols = slice(hh * V_DIM, (hh + 1) * V_DIM)

        def probs(s):
            p = jnp.exp2(s - jnp.max(s, axis=-1, keepdims=True))
            return p, 1.0 / jnp.sum(p, axis=-1, keepdims=True)

        p1, r1 = probs(s1)
        p2, r2 = probs(s2)
        a = p1 - p2 * (lam * r2 / r1)
        o = jnp.dot(a.astype(BF16), v_ref[0:n_kv, cols], preferred_element_type=F32) * r1
        o = o * lax.rsqrt(jnp.mean(o * o, axis=-1, keepdims=True) + LN_EPS) * g * (1.0 - lam_init)
        o_ref[:, cols] = o.astype(BF16)

    def attend(n_kv):
        pending = None
        for hh in range(HEADS_PER_STEP):
            cur = scores(hh, n_kv)
            if pending is not None:
                finish(hh - 1, n_kv, *pending)
            pending = cur
        finish(HEADS_PER_STEP - 1, n_kv, *pending)

    if n_ctx_tiles > tile0:
        @pl.when(i < n_ctx_tiles)
        def _():
            attend(n_ctx)

        @pl.when(i >= n_ctx_tiles)
        def _():
            attend(k_ref.shape[0])
    else:
        attend(k_ref.shape[0])


def _attention(l, q, k, v, lam_vecs, subln_g, n_ctx, tile0, lam_init):
    bsz, t_all, _ = q.shape
    tq = ROW_TILE
    hw = HEADS_PER_STEP * V_DIM
    n_tiles = t_all // tq - tile0
    vec = lambda arr: pl.BlockSpec((None, 1, arr.shape[-1]), lambda b, h, i: (l, 0, 0))
    kern = functools.partial(_attn_kernel, n_ctx=n_ctx, n_ctx_tiles=n_ctx // tq, tile0=tile0, lam_init=lam_init)
    return pl.pallas_call(
        kern,
        grid=(bsz, N_HEADS // HEADS_PER_STEP, n_tiles),
        in_specs=[
            pl.BlockSpec((None, tq, hw), lambda b, h, i: (b, i + tile0, h)),
            pl.BlockSpec((None, t_all, hw), lambda b, h, i: (b, 0, h)),
            pl.BlockSpec((None, t_all, hw), lambda b, h, i: (b, 0, h)),
            vec(lam_vecs[0]), vec(lam_vecs[1]), vec(lam_vecs[2]), vec(lam_vecs[3]), vec(subln_g),
        ],
        out_specs=pl.BlockSpec((None, tq, hw), lambda b, h, i: (b, i, h)),
        out_shape=jax.ShapeDtypeStruct((bsz, n_tiles * tq, ATTN_W), BF16),
        compiler_params=pltpu.CompilerParams(
            dimension_semantics=("parallel", "parallel", "arbitrary"), vmem_limit_bytes=VMEM_LIMIT),
        name="diff_attention",
    )(q, k, v, *lam_vecs, subln_g)


def _ssm_tables(a_re, a_im, log_dt, b_re, b_im, c_re, c_im, d_skip):
    lc = SCAN_CHUNK
    lam = lax.complex(a_re.astype(F32), a_im.astype(F32))
    dt = jnp.exp(log_dt.astype(F32))[..., None]
    ldt = lam * dt
    a_bar = jnp.exp(ldt)
    b_bar = ((a_bar - 1.0) / lam)[..., None] * lax.complex(b_re.astype(F32), b_im.astype(F32))
    cm = lax.complex(c_re.astype(F32), c_im.astype(F32))
    steps = jnp.arange(lc + 1, dtype=F32)
    apow = jnp.exp(ldt[None] * steps[:, None, None, None])
    s_idx = jnp.arange(lc)
    lag_f = s_idx[None, :] - s_idx[:, None]
    lk = lc * LANES
    eye_c = jnp.eye(SSM_GROUP, dtype=F32)
    eye_t = jnp.eye(lc, dtype=F32)
    d_g = d_skip.astype(F32).reshape(SSM_GROUPS, SSM_GROUP)

    def rows_of(t):
        t = t.reshape(lc, LANE_TILES, GROUPS_PER_TILE, SSM_GROUP, LANES)
        return t.transpose(1, 0, 2, 3, 4).reshape(LANE_TILES, lk, LANES)

    inject, readout, intra = [], [], []
    for di in range(2):
        pw_in = apow[lc - 1 - s_idx, di] if di == 0 else apow[s_idx, di]
        w = jnp.einsum('sgp,gpi->sgip', pw_in, b_bar[di])
        inject.append(rows_of(jnp.concatenate([w.real, w.imag], axis=-1)))
        pw_out = apow[s_idx + 1, di] if di == 0 else apow[lc - s_idx, di]
        vv = cm[di][None] * pw_out[:, :, None, :]
        readout.append(rows_of(jnp.concatenate([vv.real, -vv.imag], axis=-1)))
        kern = jnp.einsum('gcp,jgp,gpi->jgci', cm[di], apow[:lc, di], b_bar[di]).real
        lag = lag_f if di == 0 else -lag_f
        toe = jnp.where((lag >= 0)[:, :, None, None, None], kern[jnp.clip(lag, 0, lc - 1)], 0.0)
        toe = toe.transpose(0, 2, 4, 1, 3)
        if di == 0:
            toe = toe + jnp.einsum('st,gi,ic->sgitc', eye_t, d_g, eye_c)
        intra.append(rows_of(toe.reshape(lc, SSM_GROUPS, SSM_GROUP, LANES)))
    al = apow[lc].reshape(2, LANE_TILES, 1, STATE_W)
    return jnp.stack(inject), jnp.stack(readout), jnp.stack(intra), al.real, al.imag


def _ssm_kernel(u_ref, wi_ref, wo_ref, wk_ref, alr_ref, ali_ref, y_ref,
                bc_scr, wct_scr, a_scr, s_scr, h_scr, st_scr):
    ph = pl.program_id(1)
    ti = pl.program_id(2)
    lc = SCAN_CHUNK
    bsz = u_ref.shape[0]
    n_chunks = u_ref.shape[1] // lc
    n_rows = bsz * n_chunks
    lk = lc * LANES

    @pl.when(ti == 0)
    def _():
        st_scr[...] = jnp.zeros_like(st_scr)
        q = lax.broadcasted_iota(jnp.int32, (LANES, lk), 0)
        c = lax.broadcasted_iota(jnp.int32, (LANES, lk), 1)
        sel_state = (q == ((c >> 9) << 6) + (c & (SSM_STATE - 1))).astype(BF16)
        sel_tok = (q == ((c >> 7) << 4) + (c & (SSM_GROUP - 1))).astype(BF16)
        row_g = (lax.broadcasted_iota(jnp.int32, (lk, lk), 0) >> 4) & (GROUPS_PER_TILE - 1)
        col = lax.broadcasted_iota(jnp.int32, (lk, lk), 1)
        same_state = row_g == ((col >> 6) & (GROUPS_PER_TILE - 1))
        same_tok = row_g == ((col >> 4) & (GROUPS_PER_TILE - 1))

        def spread(tab_ref, sel, same):
            full = jnp.dot(tab_ref[...].astype(BF16), sel, preferred_element_type=F32)
            return jnp.where(same, full, 0.0).astype(BF16)

        bc_scr[...] = spread(wi_ref, sel_state, same_state)
        wct_scr[...] = spread(wo_ref, sel_state, same_state)
        a_scr[...] = spread(wk_ref, sel_tok, same_tok)

    x = jnp.concatenate(
        [jnp.concatenate([u_ref[b, pl.ds(s, n_chunks, stride=lc), :] for s in range(lc)], axis=1)
         for b in range(bsz)], axis=0).astype(BF16)
    r = lax.broadcasted_iota(jnp.int32, (n_rows, n_rows), 0)
    cc = lax.broadcasted_iota(jnp.int32, (n_rows, n_rows), 1)
    sh_b, sh_c = bsz.bit_length() - 1, n_chunks.bit_length() - 1
    to_cb = (cc == ((r & (bsz - 1)) << sh_c) + (r >> sh_b)).astype(BF16)
    to_bc = (cc == ((r & (n_chunks - 1)) << sh_b) + (r >> sh_c)).astype(BF16)
    x_cb = jnp.dot(to_cb, x, preferred_element_type=F32).astype(BF16)
    s_scr[...] = jnp.dot(x_cb, bc_scr[...], preferred_element_type=F32)
    alr, ali = alr_ref[...], ali_ref[...]

    def step(c, carry):
        hr, hi = carry
        ce = jnp.where(ph == 0, c, n_chunks - 1 - c)
        rows = pl.ds(pl.multiple_of(ce * bsz, bsz), bsz)
        h_scr[rows, 0:STATE_W] = hr
        h_scr[rows, STATE_W:2 * STATE_W] = hi
        sr = s_scr[rows, 0:STATE_W]
        si = s_scr[rows, STATE_W:2 * STATE_W]
        return alr * hr - ali * hi + sr, alr * hi + ali * hr + si

    hr, hi = lax.fori_loop(0, n_chunks, step, (st_scr[0], st_scr[1]))
    st_scr[0] = hr
    st_scr[1] = hi
    h_bc = jnp.dot(to_bc, h_scr[...].astype(BF16), preferred_element_type=F32).astype(BF16)
    y = (jnp.dot(x, a_scr[...], preferred_element_type=F32)
         + lax.dot_general(h_bc, wct_scr[...], (((1,), (1,)), ((), ())), preferred_element_type=F32))
    for b in range(bsz):
        for s in range(lc):
            y_ref[b, pl.ds(s, n_chunks, stride=lc), :] = y[b * n_chunks:(b + 1) * n_chunks,
                                                           s * LANES:(s + 1) * LANES]


def _ssm(l, u, tables, n_ctx):
    bsz, t_all, _ = u.shape
    lc = SCAN_CHUNK
    tt = SCAN_TIME_TILE
    n_t = t_all // tt
    n_ctx_t = n_ctx // tt
    rows = bsz * tt // lc
    inject, readout, intra, alr, ali = tables
    lk = lc * LANES
    assert 2 * STATE_W == lk and SSM_STATE == 64 and SSM_GROUP == 16 and LANES == 128
    assert bsz & (bsz - 1) == 0 and (tt // lc) & (tt // lc - 1) == 0

    def tile_of(ph, i):
        rev = jnp.where(i < n_ctx_t, n_ctx_t - 1 - i, n_t - 1 - (i - n_ctx_t))
        return jnp.where(ph == 0, i, rev)

    op_spec = lambda r, c: pl.BlockSpec((None, None, None, r, c), lambda j, ph, i: (l, ph, j, 0, 0))
    return pl.pallas_call(
        _ssm_kernel,
        grid=(LANE_TILES, 2, n_t),
        in_specs=[
            pl.BlockSpec((bsz, tt, LANES), lambda j, ph, i: (0, tile_of(ph, i), j)),
            op_spec(lk, LANES), op_spec(lk, LANES), op_spec(lk, LANES),
            op_spec(1, STATE_W), op_spec(1, STATE_W),
        ],
        out_specs=pl.BlockSpec((None, bsz, tt, LANES), lambda j, ph, i: (ph, 0, tile_of(ph, i), j)),
        out_shape=jax.ShapeDtypeStruct((2, bsz, t_all, SSM_W), F32),
        scratch_shapes=[
            pltpu.VMEM((lk, 2 * STATE_W), BF16),
            pltpu.VMEM((lk, 2 * STATE_W), BF16),
            pltpu.VMEM((lk, lk), BF16),
            pltpu.VMEM((rows, 2 * STATE_W), F32),
            pltpu.VMEM((rows, 2 * STATE_W), F32),
            pltpu.VMEM((2, bsz, STATE_W), F32),
        ],
        compiler_params=pltpu.CompilerParams(
            dimension_semantics=("parallel", "arbitrary", "arbitrary"), vmem_limit_bytes=VMEM_LIMIT),
        name="s5_scan",
    )(u, inject, readout, intra, alr, ali)


def _merge_kernel(attn_ref, y_ref, ga_ref, gs_ref, x_ref, g1_ref, sh2_ref, sc2_ref,
                  wpa_ref, wglu_ref, bglu_ref, wps_ref, wo_ref, lng_ref, lnb_ref, wr_ref, br_ref,
                  x1_ref, h2_ref, lg_ref, *, alpha):
    a = jnp.dot(attn_ref[...], wpa_ref[...], preferred_element_type=F32)
    ys = y_ref[0] + y_ref[1]
    gl = ys * (0.5 * (1.0 + jnp.tanh(math.sqrt(2.0 / math.pi) * (ys + 0.044715 * (ys * ys * ys)))))
    z = jnp.dot(gl.astype(BF16), wglu_ref[...], preferred_element_type=F32) + bglu_ref[...]
    sg = gl * _sigmoid(z)
    s = jnp.dot(sg.astype(BF16), wps_ref[...], preferred_element_type=F32)
    m = ga_ref[...].astype(F32) * a + gs_ref[...].astype(F32) * s
    y = jnp.dot(m.astype(BF16), wo_ref[...], preferred_element_type=F32)
    x1 = _layer_norm(alpha * x_ref[...] + g1_ref[...] * y, lng_ref[...], lnb_ref[...])
    x1_ref[...] = x1
    h2 = x1 * (1.0 + sc2_ref[...]) + sh2_ref[...]
    h2_ref[:, 0, :] = _pack_rows(h2)
    h_hi = h2.astype(BF16)
    h_lo = (h2 - h_hi.astype(F32)).astype(BF16)
    w_hi, w_lo = wr_ref[0], wr_ref[1]
    lg_ref[...] = (jnp.dot(h_hi, w_hi, preferred_element_type=F32)
                   + (jnp.dot(h_lo, w_hi, preferred_element_type=F32)
                      + jnp.dot(h_hi, w_lo, preferred_element_type=F32))) + br_ref[...]


def _merge(l, attn, y, siga, sigs, xall, mods4, wts, n_ctx, tile0, alpha):
    bsz, t_all, d = xall.shape
    tm = ROW_TILE
    n_ctx_tiles = n_ctx // tm
    n_tiles = t_all // tm - tile0
    t_out = n_tiles * tm
    tok = lambda w: pl.BlockSpec((None, tm, w), lambda b, i: (b, i + tile0, 0))
    own = lambda w: pl.BlockSpec((None, tm, w), lambda b, i: (b, i, 0))
    modv = lambda col: _mod_spec(l, col, bsz, n_ctx_tiles, tile0)
    wpa, wglu, bglu, wps, wo, lng, lnb, wr, br = wts
    kern = functools.partial(_merge_kernel, alpha=alpha)
    return pl.pallas_call(
        kern,
        grid=(bsz, n_tiles),
        in_specs=[
            own(ATTN_W),
            pl.BlockSpec((2, None, tm, SSM_W), lambda b, i: (0, b, i + tile0, 0)),
            tok(D_MODEL), tok(D_MODEL), tok(d),
            modv(2), modv(3), modv(4),
            _layer_mat(wpa, l), _layer_mat(wglu, l), _layer_vec(bglu, l), _layer_mat(wps, l), _layer_mat(wo, l),
            _layer_vec(lng, l), _layer_vec(lnb, l),
            pl.BlockSpec((None,) + wr.shape[1:], lambda b, i: (l, 0, 0, 0)), _layer_vec(br, l),
        ],
        out_specs=[
            own(d),
            pl.BlockSpec((tm, 1, d // 2), lambda b, i: (b * n_tiles + i, 0, 0)),
            pl.BlockSpec((tm, LANES), lambda b, i: (b * n_tiles + i, 0)),
        ],
        out_shape=[
            jax.ShapeDtypeStruct((bsz, t_out, d), F32),
            jax.ShapeDtypeStruct((bsz * t_out, 1, d // 2), jnp.uint32),
            jax.ShapeDtypeStruct((bsz * t_out, LANES), F32),
        ],
        compiler_params=pltpu.CompilerParams(
            dimension_semantics=("parallel", "arbitrary"), vmem_limit_bytes=VMEM_LIMIT),
        name="merge_ln1",
    )(attn, y, siga, sigs, xall, mods4, mods4, mods4, wpa, wglu, bglu, wps, wo, lng, lnb, wr, br)


def _route_kernel(lg_ref, rec_ref, blk_ref, cnt_scr, start_scr, carry_scr):
    ph = pl.program_id(0)
    i = pl.program_id(1)
    lg = lg_ref[...]
    shape = lg.shape
    lane = lax.broadcasted_iota(jnp.int32, shape, 1).astype(F32)
    far = jnp.full(shape, 1.0e9, F32)

    def first_max(vals, mask):
        vm = jnp.where(mask, vals, NEG_BIG)
        mx = jnp.max(vm, axis=1, keepdims=True)
        idx = jnp.min(jnp.where(mask & (vm == mx), lane, far), axis=1, keepdims=True)
        return mx, idx

    gmask = lane < N_GROUPS
    gmax, gidx = first_max(lg, gmask)
    gtop = 1.0 / jnp.sum(jnp.where(gmask, jnp.exp(lg - gmax), 0.0), axis=1, keepdims=True)
    lo = N_GROUPS + EXPERTS_PER_GROUP * gidx
    emask = (lane >= lo) & (lane < lo + EXPERTS_PER_GROUP)
    v1, i1 = first_max(lg, emask)
    v2, i2 = first_max(lg, emask & (lane != i1))
    e2 = jnp.exp(v2 - v1)
    den = 1.0 + e2
    w1 = (1.0 / den) * gtop
    w2 = (e2 / den) * gtop
    oh1 = (lane == i1 - N_GROUPS).astype(F32)
    oh2 = (lane == i2 - N_GROUPS).astype(F32)
    c1 = jnp.sum(oh1, axis=0, keepdims=True)
    c2 = jnp.sum(oh2, axis=0, keepdims=True)

    @pl.when((ph == 0) & (i == 0))
    def _():
        cnt_scr[...] = jnp.zeros_like(cnt_scr)

    @pl.when(ph == 0)
    def _():
        cnt_scr[...] += c1 + c2

    @pl.when((ph == 1) & (i == 0))
    def _():
        cnt = cnt_scr[...]
        padded = jnp.floor((cnt + (MOE_BLOCK - 1)) * (1.0 / MOE_BLOCK)) * MOE_BLOCK
        r = lax.broadcasted_iota(jnp.int32, (LANES, LANES), 0)
        c = lax.broadcasted_iota(jnp.int32, (LANES, LANES), 1)
        upper = (r < c).astype(F32)
        start = jnp.dot(jnp.broadcast_to(padded, (SUBLANES, LANES)), upper, preferred_element_type=F32,
                        precision=lax.Precision.HIGHEST)[0:1]
        start_scr[...] = start
        carry_scr[...] = jnp.zeros_like(carry_scr)
        nb = blk_ref.shape[0]
        blk_start = (lax.broadcasted_iota(jnp.int32, (nb, LANES), 0) * MOE_BLOCK).astype(F32)
        elane = lax.broadcasted_iota(jnp.int32, (nb, LANES), 1) < N_EXPERTS
        done = jnp.sum(jnp.where(elane & ((start + padded) <= blk_start), 1.0, 0.0), axis=1, keepdims=True)
        blk_ref[...] = jnp.broadcast_to(jnp.minimum(done, N_EXPERTS - 1.0), (nb, LANES)).astype(jnp.int32)

    @pl.when(ph == 1)
    def _():
        tr = shape[0]
        r = lax.broadcasted_iota(jnp.int32, (tr, tr), 0)
        c = lax.broadcasted_iota(jnp.int32, (tr, tr), 1)
        tri = (c < r).astype(BF16)
        base = start_scr[...] + carry_scr[...]
        r1 = jnp.dot(tri, oh1.astype(BF16), preferred_element_type=F32)
        r2 = jnp.dot(tri, oh2.astype(BF16), preferred_element_type=F32) + c1
        d1 = jnp.sum(oh1 * (base + r1), axis=1, keepdims=True)
        d2 = jnp.sum(oh2 * (base + r2), axis=1, keepdims=True)
        carry_scr[...] += c1 + c2
        rec_ref[...] = jnp.where(lane == 0.0, d1, jnp.where(lane == 1.0, d2, jnp.where(lane == 2.0, w1, w2)))


def _route(logits, n_blocks):
    n_tok = logits.shape[0]
    tr = ROUTE_TILE
    nb_pad = -(-n_blocks // SUBLANES) * SUBLANES
    return pl.pallas_call(
        _route_kernel,
        grid=(2, n_tok // tr),
        in_specs=[pl.BlockSpec((tr, LANES), lambda ph, i: (i, 0))],
        out_specs=[
            pl.BlockSpec((tr, LANES), lambda ph, i: (i * ph, 0)),
            pl.BlockSpec((nb_pad, LANES), lambda ph, i: (0, 0)),
        ],
        out_shape=[
            jax.ShapeDtypeStruct((n_tok, LANES), F32),
            jax.ShapeDtypeStruct((nb_pad, LANES), jnp.int32),
        ],
        scratch_shapes=[pltpu.VMEM((1, LANES), F32)] * 3,
        compiler_params=pltpu.CompilerParams(dimension_semantics=("arbitrary", "arbitrary")),
        name="moe_route",
    )(logits)


def _row_copy(src, dst, sem):
    return pltpu.make_async_copy(src, dst, sem)


def _dispatch_kernel(slot_ref, h_ref, xs_in_ref, xs_ref, sem):
    del xs_in_ref
    tg = h_ref.shape[0]

    def issue(r, carry):
        for k in range(2):
            d = slot_ref[0, 0, k * tg + r]
            _row_copy(h_ref.at[pl.ds(r, 1)], xs_ref.at[pl.ds(d, 1)], sem).start(priority=k)
        return carry

    lax.fori_loop(0, tg, issue, 0, unroll=ISSUE_UNROLL)

    def drain(r, carry):
        for k in range(2):
            _row_copy(h_ref.at[pl.ds(0, 1)], xs_ref.at[pl.ds(0, 1)], sem).wait()
        return carry

    lax.fori_loop(0, tg, drain, 0, unroll=ISSUE_UNROLL)


def _dispatch(h2, slots3, cap):
    n_tok, _, d = h2.shape
    tg = ROW_TILE
    xs0 = jnp.zeros((cap, 1, d), h2.dtype)
    return pl.pallas_call(
        _dispatch_kernel,
        grid=(n_tok // tg,),
        in_specs=[
            pl.BlockSpec((1, 1, 2 * tg), lambda i: (i, 0, 0), memory_space=pltpu.SMEM),
            pl.BlockSpec((tg, 1, d), lambda i: (i, 0, 0)),
            pl.BlockSpec(memory_space=pl.ANY),
        ],
        out_specs=pl.BlockSpec(memory_space=pl.ANY),
        out_shape=jax.ShapeDtypeStruct((cap, 1, d), h2.dtype),
        scratch_shapes=[pltpu.SemaphoreType.DMA(())],
        input_output_aliases={2: 0},
        compiler_params=pltpu.CompilerParams(dimension_semantics=("arbitrary",), has_side_effects=True),
        name="moe_dispatch",
    )(slots3, h2, xs0)


def _expert_kernel(be_ref, xs_ref, w1_hbm, w3_hbm, w2_hbm, ys_ref,
                   w1f, w3f, w2f, w1b, w3b, w2b, sem, slot_scr, *, layer):
    i = pl.program_id(0)
    n = pl.num_programs(0)
    e = be_ref[i]
    fresh = (i == 0) | (e != be_ref[jnp.maximum(i - 1, 0)])

    def fetch(expert, slot):
        return [_row_copy(w1_hbm.at[layer, expert], w1f.at[slot], sem.at[slot]),
                _row_copy(w3_hbm.at[layer, expert], w3f.at[slot], sem.at[slot]),
                _row_copy(w2_hbm.at[layer, expert], w2f.at[slot], sem.at[slot])]

    @pl.when(i == 0)
    def _():
        slot_scr[0] = 0
        for c in fetch(e, 0):
            c.start()

    @pl.when(fresh)
    def _():
        slot = slot_scr[0]
        for c in fetch(e, slot):
            c.wait()
        nxt = lax.while_loop(lambda j: (j < n) & (be_ref[jnp.minimum(j, n - 1)] == e), lambda j: j + 1, i + 1)

        @pl.when(nxt < n)
        def _():
            for c in fetch(be_ref[jnp.minimum(nxt, n - 1)], 1 - slot):
                c.start()

        w1b[...] = w1f[slot].astype(BF16)
        w3b[...] = w3f[slot].astype(BF16)
        w2b[...] = w2f[slot].astype(BF16)
        slot_scr[0] = 1 - slot

    x = _unpack_rows(xs_ref[:, 0, :]).astype(BF16)
    a = jnp.dot(x, w1b[...], preferred_element_type=F32)
    b = jnp.dot(x, w3b[...], preferred_element_type=F32)
    hid = (a * _sigmoid(a)) * b
    ys_ref[:, 0, :] = _pack_rows(jnp.dot(hid.astype(BF16), w2b[...], preferred_element_type=F32))


def _experts(l, blk_expert, xs, w1, w3, w2):
    cap, _, dp = xs.shape
    d, f = w1.shape[-2:]
    n_blocks = cap // MOE_BLOCK
    rows = pl.BlockSpec((MOE_BLOCK, 1, dp), lambda i, be: (i, 0, 0))
    hbm = pl.BlockSpec(memory_space=pl.ANY)
    grid_spec = pltpu.PrefetchScalarGridSpec(
        num_scalar_prefetch=1,
        grid=(n_blocks,),
        in_specs=[rows, hbm, hbm, hbm],
        out_specs=rows,
        scratch_shapes=[pltpu.VMEM((2, d, f), F32), pltpu.VMEM((2, d, f), F32), pltpu.VMEM((2, f, d), F32),
                        pltpu.VMEM((d, f), BF16), pltpu.VMEM((d, f), BF16), pltpu.VMEM((f, d), BF16),
                        pltpu.SemaphoreType.DMA((2,)), pltpu.SMEM((1,), jnp.int32)],
    )
    return pl.pallas_call(
        functools.partial(_expert_kernel, layer=l),
        grid_spec=grid_spec,
        out_shape=jax.ShapeDtypeStruct((cap, 1, dp), xs.dtype),
        compiler_params=pltpu.CompilerParams(
            dimension_semantics=("arbitrary",), vmem_limit_bytes=VMEM_LIMIT),
        name="moe_experts",
    )(blk_expert, xs, w1, w3, w2)


def _gather_expert_rows(slot_ref, slot_nxt_ref, ys_ref, buf, sem):
    tm = buf.shape[2]
    g = pl.program_id(0) * pl.num_programs(1) + pl.program_id(1)
    total = pl.num_programs(0) * pl.num_programs(1)
    cur = g & 1

    def issue(ref, slot):
        def body(r, carry):
            for k in range(2):
                d = ref[0, 0, k * tm + r]
                _row_copy(ys_ref.at[pl.ds(d, 1)], buf.at[slot, k, pl.ds(r, 1)], sem.at[slot]).start(priority=k)
            return carry

        lax.fori_loop(0, tm, body, 0, unroll=ISSUE_UNROLL)

    @pl.when(g == 0)
    def _():
        issue(slot_ref, 0)

    @pl.when(g + 1 < total)
    def _():
        issue(slot_nxt_ref, 1 - cur)

    def drain(r, carry):
        for k in range(2):
            _row_copy(ys_ref.at[pl.ds(0, 1)], buf.at[cur, 0, pl.ds(0, 1)], sem.at[cur]).wait()
        return carry

    lax.fori_loop(0, tm, drain, 0, unroll=ISSUE_UNROLL)
    return buf[cur, 0, :, 0, :], buf[cur, 1, :, 0, :]


def _combine_rows(rec, rows1, rows2, x1, g2, lng, lnb, alpha):
    f = rec[:, 2:3] * _unpack_rows(rows1) + rec[:, 3:4] * _unpack_rows(rows2)
    return _layer_norm(alpha * x1 + g2 * f, lng, lnb)


def _combine_kernel(slot_ref, slot_nxt_ref, rec_ref, x_ref, g2_ref, lng_ref, lnb_ref, ys_ref, o_ref,
                    buf, sem, *, alpha):
    rows1, rows2 = _gather_expert_rows(slot_ref, slot_nxt_ref, ys_ref, buf, sem)
    o_ref[...] = _combine_rows(rec_ref[...], rows1, rows2, x_ref[...], g2_ref[...], lng_ref[...], lnb_ref[...],
                               alpha)


def _combine_inproj_kernel(slot_ref, slot_nxt_ref, rec_ref, x_ref, g2_ref, lng_ref, lnb_ref, ys_ref,
                           sh_ref, sc_ref, cos_ref, sa_ref, sb_ref, w_ref,
                           x2_ref, k_ref, v_ref, u_ref, q_ref, ga_ref, gs_ref, buf, sem, *, alpha):
    rows1, rows2 = _gather_expert_rows(slot_ref, slot_nxt_ref, ys_ref, buf, sem)
    x2 = _combine_rows(rec_ref[...], rows1, rows2, x_ref[...], g2_ref[...], lng_ref[...], lnb_ref[...], alpha)
    x2_ref[...] = x2
    _inproj_body(x2, sh_ref, sc_ref, cos_ref, sa_ref, sb_ref, w_ref, k_ref, v_ref, u_ref, q_ref, ga_ref, gs_ref)


def _combine_specs(l, slots3, rec, x1, mods4, lng, lnb, ys, n_ctx, tile0):
    bsz, t_out, d = x1.shape
    tm = ROW_TILE
    n_tiles = t_out // tm
    total = bsz * n_tiles
    step = lambda b, i: b * n_tiles + i
    in_specs = [
        pl.BlockSpec((1, 1, 2 * tm), lambda b, i: (step(b, i), 0, 0), memory_space=pltpu.SMEM),
        pl.BlockSpec((1, 1, 2 * tm), lambda b, i: (jnp.minimum(step(b, i) + 1, total - 1), 0, 0),
                     memory_space=pltpu.SMEM),
        pl.BlockSpec((tm, LANES), lambda b, i: (step(b, i), 0)),
        pl.BlockSpec((None, tm, d), lambda b, i: (b, i, 0)),
        _mod_spec(l, 5, bsz, n_ctx // tm, tile0),
        _layer_vec(lng, l), _layer_vec(lnb, l),
        pl.BlockSpec(memory_space=pl.ANY),
    ]
    scratch = [pltpu.VMEM((2, 2, tm) + ys.shape[1:], ys.dtype), pltpu.SemaphoreType.DMA((2,))]
    return in_specs, scratch, (slots3, slots3, rec, x1, mods4, lng, lnb, ys)


def _combine(l, slots3, rec, x1, mods4, lng, lnb, ys, n_ctx, tile0, alpha):
    bsz, t_out, d = x1.shape
    tm = ROW_TILE
    in_specs, scratch, args = _combine_specs(l, slots3, rec, x1, mods4, lng, lnb, ys, n_ctx, tile0)
    return pl.pallas_call(
        functools.partial(_combine_kernel, alpha=alpha),
        grid=(bsz, t_out // tm),
        in_specs=in_specs,
        out_specs=pl.BlockSpec((None, tm, d), lambda b, i: (b, i, 0)),
        out_shape=jax.ShapeDtypeStruct((bsz, t_out, d), F32),
        scratch_shapes=scratch,
        compiler_params=pltpu.CompilerParams(dimension_semantics=("arbitrary", "arbitrary")),
        name="moe_combine_ln2",
    )(*args)


def _combine_inproj(l, slots3, rec, x1, mods4, lng, lnb, ys, tabs, w_in_bf, n_ctx, alpha):
    bsz, t_all, d = x1.shape
    tm = ROW_TILE
    in_specs, scratch, args = _combine_specs(l, slots3, rec, x1, mods4, lng, lnb, ys, n_ctx, 0)
    p_specs, p_out_specs, p_out_shape, p_args = _inproj_specs(l + 1, bsz, t_all, mods4, tabs, w_in_bf, n_ctx)
    return pl.pallas_call(
        functools.partial(_combine_inproj_kernel, alpha=alpha),
        grid=(bsz, t_all // tm),
        in_specs=in_specs + p_specs,
        out_specs=[pl.BlockSpec((None, tm, d), lambda b, i: (b, i, 0))] + p_out_specs,
        out_shape=[jax.ShapeDtypeStruct((bsz, t_all, d), F32)] + p_out_shape,
        scratch_shapes=scratch,
        compiler_params=pltpu.CompilerParams(
            dimension_semantics=("arbitrary", "arbitrary"), vmem_limit_bytes=VMEM_LIMIT),
        name="moe_combine_ln2_inproj",
    )(*args, *p_args)


def _rope_tables(n_ctx, n_lat):
    rows = n_lat // GRID_W
    row = jnp.repeat(jnp.arange(rows, dtype=F32), GRID_W)
    col = jnp.tile(jnp.arange(GRID_W, dtype=F32), rows)
    inv = 1.0 / (ROPE_BASE ** (jnp.arange(0, ROPE_AXIS_DIM, 2, dtype=F32) / ROPE_AXIS_DIM))
    half = ROPE_AXIS_DIM // 2
    zeros = jnp.zeros((n_lat, half), F32)
    cos, sa, sb = [], [], []
    for pos in (row, col):
        ang = pos[:, None] * inv
        c, s = jnp.cos(ang), jnp.sin(ang)
        cos += [c, c]
        sa += [-s, zeros]
        sb += [zeros, s]
    reps = LANES // HEAD_DIM

    def full(parts, ctx_fill):
        lat = jnp.tile(jnp.concatenate(parts, axis=1), (1, reps))
        return jnp.concatenate([jnp.full((n_ctx, LANES), ctx_fill, F32), lat], axis=0)

    return full(cos, 1.0), full(sa, 0.0), full(sb, 0.0)


def kernel(x, c, ctx, c_ctx, w_mod, b_mod, w_in, lam_q1, lam_k1, lam_q2, lam_k2, subln_g,
           ssm_a_re, ssm_a_im, ssm_log_dt, ssm_b_re, ssm_b_im, ssm_c_re, ssm_c_im, ssm_d,
           w_glu, b_glu, w_pa, w_ps, w_o, ln1_g, ln1_b,
           router_g_w, router_g_b, router_e_w, router_e_b, moe_w1, moe_w3, moe_w2, ln2_g, ln2_b):
    bsz, n_lat, d = x.shape
    n_ctx = ctx.shape[1]
    depth = w_mod.shape[0]
    assert d == D_MODEL and bsz % SUBLANES == 0
    assert n_ctx % ROW_TILE == 0 and n_lat % ROW_TILE == 0 and n_ctx % SCAN_TIME_TILE == 0
    alpha = (2.0 * depth) ** 0.25
    n_ctx_tiles = n_ctx // ROW_TILE

    mod_rows = -(-(bsz + 1) // SUBLANES) * SUBLANES
    cvec = jnp.concatenate([c, c_ctx[None, :], jnp.zeros((mod_rows - bsz - 1, d), F32)], axis=0)
    mods4 = _modulation(cvec, w_mod, b_mod).reshape(depth, mod_rows, 1, N_MOD * d)
    tabs = _rope_tables(n_ctx, n_lat)
    xall = jnp.concatenate([ctx, x], axis=1)
    vecs = lambda a: a.reshape(depth, 1, a.shape[-1])
    ops = jax.vmap(_ssm_tables)(ssm_a_re, ssm_a_im, ssm_log_dt, ssm_b_re, ssm_b_im, ssm_c_re, ssm_c_im, ssm_d)
    n_pad = LANES - N_GROUPS - N_EXPERTS
    wr = jnp.concatenate([router_g_w, router_e_w, jnp.zeros((depth, d, n_pad), F32)], axis=2)
    br = jnp.concatenate([router_g_b, router_e_b, jnp.zeros((depth, n_pad), F32)], axis=1)
    wr_hi = wr.astype(BF16)
    wr_split = jnp.stack([wr_hi, (wr - wr_hi.astype(F32)).astype(BF16)], axis=1)
    w_in_bf = w_in.astype(BF16)
    wts = (w_pa.astype(BF16), w_glu.astype(BF16), vecs(b_glu), w_ps.astype(BF16), w_o.astype(BF16),
           vecs(ln1_g), vecs(ln1_b), wr_split, vecs(br))
    lam_vecs = (vecs(lam_q1), vecs(lam_k1), vecs(lam_q2), vecs(lam_k2))
    ln2 = (vecs(ln2_g), vecs(ln2_b))
    subln = vecs(subln_g)

    proj = _inproj(0, xall, mods4, tabs, w_in_bf, n_ctx)
    for l in range(depth):
        last = l == depth - 1
        tile0 = n_ctx_tiles if last else 0
        lam_init = 0.8 - 0.6 * math.exp(-0.3 * l)
        k, v, u, q, siga, sigs = proj
        attn = _attention(l, q, k, v, lam_vecs, subln, n_ctx, tile0, lam_init)
        y = _ssm(l, u, ops, n_ctx)
        x1, h2, logits = _merge(l, attn, y, siga, sigs, xall, mods4, wts, n_ctx, tile0, alpha)

        n_tok = h2.shape[0]
        n_blocks = -(-(2 * n_tok + N_EXPERTS * (MOE_BLOCK - 1)) // MOE_BLOCK)
        rec, blk = _route(logits, n_blocks)
        n_tiles_tok = n_tok // ROW_TILE
        slots3 = (rec[:, 0:2].astype(jnp.int32).reshape(n_tiles_tok, ROW_TILE, 2)
                  .transpose(0, 2, 1).reshape(n_tiles_tok, 1, 2 * ROW_TILE))
        xs = _dispatch(h2, slots3, n_blocks * MOE_BLOCK)
        ys = _experts(l, blk[:n_blocks, 0], xs, moe_w1, moe_w3, moe_w2)
        if last:
            xall = _combine(l, slots3, rec, x1, mods4, ln2[0], ln2[1], ys, n_ctx, tile0, alpha)
        else:
            xall, *proj = _combine_inproj(l, slots3, rec, x1, mods4, ln2[0], ln2[1], ys, tabs, w_in_bf, n_ctx,
                                          alpha)
    return xall
```

```python
import functools
import math

import jax
import jax.numpy as jnp
from jax import lax
from jax.experimental import pallas as pl
from jax.experimental.pallas import tpu as pltpu

F32 = jnp.float32
BF16 = jnp.bfloat16

D_MODEL = 1024
N_HEADS = 8
HEAD_DIM = 64
V_DIM = 2 * HEAD_DIM
QK_W = N_HEADS * 2 * HEAD_DIM
ATTN_W = N_HEADS * V_DIM
SSM_W = D_MODEL // 2
SSM_GROUP = 16
SSM_GROUPS = SSM_W // SSM_GROUP
SSM_STATE = 64
N_GROUPS = 4
EXPERTS_PER_GROUP = 8
N_EXPERTS = N_GROUPS * EXPERTS_PER_GROUP
EXPERT_HIDDEN = D_MODEL // 2
KVU_W = QK_W + ATTN_W + SSM_W
IN_W = KVU_W + QK_W + 2 * D_MODEL
N_MOD = 6
GRID_W = 64
ROPE_BASE = 10000.0
ROPE_AXIS_DIM = HEAD_DIM // 2
LN_EPS = 1e-5

LANES = 128
SUBLANES = 8
VMEM_LIMIT = 56 * 1024 * 1024

ROW_TILE = 256
SCAN_CHUNK = 8
SCAN_TIME_TILE = 256
MOE_BLOCK = 256
ROUTE_TILE = 512
HEADS_PER_STEP = 4
SCORE_LOOKAHEAD = 1
ISSUE_UNROLL = 8
LANE_TILES = SSM_W // LANES
GROUPS_PER_TILE = LANES // SSM_GROUP
STATE_W = GROUPS_PER_TILE * SSM_STATE
NEG_BIG = -3.0e38


def _sigmoid(x):
    return 1.0 / (1.0 + jnp.exp(-x))


def _pack_rows(x):
    w = x.shape[1] // 2
    bits = lax.bitcast_convert_type(x.astype(BF16).astype(F32), jnp.uint32)
    return bits[:, :w] | (bits[:, w:] >> 16)


def _unpack_rows(p):
    hi = lax.bitcast_convert_type(p & jnp.uint32(0xFFFF0000), F32)
    lo = lax.bitcast_convert_type(p << 16, F32)
    return jnp.concatenate([hi, lo], axis=1)


def _layer_norm(x, g, b):
    xc = x - jnp.mean(x, axis=-1, keepdims=True)
    var = jnp.mean(xc * xc, axis=-1, keepdims=True)
    return xc * lax.rsqrt(var + LN_EPS) * g + b


def _layer_vec(arr, l):
    return pl.BlockSpec((None, 1, arr.shape[-1]), lambda b, i: (l, 0, 0))


def _layer_mat(arr, l):
    return pl.BlockSpec((None,) + arr.shape[1:], lambda b, i: (l, 0, 0))


def _mod_spec(l, col, bsz, n_ctx_tiles, tile0):
    return pl.BlockSpec((None, None, 1, D_MODEL),
                        lambda b, i: (l, jnp.where(i + tile0 < n_ctx_tiles, bsz, b), 0, col))


def _mod_kernel(c_ref, w_ref, b_ref, o_ref):
    c = c_ref[...]
    s = c * _sigmoid(c)
    o_ref[...] = jnp.dot(s, w_ref[...], preferred_element_type=F32, precision=lax.Precision.HIGHEST) + b_ref[...]


def _modulation(cvec, w_mod, b_mod):
    depth, d, w6 = w_mod.shape
    rows = cvec.shape[0]
    tn = 1024
    return pl.pallas_call(
        _mod_kernel,
        grid=(depth, w6 // tn),
        in_specs=[
            pl.BlockSpec((rows, d), lambda l, j: (0, 0)),
            pl.BlockSpec((None, d, tn), lambda l, j: (l, 0, j)),
            pl.BlockSpec((None, 1, tn), lambda l, j: (l, 0, j)),
        ],
        out_specs=pl.BlockSpec((None, rows, tn), lambda l, j: (l, 0, j)),
        out_shape=jax.ShapeDtypeStruct((depth, rows, w6), F32),
        name="modulation",
    )(cvec, w_mod, b_mod.reshape(depth, 1, w6))


def _inproj_kernel(x_ref, sh_ref, sc_ref, cos_ref, sa_ref, sb_ref, w_ref,
                   k_ref, v_ref, u_ref, q_ref, ga_ref, gs_ref):
    _inproj_body(x_ref[...], sh_ref, sc_ref, cos_ref, sa_ref, sb_ref, w_ref,
                 k_ref, v_ref, u_ref, q_ref, ga_ref, gs_ref)


def _inproj_body(x, sh_ref, sc_ref, cos_ref, sa_ref, sb_ref, w_ref, k_ref, v_ref, u_ref, q_ref, ga_ref, gs_ref):
    h = (x * (1.0 + sc_ref[...]) + sh_ref[...]).astype(BF16)
    cos, sa, sb = cos_ref[...], sa_ref[...], sb_ref[...]

    def rope(t):
        return t * cos + pltpu.roll(t, LANES - 16, 1) * sa + pltpu.roll(t, 16, 1) * sb

    def proj(lo, hi):
        return jnp.dot(h, w_ref[:, lo:hi], preferred_element_type=F32)

    kk = proj(0, QK_W)
    for c in range(QK_W // LANES):
        k_ref[:, c * LANES:(c + 1) * LANES] = rope(kk[:, c * LANES:(c + 1) * LANES]).astype(BF16)
    v_ref[...] = proj(QK_W, QK_W + ATTN_W).astype(BF16)
    u_ref[...] = proj(QK_W + ATTN_W, KVU_W)
    qq = proj(KVU_W, KVU_W + QK_W)
    scale = HEAD_DIM ** -0.5 * math.log2(math.e)
    for c in range(QK_W // LANES):
        q_ref[:, c * LANES:(c + 1) * LANES] = (rope(qq[:, c * LANES:(c + 1) * LANES]) * scale).astype(BF16)
    ga_ref[...] = _sigmoid(proj(KVU_W + QK_W, KVU_W + QK_W + D_MODEL)).astype(BF16)
    gs_ref[...] = _sigmoid(proj(KVU_W + QK_W + D_MODEL, IN_W)).astype(BF16)


def _inproj_specs(l, bsz, t_all, mods4, tabs, w_in_bf, n_ctx):
    tm = ROW_TILE
    n_ctx_tiles = n_ctx // tm
    tok_spec = lambda w: pl.BlockSpec((None, tm, w), lambda b, i: (b, i, 0))
    tab_spec = pl.BlockSpec((tm, LANES), lambda b, i: (i, 0))
    big = lambda w, dt: jax.ShapeDtypeStruct((bsz, t_all, w), dt)
    in_specs = [_mod_spec(l, 0, bsz, n_ctx_tiles, 0), _mod_spec(l, 1, bsz, n_ctx_tiles, 0),
                tab_spec, tab_spec, tab_spec, _layer_mat(w_in_bf, l)]
    out_specs = [tok_spec(QK_W), tok_spec(ATTN_W), tok_spec(SSM_W),
                 tok_spec(QK_W), tok_spec(D_MODEL), tok_spec(D_MODEL)]
    out_shape = [big(QK_W, BF16), big(ATTN_W, BF16), big(SSM_W, F32),
                 big(QK_W, BF16), big(D_MODEL, BF16), big(D_MODEL, BF16)]
    return in_specs, out_specs, out_shape, (mods4, mods4, tabs[0], tabs[1], tabs[2], w_in_bf)


def _inproj(l, xall, mods4, tabs, w_in_bf, n_ctx):
    bsz, t_all, d = xall.shape
    tm = ROW_TILE
    in_specs, out_specs, out_shape, args = _inproj_specs(l, bsz, t_all, mods4, tabs, w_in_bf, n_ctx)
    return pl.pallas_call(
        _inproj_kernel,
        grid=(bsz, t_all // tm),
        in_specs=[pl.BlockSpec((None, tm, d), lambda b, i: (b, i, 0))] + in_specs,
        out_specs=out_specs,
        out_shape=out_shape,
        compiler_params=pltpu.CompilerParams(
            dimension_semantics=("parallel", "arbitrary"), vmem_limit_bytes=VMEM_LIMIT),
        name="inproj",
    )(xall, *args)


def _attn_kernel(q_ref, k_ref, v_ref, lq1_ref, lk1_ref, lq2_ref, lk2_ref, g_ref, o_ref,
                 *, n_ctx, n_ctx_tiles, tile0, lam_init):
    i = pl.program_id(2) + tile0
    lam = (jnp.exp(jnp.sum(lq1_ref[...] * lk1_ref[...], axis=1, keepdims=True))
           - jnp.exp(jnp.sum(lq2_ref[...] * lk2_ref[...], axis=1, keepdims=True)) + lam_init)
    g = g_ref[...]
    lane = lax.broadcasted_iota(jnp.int32, (q_ref.shape[0], V_DIM), 1)

    def scores(hh, n_kv):
        cols = slice(hh * V_DIM, (hh + 1) * V_DIM)
        q = q_ref[:, cols]
        zero = jnp.zeros_like(q)
        k = k_ref[0:n_kv, cols]
        nt = (((1,), (1,)), ((), ()))
        s1 = lax.dot_general(jnp.where(lane < HEAD_DIM, q, zero), k, nt, preferred_element_type=F32)
        s2 = lax.dot_general(jnp.where(lane < HEAD_DIM, zero, q), k, nt, preferred_element_type=F32)
        return s1, s2

    def finish(hh, n_kv, s1, s2):
        cols = slice(hh * V_DIM, (hh + 1) * V_DIM)

        def probs(s):
            p = jnp.exp2(s - jnp.max(s, axis=-1, keepdims=True))
            return p, 1.0 / jnp.sum(p, axis=-1, keepdims=True)

        p1, r1 = probs(s1)
        p2, r2 = probs(s2)
        a = p1 - p2 * (lam * r2 / r1)
        o = jnp.dot(a.astype(BF16), v_ref[0:n_kv, cols], preferred_element_type=F32) * r1
        o = o * lax.rsqrt(jnp.mean(o * o, axis=-1, keepdims=True) + LN_EPS) * g * (1.0 - lam_init)
        o_ref[:, cols] = o.astype(BF16)

    def attend(n_kv):
        pending = []
        for hh in range(HEADS_PER_STEP):
            pending.append((hh, scores(hh, n_kv)))
            if len(pending) > SCORE_LOOKAHEAD:
                h0, s0 = pending.pop(0)
                finish(h0, n_kv, *s0)
        for h0, s0 in pending:
            finish(h0, n_kv, *s0)

    if n_ctx_tiles > tile0:
        @pl.when(i < n_ctx_tiles)
        def _():
            attend(n_ctx)

        @pl.when(i >= n_ctx_tiles)
        def _():
            attend(k_ref.shape[0])
    else:
        attend(k_ref.shape[0])


def _attention(l, q, k, v, lam_vecs, subln_g, n_ctx, tile0, lam_init):
    bsz, t_all, _ = q.shape
    tq = ROW_TILE
    hw = HEADS_PER_STEP * V_DIM
    n_tiles = t_all // tq - tile0
    vec = lambda arr: pl.BlockSpec((None, 1, arr.shape[-1]), lambda b, h, i: (l, 0, 0))
    kern = functools.partial(_attn_kernel, n_ctx=n_ctx, n_ctx_tiles=n_ctx // tq, tile0=tile0, lam_init=lam_init)
    return pl.pallas_call(
        kern,
        grid=(bsz, N_HEADS // HEADS_PER_STEP, n_tiles),
        in_specs=[
            pl.BlockSpec((None, tq, hw), lambda b, h, i: (b, i + tile0, h)),
            pl.BlockSpec((None, t_all, hw), lambda b, h, i: (b, 0, h)),
            pl.BlockSpec((None, t_all, hw), lambda b, h, i: (b, 0, h)),
            vec(lam_vecs[0]), vec(lam_vecs[1]), vec(lam_vecs[2]), vec(lam_vecs[3]), vec(subln_g),
        ],
        out_specs=pl.BlockSpec((None, tq, hw), lambda b, h, i: (b, i, h)),
        out_shape=jax.ShapeDtypeStruct((bsz, n_tiles * tq, ATTN_W), BF16),
        compiler_params=pltpu.CompilerParams(
            dimension_semantics=("parallel", "parallel", "arbitrary"), vmem_limit_bytes=VMEM_LIMIT),
        name="diff_attention",
    )(q, k, v, *lam_vecs, subln_g)


def _ssm_tables(a_re, a_im, log_dt, b_re, b_im, c_re, c_im, d_skip):
    lc = SCAN_CHUNK
    lam = lax.complex(a_re.astype(F32), a_im.astype(F32))
    dt = jnp.exp(log_dt.astype(F32))[..., None]
    ldt = lam * dt
    a_bar = jnp.exp(ldt)
    b_bar = ((a_bar - 1.0) / lam)[..., None] * lax.complex(b_re.astype(F32), b_im.astype(F32))
    cm = lax.complex(c_re.astype(F32), c_im.astype(F32))
    steps = jnp.arange(lc + 1, dtype=F32)
    apow = jnp.exp(ldt[None] * steps[:, None, None, None])
    s_idx = jnp.arange(lc)
    lag_f = s_idx[None, :] - s_idx[:, None]
    lk = lc * LANES
    eye_c = jnp.eye(SSM_GROUP, dtype=F32)
    eye_t = jnp.eye(lc, dtype=F32)
    d_g = d_skip.astype(F32).reshape(SSM_GROUPS, SSM_GROUP)

    def rows_of(t):
        t = t.reshape(lc, LANE_TILES, GROUPS_PER_TILE, SSM_GROUP, LANES)
        return t.transpose(1, 0, 2, 3, 4).reshape(LANE_TILES, lk, LANES)

    inject, readout, intra = [], [], []
    for di in range(2):
        pw_in = apow[lc - 1 - s_idx, di] if di == 0 else apow[s_idx, di]
        w = jnp.einsum('sgp,gpi->sgip', pw_in, b_bar[di])
        inject.append(rows_of(jnp.concatenate([w.real, w.imag], axis=-1)))
        pw_out = apow[s_idx + 1, di] if di == 0 else apow[lc - s_idx, di]
        vv = cm[di][None] * pw_out[:, :, None, :]
        readout.append(rows_of(jnp.concatenate([vv.real, -vv.imag], axis=-1)))
        kern = jnp.einsum('gcp,jgp,gpi->jgci', cm[di], apow[:lc, di], b_bar[di]).real
        lag = lag_f if di == 0 else -lag_f
        toe = jnp.where((lag >= 0)[:, :, None, None, None], kern[jnp.clip(lag, 0, lc - 1)], 0.0)
        toe = toe.transpose(0, 2, 4, 1, 3)
        if di == 0:
            toe = toe + jnp.einsum('st,gi,ic->sgitc', eye_t, d_g, eye_c)
        intra.append(rows_of(toe.reshape(lc, SSM_GROUPS, SSM_GROUP, LANES)))
    al = apow[lc].reshape(2, LANE_TILES, 1, STATE_W)
    return jnp.stack(inject), jnp.stack(readout), jnp.stack(intra), al.real, al.imag


def _ssm_kernel(u_ref, wi_ref, wo_ref, wk_ref, alr_ref, ali_ref, y_ref,
                bc_scr, wct_scr, a_scr, s_scr, h_scr, st_scr):
    ph = pl.program_id(1)
    ti = pl.program_id(2)
    lc = SCAN_CHUNK
    bsz = u_ref.shape[0]
    n_chunks = u_ref.shape[1] // lc
    n_rows = bsz * n_chunks
    lk = lc * LANES

    @pl.when(ti == 0)
    def _():
        st_scr[...] = jnp.zeros_like(st_scr)
        q = lax.broadcasted_iota(jnp.int32, (LANES, lk), 0)
        c = lax.broadcasted_iota(jnp.int32, (LANES, lk), 1)
        sel_state = (q == ((c >> 9) << 6) + (c & (SSM_STATE - 1))).astype(BF16)
        sel_tok = (q == ((c >> 7) << 4) + (c & (SSM_GROUP - 1))).astype(BF16)
        row_g = (lax.broadcasted_iota(jnp.int32, (lk, lk), 0) >> 4) & (GROUPS_PER_TILE - 1)
        col = lax.broadcasted_iota(jnp.int32, (lk, lk), 1)
        same_state = row_g == ((col >> 6) & (GROUPS_PER_TILE - 1))
        same_tok = row_g == ((col >> 4) & (GROUPS_PER_TILE - 1))

        def spread(tab_ref, sel, same):
            full = jnp.dot(tab_ref[...].astype(BF16), sel, preferred_element_type=F32)
            return jnp.where(same, full, 0.0).astype(BF16)

        bc_scr[...] = spread(wi_ref, sel_state, same_state)
        wct_scr[...] = spread(wo_ref, sel_state, same_state)
        a_scr[...] = spread(wk_ref, sel_tok, same_tok)

    x = jnp.concatenate(
        [jnp.concatenate([u_ref[b, pl.ds(s, n_chunks, stride=lc), :] for s in range(lc)], axis=1)
         for b in range(bsz)], axis=0).astype(BF16)
    r = lax.broadcasted_iota(jnp.int32, (n_rows, n_rows), 0)
    cc = lax.broadcasted_iota(jnp.int32, (n_rows, n_rows), 1)
    sh_b, sh_c = bsz.bit_length() - 1, n_chunks.bit_length() - 1
    to_cb = (cc == ((r & (bsz - 1)) << sh_c) + (r >> sh_b)).astype(BF16)
    to_bc = (cc == ((r & (n_chunks - 1)) << sh_b) + (r >> sh_c)).astype(BF16)
    x_cb = jnp.dot(to_cb, x, preferred_element_type=F32).astype(BF16)
    s_scr[...] = jnp.dot(x_cb, bc_scr[...], preferred_element_type=F32)
    alr, ali = alr_ref[...], ali_ref[...]

    def step(c, carry):
        hr, hi = carry
        ce = jnp.where(ph == 0, c, n_chunks - 1 - c)
        rows = pl.ds(pl.multiple_of(ce * bsz, bsz), bsz)
        h_scr[rows, 0:STATE_W] = hr
        h_scr[rows, STATE_W:2 * STATE_W] = hi
        sr = s_scr[rows, 0:STATE_W]
        si = s_scr[rows, STATE_W:2 * STATE_W]
        return alr * hr - ali * hi + sr, alr * hi + ali * hr + si

    hr, hi = lax.fori_loop(0, n_chunks, step, (st_scr[0], st_scr[1]))
    st_scr[0] = hr
    st_scr[1] = hi
    h_bc = jnp.dot(to_bc, h_scr[...].astype(BF16), preferred_element_type=F32).astype(BF16)
    y = (jnp.dot(x, a_scr[...], preferred_element_type=F32)
         + lax.dot_general(h_bc, wct_scr[...], (((1,), (1,)), ((), ())), preferred_element_type=F32))
    for b in range(bsz):
        for s in range(lc):
            y_ref[b, pl.ds(s, n_chunks, stride=lc), :] = y[b * n_chunks:(b + 1) * n_chunks,
                                                           s * LANES:(s + 1) * LANES]


def _ssm(l, u, tables, n_ctx):
    bsz, t_all, _ = u.shape
    lc = SCAN_CHUNK
    tt = SCAN_TIME_TILE
    n_t = t_all // tt
    n_ctx_t = n_ctx // tt
    rows = bsz * tt // lc
    inject, readout, intra, alr, ali = tables
    lk = lc * LANES
    assert 2 * STATE_W == lk and SSM_STATE == 64 and SSM_GROUP == 16 and LANES == 128
    assert bsz & (bsz - 1) == 0 and (tt // lc) & (tt // lc - 1) == 0

    def tile_of(ph, i):
        rev = jnp.where(i < n_ctx_t, n_ctx_t - 1 - i, n_t - 1 - (i - n_ctx_t))
        return jnp.where(ph == 0, i, rev)

    op_spec = lambda r, c: pl.BlockSpec((None, None, None, r, c), lambda j, ph, i: (l, ph, j, 0, 0))
    return pl.pallas_call(
        _ssm_kernel,
        grid=(LANE_TILES, 2, n_t),
        in_specs=[
            pl.BlockSpec((bsz, tt, LANES), lambda j, ph, i: (0, tile_of(ph, i), j)),
            op_spec(lk, LANES), op_spec(lk, LANES), op_spec(lk, LANES),
            op_spec(1, STATE_W), op_spec(1, STATE_W),
        ],
        out_specs=pl.BlockSpec((None, bsz, tt, LANES), lambda j, ph, i: (ph, 0, tile_of(ph, i), j)),
        out_shape=jax.ShapeDtypeStruct((2, bsz, t_all, SSM_W), F32),
        scratch_shapes=[
            pltpu.VMEM((lk, 2 * STATE_W), BF16),
            pltpu.VMEM((lk, 2 * STATE_W), BF16),
            pltpu.VMEM((lk, lk), BF16),
            pltpu.VMEM((rows, 2 * STATE_W), F32),
            pltpu.VMEM((rows, 2 * STATE_W), F32),
            pltpu.VMEM((2, bsz, STATE_W), F32),
        ],
        compiler_params=pltpu.CompilerParams(
            dimension_semantics=("parallel", "arbitrary", "arbitrary"), vmem_limit_bytes=VMEM_LIMIT),
        name="s5_scan",
    )(u, inject, readout, intra, alr, ali)


def _merge_kernel(attn_ref, y_ref, ga_ref, gs_ref, x_ref, g1_ref, sh2_ref, sc2_ref,
                  wpa_ref, wglu_ref, bglu_ref, wps_ref, wo_ref, lng_ref, lnb_ref, wr_ref, br_ref,
                  x1_ref, h2_ref, ch_ref, cnt_ref, *, alpha):
    a = jnp.dot(attn_ref[...], wpa_ref[...], preferred_element_type=F32)
    ys = y_ref[0] + y_ref[1]
    gl = ys * (0.5 * (1.0 + jnp.tanh(math.sqrt(2.0 / math.pi) * (ys + 0.044715 * (ys * ys * ys)))))
    z = jnp.dot(gl.astype(BF16), wglu_ref[...], preferred_element_type=F32) + bglu_ref[...]
    sg = gl * _sigmoid(z)
    s = jnp.dot(sg.astype(BF16), wps_ref[...], preferred_element_type=F32)
    m = ga_ref[...].astype(F32) * a + gs_ref[...].astype(F32) * s
    y = jnp.dot(m.astype(BF16), wo_ref[...], preferred_element_type=F32)
    x1 = _layer_norm(alpha * x_ref[...] + g1_ref[...] * y, lng_ref[...], lnb_ref[...])
    x1_ref[...] = x1
    h2 = x1 * (1.0 + sc2_ref[...]) + sh2_ref[...]
    h2_ref[:, 0, :] = _pack_rows(h2)
    h_hi = h2.astype(BF16)
    h_lo = (h2 - h_hi.astype(F32)).astype(BF16)
    w_hi, w_lo = wr_ref[0], wr_ref[1]
    lg = (jnp.dot(h_hi, w_hi, preferred_element_type=F32)
          + (jnp.dot(h_lo, w_hi, preferred_element_type=F32)
             + jnp.dot(h_hi, w_lo, preferred_element_type=F32))) + br_ref[...]
    e1, e2, w1, w2 = _route_choices(lg)
    lane = lax.broadcasted_iota(jnp.int32, lg.shape, 1).astype(F32)
    ch_ref[...] = _record(lane, e1, e2, w1, w2)
    rows_per_expert = jnp.sum((lane == e1).astype(F32) + (lane == e2).astype(F32), axis=0, keepdims=True)

    @pl.when((pl.program_id(0) == 0) & (pl.program_id(1) == 0))
    def _():
        cnt_ref[...] = jnp.zeros_like(cnt_ref)

    cnt_ref[...] += jnp.broadcast_to(rows_per_expert, cnt_ref.shape)


def _merge(l, attn, y, siga, sigs, xall, mods4, wts, n_ctx, tile0, alpha):
    bsz, t_all, d = xall.shape
    tm = ROW_TILE
    n_ctx_tiles = n_ctx // tm
    n_tiles = t_all // tm - tile0
    t_out = n_tiles * tm
    tok = lambda w: pl.BlockSpec((None, tm, w), lambda b, i: (b, i + tile0, 0))
    own = lambda w: pl.BlockSpec((None, tm, w), lambda b, i: (b, i, 0))
    modv = lambda col: _mod_spec(l, col, bsz, n_ctx_tiles, tile0)
    wpa, wglu, bglu, wps, wo, lng, lnb, wr, br = wts
    kern = functools.partial(_merge_kernel, alpha=alpha)
    return pl.pallas_call(
        kern,
        grid=(bsz, n_tiles),
        in_specs=[
            own(ATTN_W),
            pl.BlockSpec((2, None, tm, SSM_W), lambda b, i: (0, b, i + tile0, 0)),
            tok(D_MODEL), tok(D_MODEL), tok(d),
            modv(2), modv(3), modv(4),
            _layer_mat(wpa, l), _layer_mat(wglu, l), _layer_vec(bglu, l), _layer_mat(wps, l), _layer_mat(wo, l),
            _layer_vec(lng, l), _layer_vec(lnb, l),
            pl.BlockSpec((None,) + wr.shape[1:], lambda b, i: (l, 0, 0, 0)), _layer_vec(br, l),
        ],
        out_specs=[
            own(d),
            pl.BlockSpec((tm, 1, d // 2), lambda b, i: (b * n_tiles + i, 0, 0)),
            pl.BlockSpec((tm, LANES), lambda b, i: (b * n_tiles + i, 0)),
            pl.BlockSpec((SUBLANES, LANES), lambda b, i: (0, 0)),
        ],
        out_shape=[
            jax.ShapeDtypeStruct((bsz, t_out, d), F32),
            jax.ShapeDtypeStruct((bsz * t_out, 1, d // 2), jnp.uint32),
            jax.ShapeDtypeStruct((bsz * t_out, LANES), F32),
            jax.ShapeDtypeStruct((SUBLANES, LANES), F32),
        ],
        compiler_params=pltpu.CompilerParams(
            dimension_semantics=("arbitrary", "arbitrary"), vmem_limit_bytes=VMEM_LIMIT),
        name="merge_ln1",
    )(attn, y, siga, sigs, xall, mods4, mods4, mods4, wpa, wglu, bglu, wps, wo, lng, lnb, wr, br)


def _route_choices(lg):
    shape = lg.shape
    lane = lax.broadcasted_iota(jnp.int32, shape, 1).astype(F32)
    far = jnp.full(shape, 1.0e9, F32)

    def first_max(vals, mask):
        vm = jnp.where(mask, vals, NEG_BIG)
        mx = jnp.max(vm, axis=1, keepdims=True)
        idx = jnp.min(jnp.where(mask & (vm == mx), lane, far), axis=1, keepdims=True)
        return mx, idx

    gmask = lane < N_GROUPS
    gmax, gidx = first_max(lg, gmask)
    gtop = 1.0 / jnp.sum(jnp.where(gmask, jnp.exp(lg - gmax), 0.0), axis=1, keepdims=True)
    lo = N_GROUPS + EXPERTS_PER_GROUP * gidx
    emask = (lane >= lo) & (lane < lo + EXPERTS_PER_GROUP)
    v1, i1 = first_max(lg, emask)
    v2, i2 = first_max(lg, emask & (lane != i1))
    e2 = jnp.exp(v2 - v1)
    den = 1.0 + e2
    return i1 - N_GROUPS, i2 - N_GROUPS, (1.0 / den) * gtop, (e2 / den) * gtop


def _record(lane, a, b, c, d):
    return jnp.where(lane == 0.0, a, jnp.where(lane == 1.0, b, jnp.where(lane == 2.0, c, d)))


def _route_kernel(ch_ref, cnt_ref, rec_ref, blk_ref, start_scr, carry_scr):
    i = pl.program_id(0)
    ch = ch_ref[...]
    shape = ch.shape
    lane = lax.broadcasted_iota(jnp.int32, shape, 1).astype(F32)
    oh1 = (lane == ch[:, 0:1]).astype(F32)
    oh2 = (lane == ch[:, 1:2]).astype(F32)
    c1 = jnp.sum(oh1, axis=0, keepdims=True)
    c2 = jnp.sum(oh2, axis=0, keepdims=True)

    @pl.when(i == 0)
    def _():
        cnt = cnt_ref[0:1, :]
        padded = jnp.floor((cnt + (MOE_BLOCK - 1)) * (1.0 / MOE_BLOCK)) * MOE_BLOCK
        r = lax.broadcasted_iota(jnp.int32, (LANES, LANES), 0)
        c = lax.broadcasted_iota(jnp.int32, (LANES, LANES), 1)
        upper = (r < c).astype(F32)
        start = jnp.dot(jnp.broadcast_to(padded, (SUBLANES, LANES)), upper, preferred_element_type=F32,
                        precision=lax.Precision.HIGHEST)[0:1]
        start_scr[...] = start
        carry_scr[...] = jnp.zeros_like(carry_scr)
        nb = blk_ref.shape[0]
        blk_start = (lax.broadcasted_iota(jnp.int32, (nb, LANES), 0) * MOE_BLOCK).astype(F32)
        elane = lax.broadcasted_iota(jnp.int32, (nb, LANES), 1) < N_EXPERTS
        done = jnp.sum(jnp.where(elane & ((start + padded) <= blk_start), 1.0, 0.0), axis=1, keepdims=True)
        blk_ref[...] = jnp.broadcast_to(jnp.minimum(done, N_EXPERTS - 1.0), (nb, LANES)).astype(jnp.int32)

    tr = shape[0]
    r = lax.broadcasted_iota(jnp.int32, (tr, tr), 0)
    c = lax.broadcasted_iota(jnp.int32, (tr, tr), 1)
    tri = (c < r).astype(BF16)
    base = start_scr[...] + carry_scr[...]
    r1 = jnp.dot(tri, oh1.astype(BF16), preferred_element_type=F32)
    r2 = jnp.dot(tri, oh2.astype(BF16), preferred_element_type=F32) + c1
    d1 = jnp.sum(oh1 * (base + r1), axis=1, keepdims=True)
    d2 = jnp.sum(oh2 * (base + r2), axis=1, keepdims=True)
    carry_scr[...] += c1 + c2
    rec_ref[...] = _record(lane, d1, d2, ch[:, 2:3], ch[:, 3:4])


def _route(choices, counts, n_blocks):
    n_tok = choices.shape[0]
    tr = ROUTE_TILE
    nb_pad = -(-n_blocks // SUBLANES) * SUBLANES
    return pl.pallas_call(
        _route_kernel,
        grid=(n_tok // tr,),
        in_specs=[pl.BlockSpec((tr, LANES), lambda i: (i, 0)),
                  pl.BlockSpec(counts.shape, lambda i: (0, 0))],
        out_specs=[
            pl.BlockSpec((tr, LANES), lambda i: (i, 0)),
            pl.BlockSpec((nb_pad, LANES), lambda i: (0, 0)),
        ],
        out_shape=[
            jax.ShapeDtypeStruct((n_tok, LANES), F32),
            jax.ShapeDtypeStruct((nb_pad, LANES), jnp.int32),
        ],
        scratch_shapes=[pltpu.VMEM((1, LANES), F32)] * 2,
        compiler_params=pltpu.CompilerParams(dimension_semantics=("arbitrary",)),
        name="moe_route",
    )(choices, counts)


def _row_copy(src, dst, sem):
    return pltpu.make_async_copy(src, dst, sem)


def _dispatch_kernel(slot_ref, h_ref, xs_in_ref, xs_ref, sem):
    del xs_in_ref
    tg = h_ref.shape[0]

    def issue(r, carry):
        for k in range(2):
            d = slot_ref[0, 0, k * tg + r]
            _row_copy(h_ref.at[pl.ds(r, 1)], xs_ref.at[pl.ds(d, 1)], sem).start(priority=k)
        return carry

    lax.fori_loop(0, tg, issue, 0, unroll=ISSUE_UNROLL)

    def drain(r, carry):
        for k in range(2):
            _row_copy(h_ref.at[pl.ds(0, 1)], xs_ref.at[pl.ds(0, 1)], sem).wait()
        return carry

    lax.fori_loop(0, tg, drain, 0, unroll=ISSUE_UNROLL)


def _dispatch(h2, slots3, cap):
    n_tok, _, d = h2.shape
    tg = ROW_TILE
    xs0 = jnp.zeros((cap, 1, d), h2.dtype)
    return pl.pallas_call(
        _dispatch_kernel,
        grid=(n_tok // tg,),
        in_specs=[
            pl.BlockSpec((1, 1, 2 * tg), lambda i: (i, 0, 0), memory_space=pltpu.SMEM),
            pl.BlockSpec((tg, 1, d), lambda i: (i, 0, 0)),
            pl.BlockSpec(memory_space=pl.ANY),
        ],
        out_specs=pl.BlockSpec(memory_space=pl.ANY),
        out_shape=jax.ShapeDtypeStruct((cap, 1, d), h2.dtype),
        scratch_shapes=[pltpu.SemaphoreType.DMA(())],
        input_output_aliases={2: 0},
        compiler_params=pltpu.CompilerParams(dimension_semantics=("arbitrary",), has_side_effects=True),
        name="moe_dispatch",
    )(slots3, h2, xs0)


def _expert_kernel(be_ref, xs_ref, w1_hbm, w3_hbm, w2_hbm, ys_ref,
                   w1f, w3f, w2f, w1b, w3b, w2b, sem, slot_scr, *, layer):
    i = pl.program_id(0)
    n = pl.num_programs(0)
    e = be_ref[i]
    fresh = (i == 0) | (e != be_ref[jnp.maximum(i - 1, 0)])

    def fetch(expert, slot):
        return [_row_copy(w1_hbm.at[layer, expert], w1f.at[slot], sem.at[slot]),
                _row_copy(w3_hbm.at[layer, expert], w3f.at[slot], sem.at[slot]),
                _row_copy(w2_hbm.at[layer, expert], w2f.at[slot], sem.at[slot])]

    @pl.when(i == 0)
    def _():
        slot_scr[0] = 0
        for c in fetch(e, 0):
            c.start()

    @pl.when(fresh)
    def _():
        slot = slot_scr[0]
        for c in fetch(e, slot):
            c.wait()
        nxt = lax.while_loop(lambda j: (j < n) & (be_ref[jnp.minimum(j, n - 1)] == e), lambda j: j + 1, i + 1)

        @pl.when(nxt < n)
        def _():
            for c in fetch(be_ref[jnp.minimum(nxt, n - 1)], 1 - slot):
                c.start()

        w1b[...] = w1f[slot].astype(BF16)
        w3b[...] = w3f[slot].astype(BF16)
        w2b[...] = w2f[slot].astype(BF16)
        slot_scr[0] = 1 - slot

    x = _unpack_rows(xs_ref[:, 0, :]).astype(BF16)
    a = jnp.dot(x, w1b[...], preferred_element_type=F32)
    b = jnp.dot(x, w3b[...], preferred_element_type=F32)
    hid = (a * _sigmoid(a)) * b
    ys_ref[...] = _pack_rows(jnp.dot(hid.astype(BF16), w2b[...], preferred_element_type=F32))


def _experts(l, blk_expert, xs, w1, w3, w2):
    cap, _, dp = xs.shape
    d, f = w1.shape[-2:]
    n_blocks = cap // MOE_BLOCK
    rows = pl.BlockSpec((MOE_BLOCK, 1, dp), lambda i, be: (i, 0, 0))
    hbm = pl.BlockSpec(memory_space=pl.ANY)
    grid_spec = pltpu.PrefetchScalarGridSpec(
        num_scalar_prefetch=1,
        grid=(n_blocks,),
        in_specs=[rows, hbm, hbm, hbm],
        out_specs=pl.BlockSpec((MOE_BLOCK, dp), lambda i, be: (i, 0)),
        scratch_shapes=[pltpu.VMEM((2, d, f), F32), pltpu.VMEM((2, d, f), F32), pltpu.VMEM((2, f, d), F32),
                        pltpu.VMEM((d, f), BF16), pltpu.VMEM((d, f), BF16), pltpu.VMEM((f, d), BF16),
                        pltpu.SemaphoreType.DMA((2,)), pltpu.SMEM((1,), jnp.int32)],
    )
    return pl.pallas_call(
        functools.partial(_expert_kernel, layer=l),
        grid_spec=grid_spec,
        out_shape=jax.ShapeDtypeStruct((cap, dp), xs.dtype),
        compiler_params=pltpu.CompilerParams(
            dimension_semantics=("arbitrary",), vmem_limit_bytes=VMEM_LIMIT),
        name="moe_experts",
    )(blk_expert, xs, w1, w3, w2)


def _gather_expert_rows(slot_ref, slot_nxt_ref, ys_ref, buf, sem):
    tm = buf.shape[2]
    g = pl.program_id(0) * pl.num_programs(1) + pl.program_id(1)
    total = pl.num_programs(0) * pl.num_programs(1)
    cur = g & 1

    def issue(ref, slot):
        def body(r, carry):
            for k in range(2):
                d = ref[0, 0, k * tm + r]
                _row_copy(ys_ref.at[pl.ds(d, 1)], buf.at[slot, k, pl.ds(r, 1)], sem.at[slot]).start(priority=k)
            return carry

        lax.fori_loop(0, tm, body, 0, unroll=ISSUE_UNROLL)

    @pl.when(g == 0)
    def _():
        issue(slot_ref, 0)

    @pl.when(g + 1 < total)
    def _():
        issue(slot_nxt_ref, 1 - cur)

    def drain(r, carry):
        for k in range(2):
            _row_copy(ys_ref.at[pl.ds(0, 1)], buf.at[cur, 0, pl.ds(0, 1)], sem.at[cur]).wait()
        return carry

    lax.fori_loop(0, tm, drain, 0, unroll=ISSUE_UNROLL)
    return buf[cur, 0], buf[cur, 1]


def _combine_rows(rec, rows1, rows2, x1, g2, lng, lnb, alpha):
    f = rec[:, 2:3] * _unpack_rows(rows1) + rec[:, 3:4] * _unpack_rows(rows2)
    return _layer_norm(alpha * x1 + g2 * f, lng, lnb)


def _combine_kernel(slot_ref, slot_nxt_ref, rec_ref, x_ref, g2_ref, lng_ref, lnb_ref, ys_ref, o_ref,
                    buf, sem, *, alpha):
    rows1, rows2 = _gather_expert_rows(slot_ref, slot_nxt_ref, ys_ref, buf, sem)
    o_ref[...] = _combine_rows(rec_ref[...], rows1, rows2, x_ref[...], g2_ref[...], lng_ref[...], lnb_ref[...],
                               alpha)


def _combine_inproj_kernel(slot_ref, slot_nxt_ref, rec_ref, x_ref, g2_ref, lng_ref, lnb_ref, ys_ref,
                           sh_ref, sc_ref, cos_ref, sa_ref, sb_ref, w_ref,
                           x2_ref, k_ref, v_ref, u_ref, q_ref, ga_ref, gs_ref, buf, sem, *, alpha):
    rows1, rows2 = _gather_expert_rows(slot_ref, slot_nxt_ref, ys_ref, buf, sem)
    x2 = _combine_rows(rec_ref[...], rows1, rows2, x_ref[...], g2_ref[...], lng_ref[...], lnb_ref[...], alpha)
    x2_ref[...] = x2
    _inproj_body(x2, sh_ref, sc_ref, cos_ref, sa_ref, sb_ref, w_ref, k_ref, v_ref, u_ref, q_ref, ga_ref, gs_ref)


def _combine_specs(l, slots3, rec, x1, mods4, lng, lnb, ys, n_ctx, tile0):
    bsz, t_out, d = x1.shape
    tm = ROW_TILE
    n_tiles = t_out // tm
    total = bsz * n_tiles
    step = lambda b, i: b * n_tiles + i
    in_specs = [
        pl.BlockSpec((1, 1, 2 * tm), lambda b, i: (step(b, i), 0, 0), memory_space=pltpu.SMEM),
        pl.BlockSpec((1, 1, 2 * tm), lambda b, i: (jnp.minimum(step(b, i) + 1, total - 1), 0, 0),
                     memory_space=pltpu.SMEM),
        pl.BlockSpec((tm, LANES), lambda b, i: (step(b, i), 0)),
        pl.BlockSpec((None, tm, d), lambda b, i: (b, i, 0)),
        _mod_spec(l, 5, bsz, n_ctx // tm, tile0),
        _layer_vec(lng, l), _layer_vec(lnb, l),
        pl.BlockSpec(memory_space=pl.ANY),
    ]
    scratch = [pltpu.VMEM((2, 2, tm) + ys.shape[1:], ys.dtype), pltpu.SemaphoreType.DMA((2,))]
    return in_specs, scratch, (slots3, slots3, rec, x1, mods4, lng, lnb, ys)


def _combine(l, slots3, rec, x1, mods4, lng, lnb, ys, n_ctx, tile0, alpha):
    bsz, t_out, d = x1.shape
    tm = ROW_TILE
    in_specs, scratch, args = _combine_specs(l, slots3, rec, x1, mods4, lng, lnb, ys, n_ctx, tile0)
    return pl.pallas_call(
        functools.partial(_combine_kernel, alpha=alpha),
        grid=(bsz, t_out // tm),
        in_specs=in_specs,
        out_specs=pl.BlockSpec((None, tm, d), lambda b, i: (b, i, 0)),
        out_shape=jax.ShapeDtypeStruct((bsz, t_out, d), F32),
        scratch_shapes=scratch,
        compiler_params=pltpu.CompilerParams(dimension_semantics=("arbitrary", "arbitrary")),
        name="moe_combine_ln2",
    )(*args)


def _combine_inproj(l, slots3, rec, x1, mods4, lng, lnb, ys, tabs, w_in_bf, n_ctx, alpha):
    bsz, t_all, d = x1.shape
    tm = ROW_TILE
    in_specs, scratch, args = _combine_specs(l, slots3, rec, x1, mods4, lng, lnb, ys, n_ctx, 0)
    p_specs, p_out_specs, p_out_shape, p_args = _inproj_specs(l + 1, bsz, t_all, mods4, tabs, w_in_bf, n_ctx)
    return pl.pallas_call(
        functools.partial(_combine_inproj_kernel, alpha=alpha),
        grid=(bsz, t_all // tm),
        in_specs=in_specs + p_specs,
        out_specs=[pl.BlockSpec((None, tm, d), lambda b, i: (b, i, 0))] + p_out_specs,
        out_shape=[jax.ShapeDtypeStruct((bsz, t_all, d), F32)] + p_out_shape,
        scratch_shapes=scratch,
        compiler_params=pltpu.CompilerParams(
            dimension_semantics=("arbitrary", "arbitrary"), vmem_limit_bytes=VMEM_LIMIT),
        name="moe_combine_ln2_inproj",
    )(*args, *p_args)


def _rope_tables(n_ctx, n_lat):
    rows = n_lat // GRID_W
    row = jnp.repeat(jnp.arange(rows, dtype=F32), GRID_W)
    col = jnp.tile(jnp.arange(GRID_W, dtype=F32), rows)
    inv = 1.0 / (ROPE_BASE ** (jnp.arange(0, ROPE_AXIS_DIM, 2, dtype=F32) / ROPE_AXIS_DIM))
    half = ROPE_AXIS_DIM // 2
    zeros = jnp.zeros((n_lat, half), F32)
    cos, sa, sb = [], [], []
    for pos in (row, col):
        ang = pos[:, None] * inv
        c, s = jnp.cos(ang), jnp.sin(ang)
        cos += [c, c]
        sa += [-s, zeros]
        sb += [zeros, s]
    reps = LANES // HEAD_DIM

    def full(parts, ctx_fill):
        lat = jnp.tile(jnp.concatenate(parts, axis=1), (1, reps))
        return jnp.concatenate([jnp.full((n_ctx, LANES), ctx_fill, F32), lat], axis=0)

    return full(cos, 1.0), full(sa, 0.0), full(sb, 0.0)


def kernel(x, c, ctx, c_ctx, w_mod, b_mod, w_in, lam_q1, lam_k1, lam_q2, lam_k2, subln_g,
           ssm_a_re, ssm_a_im, ssm_log_dt, ssm_b_re, ssm_b_im, ssm_c_re, ssm_c_im, ssm_d,
           w_glu, b_glu, w_pa, w_ps, w_o, ln1_g, ln1_b,
           router_g_w, router_g_b, router_e_w, router_e_b, moe_w1, moe_w3, moe_w2, ln2_g, ln2_b):
    bsz, n_lat, d = x.shape
    n_ctx = ctx.shape[1]
    depth = w_mod.shape[0]
    assert d == D_MODEL and bsz % SUBLANES == 0
    assert n_ctx % ROW_TILE == 0 and n_lat % ROW_TILE == 0 and n_ctx % SCAN_TIME_TILE == 0
    alpha = (2.0 * depth) ** 0.25
    n_ctx_tiles = n_ctx // ROW_TILE

    mod_rows = -(-(bsz + 1) // SUBLANES) * SUBLANES
    cvec = jnp.concatenate([c, c_ctx[None, :], jnp.zeros((mod_rows - bsz - 1, d), F32)], axis=0)
    mods4 = _modulation(cvec, w_mod, b_mod).reshape(depth, mod_rows, 1, N_MOD * d)
    tabs = _rope_tables(n_ctx, n_lat)
    xall = jnp.concatenate([ctx, x], axis=1)
    vecs = lambda a: a.reshape(depth, 1, a.shape[-1])
    ops = jax.vmap(_ssm_tables)(ssm_a_re, ssm_a_im, ssm_log_dt, ssm_b_re, ssm_b_im, ssm_c_re, ssm_c_im, ssm_d)
    n_pad = LANES - N_GROUPS - N_EXPERTS
    wr = jnp.concatenate([router_g_w, router_e_w, jnp.zeros((depth, d, n_pad), F32)], axis=2)
    br = jnp.concatenate([router_g_b, router_e_b, jnp.zeros((depth, n_pad), F32)], axis=1)
    wr_hi = wr.astype(BF16)
    wr_split = jnp.stack([wr_hi, (wr - wr_hi.astype(F32)).astype(BF16)], axis=1)
    w_in_bf = w_in.astype(BF16)
    wts = (w_pa.astype(BF16), w_glu.astype(BF16), vecs(b_glu), w_ps.astype(BF16), w_o.astype(BF16),
           vecs(ln1_g), vecs(ln1_b), wr_split, vecs(br))
    lam_vecs = (vecs(lam_q1), vecs(lam_k1), vecs(lam_q2), vecs(lam_k2))
    ln2 = (vecs(ln2_g), vecs(ln2_b))
    subln = vecs(subln_g)

    proj = _inproj(0, xall, mods4, tabs, w_in_bf, n_ctx)
    for l in range(depth):
        last = l == depth - 1
        tile0 = n_ctx_tiles if last else 0
        lam_init = 0.8 - 0.6 * math.exp(-0.3 * l)
        k, v, u, q, siga, sigs = proj
        attn = _attention(l, q, k, v, lam_vecs, subln, n_ctx, tile0, lam_init)
        y = _ssm(l, u, ops, n_ctx)
        x1, h2, choices, counts = _merge(l, attn, y, siga, sigs, xall, mods4, wts, n_ctx, tile0, alpha)

        n_tok = h2.shape[0]
        n_blocks = -(-(2 * n_tok + N_EXPERTS * (MOE_BLOCK - 1)) // MOE_BLOCK)
        rec, blk = _route(choices, counts, n_blocks)
        n_tiles_tok = n_tok // ROW_TILE
        slots3 = (rec[:, 0:2].astype(jnp.int32).reshape(n_tiles_tok, ROW_TILE, 2)
                  .transpose(0, 2, 1).reshape(n_tiles_tok, 1, 2 * ROW_TILE))
        xs = _dispatch(h2, slots3, n_blocks * MOE_BLOCK)
        ys = _experts(l, blk[:n_blocks, 0], xs, moe_w1, moe_w3, moe_w2)
        if last:
            xall = _combine(l, slots3, rec, x1, mods4, ln2[0], ln2[1], ys, n_ctx, tile0, alpha)
        else:
            xall, *proj = _combine_inproj(l, slots3, rec, x1, mods4, ln2[0], ln2[1], ys, tabs, w_in_bf, n_ctx,
                                          alpha)
    return xall
```

```python
import functools
import math

import jax
import jax.numpy as jnp
from jax import lax
from jax.experimental import pallas as pl
from jax.experimental.pallas import tpu as pltpu

F32 = jnp.float32
BF16 = jnp.bfloat16

D_MODEL = 1024
N_HEADS = 8
HEAD_DIM = 64
V_DIM = 2 * HEAD_DIM
QK_W = N_HEADS * 2 * HEAD_DIM
ATTN_W = N_HEADS * V_DIM
SSM_W = D_MODEL // 2
SSM_GROUP = 16
SSM_GROUPS = SSM_W // SSM_GROUP
SSM_STATE = 64
N_GROUPS = 4
EXPERTS_PER_GROUP = 8
N_EXPERTS = N_GROUPS * EXPERTS_PER_GROUP
EXPERT_HIDDEN = D_MODEL // 2
KVU_W = QK_W + ATTN_W + SSM_W
IN_W = KVU_W + QK_W + 2 * D_MODEL
N_MOD = 6
GRID_W = 64
ROPE_BASE = 10000.0
ROPE_AXIS_DIM = HEAD_DIM // 2
LN_EPS = 1e-5

LANES = 128
SUBLANES = 8
VMEM_LIMIT = 56 * 1024 * 1024

ROW_TILE = 256
SCAN_CHUNK = 8
SCAN_TIME_TILE = 256
MOE_BLOCK = 256
ROUTE_TILE = 512
HEADS_PER_STEP = 4
SCORE_LOOKAHEAD = 1
ISSUE_UNROLL = 8
LANE_TILES = SSM_W // LANES
GROUPS_PER_TILE = LANES // SSM_GROUP
STATE_W = GROUPS_PER_TILE * SSM_STATE
NEG_BIG = -3.0e38


def _sigmoid(x):
    return 1.0 / (1.0 + jnp.exp(-x))


def _pack_rows(x):
    w = x.shape[1] // 2
    bits = lax.bitcast_convert_type(x.astype(BF16).astype(F32), jnp.uint32)
    return bits[:, :w] | (bits[:, w:] >> 16)


def _unpack_rows(p):
    hi = lax.bitcast_convert_type(p & jnp.uint32(0xFFFF0000), F32)
    lo = lax.bitcast_convert_type(p << 16, F32)
    return jnp.concatenate([hi, lo], axis=1)


def _layer_norm(x, g, b):
    xc = x - jnp.mean(x, axis=-1, keepdims=True)
    var = jnp.mean(xc * xc, axis=-1, keepdims=True)
    return xc * lax.rsqrt(var + LN_EPS) * g + b


def _layer_vec(arr, l):
    return pl.BlockSpec((None, 1, arr.shape[-1]), lambda b, i: (l, 0, 0))


def _layer_mat(arr, l):
    return pl.BlockSpec((None,) + arr.shape[1:], lambda b, i: (l, 0, 0))


def _mod_spec(l, col, bsz, n_ctx_tiles, tile0):
    return pl.BlockSpec((None, None, 1, D_MODEL),
                        lambda b, i: (l, jnp.where(i + tile0 < n_ctx_tiles, bsz, b), 0, col))


def _mod_kernel(c_ref, w_ref, b_ref, o_ref):
    c = c_ref[...]
    s = c * _sigmoid(c)
    o_ref[...] = jnp.dot(s, w_ref[...], preferred_element_type=F32, precision=lax.Precision.HIGHEST) + b_ref[...]


def _modulation(cvec, w_mod, b_mod):
    depth, d, w6 = w_mod.shape
    rows = cvec.shape[0]
    tn = 1024
    return pl.pallas_call(
        _mod_kernel,
        grid=(depth, w6 // tn),
        in_specs=[
            pl.BlockSpec((rows, d), lambda l, j: (0, 0)),
            pl.BlockSpec((None, d, tn), lambda l, j: (l, 0, j)),
            pl.BlockSpec((None, 1, tn), lambda l, j: (l, 0, j)),
        ],
        out_specs=pl.BlockSpec((None, rows, tn), lambda l, j: (l, 0, j)),
        out_shape=jax.ShapeDtypeStruct((depth, rows, w6), F32),
        name="modulation",
    )(cvec, w_mod, b_mod.reshape(depth, 1, w6))


def _inproj_kernel(x_ref, sh_ref, sc_ref, cos_ref, sa_ref, sb_ref, w_ref,
                   k_ref, v_ref, u_ref, q_ref, ga_ref, gs_ref):
    _inproj_body(x_ref[...], sh_ref, sc_ref, cos_ref, sa_ref, sb_ref, w_ref,
                 k_ref, v_ref, u_ref, q_ref, ga_ref, gs_ref)


def _inproj_body(x, sh_ref, sc_ref, cos_ref, sa_ref, sb_ref, w_ref, k_ref, v_ref, u_ref, q_ref, ga_ref, gs_ref):
    h = (x * (1.0 + sc_ref[...]) + sh_ref[...]).astype(BF16)
    cos, sa, sb = cos_ref[...], sa_ref[...], sb_ref[...]

    def rope(t):
        return t * cos + pltpu.roll(t, LANES - 16, 1) * sa + pltpu.roll(t, 16, 1) * sb

    def proj(lo, hi):
        return jnp.dot(h, w_ref[:, lo:hi], preferred_element_type=F32)

    kk = proj(0, QK_W)
    for c in range(QK_W // LANES):
        k_ref[:, c * LANES:(c + 1) * LANES] = rope(kk[:, c * LANES:(c + 1) * LANES]).astype(BF16)
    v_ref[...] = proj(QK_W, QK_W + ATTN_W).astype(BF16)
    u_ref[...] = proj(QK_W + ATTN_W, KVU_W)
    qq = proj(KVU_W, KVU_W + QK_W)
    scale = HEAD_DIM ** -0.5 * math.log2(math.e)
    for c in range(QK_W // LANES):
        q_ref[:, c * LANES:(c + 1) * LANES] = (rope(qq[:, c * LANES:(c + 1) * LANES]) * scale).astype(BF16)
    ga_ref[...] = _sigmoid(proj(KVU_W + QK_W, KVU_W + QK_W + D_MODEL)).astype(BF16)
    gs_ref[...] = _sigmoid(proj(KVU_W + QK_W + D_MODEL, IN_W)).astype(BF16)


def _inproj_specs(l, bsz, t_all, mods4, tabs, w_in_bf, n_ctx):
    tm = ROW_TILE
    n_ctx_tiles = n_ctx // tm
    tok_spec = lambda w: pl.BlockSpec((None, tm, w), lambda b, i: (b, i, 0))
    tab_spec = pl.BlockSpec((tm, LANES), lambda b, i: (i, 0))
    big = lambda w, dt: jax.ShapeDtypeStruct((bsz, t_all, w), dt)
    in_specs = [_mod_spec(l, 0, bsz, n_ctx_tiles, 0), _mod_spec(l, 1, bsz, n_ctx_tiles, 0),
                tab_spec, tab_spec, tab_spec, _layer_mat(w_in_bf, l)]
    out_specs = [tok_spec(QK_W), tok_spec(ATTN_W), tok_spec(SSM_W),
                 tok_spec(QK_W), tok_spec(D_MODEL), tok_spec(D_MODEL)]
    out_shape = [big(QK_W, BF16), big(ATTN_W, BF16), big(SSM_W, F32),
                 big(QK_W, BF16), big(D_MODEL, BF16), big(D_MODEL, BF16)]
    return in_specs, out_specs, out_shape, (mods4, mods4, tabs[0], tabs[1], tabs[2], w_in_bf)


def _inproj(l, xall, mods4, tabs, w_in_bf, n_ctx):
    bsz, t_all, d = xall.shape
    tm = ROW_TILE
    in_specs, out_specs, out_shape, args = _inproj_specs(l, bsz, t_all, mods4, tabs, w_in_bf, n_ctx)
    return pl.pallas_call(
        _inproj_kernel,
        grid=(bsz, t_all // tm),
        in_specs=[pl.BlockSpec((None, tm, d), lambda b, i: (b, i, 0))] + in_specs,
        out_specs=out_specs,
        out_shape=out_shape,
        compiler_params=pltpu.CompilerParams(
            dimension_semantics=("parallel", "arbitrary"), vmem_limit_bytes=VMEM_LIMIT),
        name="inproj",
    )(xall, *args)


def _attn_kernel(q_ref, k_ref, v_ref, lq1_ref, lk1_ref, lq2_ref, lk2_ref, g_ref, o_ref,
                 *, n_ctx, n_ctx_tiles, tile0, lam_init):
    i = pl.program_id(2) + tile0
    lam = (jnp.exp(jnp.sum(lq1_ref[...] * lk1_ref[...], axis=1, keepdims=True))
           - jnp.exp(jnp.sum(lq2_ref[...] * lk2_ref[...], axis=1, keepdims=True)) + lam_init)
    g = g_ref[...]
    lane = lax.broadcasted_iota(jnp.int32, (q_ref.shape[0], V_DIM), 1)

    def scores(hh, n_kv):
        cols = slice(hh * V_DIM, (hh + 1) * V_DIM)
        q = q_ref[:, cols]
        zero = jnp.zeros_like(q)
        k = k_ref[0:n_kv, cols]
        nt = (((1,), (1,)), ((), ()))
        s1 = lax.dot_general(jnp.where(lane < HEAD_DIM, q, zero), k, nt, preferred_element_type=F32)
        s2 = lax.dot_general(jnp.where(lane < HEAD_DIM, zero, q), k, nt, preferred_element_type=F32)
        return s1, s2

    def finish(hh, n_kv, s1, s2):
        cols = slice(hh * V_DIM, (hh + 1) * V_DIM)

        def probs(s):
            p = jnp.exp2(s - jnp.max(s, axis=-1, keepdims=True))
            return p, 1.0 / jnp.sum(p, axis=-1, keepdims=True)

        p1, r1 = probs(s1)
        p2, r2 = probs(s2)
        a = p1 - p2 * (lam * r2 / r1)
        o = jnp.dot(a.astype(BF16), v_ref[0:n_kv, cols], preferred_element_type=F32) * r1
        o = o * lax.rsqrt(jnp.mean(o * o, axis=-1, keepdims=True) + LN_EPS) * g * (1.0 - lam_init)
        o_ref[:, cols] = o.astype(BF16)

    def attend(n_kv):
        pending = []
        for hh in range(HEADS_PER_STEP):
            pending.append((hh, scores(hh, n_kv)))
            if len(pending) > SCORE_LOOKAHEAD:
                h0, s0 = pending.pop(0)
                finish(h0, n_kv, *s0)
        for h0, s0 in pending:
            finish(h0, n_kv, *s0)

    if n_ctx_tiles > tile0:
        @pl.when(i < n_ctx_tiles)
        def _():
            attend(n_ctx)

        @pl.when(i >= n_ctx_tiles)
        def _():
            attend(k_ref.shape[0])
    else:
        attend(k_ref.shape[0])


def _attention(l, q, k, v, lam_vecs, subln_g, n_ctx, tile0, lam_init):
    bsz, t_all, _ = q.shape
    tq = ROW_TILE
    hw = HEADS_PER_STEP * V_DIM
    n_tiles = t_all // tq - tile0
    vec = lambda arr: pl.BlockSpec((None, 1, arr.shape[-1]), lambda b, h, i: (l, 0, 0))
    kern = functools.partial(_attn_kernel, n_ctx=n_ctx, n_ctx_tiles=n_ctx // tq, tile0=tile0, lam_init=lam_init)
    return pl.pallas_call(
        kern,
        grid=(bsz, N_HEADS // HEADS_PER_STEP, n_tiles),
        in_specs=[
            pl.BlockSpec((None, tq, hw), lambda b, h, i: (b, i + tile0, h)),
            pl.BlockSpec((None, t_all, hw), lambda b, h, i: (b, 0, h)),
            pl.BlockSpec((None, t_all, hw), lambda b, h, i: (b, 0, h)),
            vec(lam_vecs[0]), vec(lam_vecs[1]), vec(lam_vecs[2]), vec(lam_vecs[3]), vec(subln_g),
        ],
        out_specs=pl.BlockSpec((None, tq, hw), lambda b, h, i: (b, i, h)),
        out_shape=jax.ShapeDtypeStruct((bsz, n_tiles * tq, ATTN_W), BF16),
        compiler_params=pltpu.CompilerParams(
            dimension_semantics=("parallel", "parallel", "arbitrary"), vmem_limit_bytes=VMEM_LIMIT),
        name="diff_attention",
    )(q, k, v, *lam_vecs, subln_g)


def _ssm_tables(a_re, a_im, log_dt, b_re, b_im, c_re, c_im, d_skip):
    lc = SCAN_CHUNK
    lam = lax.complex(a_re.astype(F32), a_im.astype(F32))
    dt = jnp.exp(log_dt.astype(F32))[..., None]
    ldt = lam * dt
    a_bar = jnp.exp(ldt)
    b_bar = ((a_bar - 1.0) / lam)[..., None] * lax.complex(b_re.astype(F32), b_im.astype(F32))
    cm = lax.complex(c_re.astype(F32), c_im.astype(F32))
    steps = jnp.arange(lc + 1, dtype=F32)
    apow = jnp.exp(ldt[None] * steps[:, None, None, None])
    s_idx = jnp.arange(lc)
    lag_f = s_idx[None, :] - s_idx[:, None]
    lk = lc * LANES
    eye_c = jnp.eye(SSM_GROUP, dtype=F32)
    eye_t = jnp.eye(lc, dtype=F32)
    d_g = d_skip.astype(F32).reshape(SSM_GROUPS, SSM_GROUP)

    def rows_of(t):
        t = t.reshape(lc, LANE_TILES, GROUPS_PER_TILE, SSM_GROUP, LANES)
        return t.transpose(1, 0, 2, 3, 4).reshape(LANE_TILES, lk, LANES)

    inject, readout, intra = [], [], []
    for di in range(2):
        pw_in = apow[lc - 1 - s_idx, di] if di == 0 else apow[s_idx, di]
        w = jnp.einsum('sgp,gpi->sgip', pw_in, b_bar[di])
        inject.append(rows_of(jnp.concatenate([w.real, w.imag], axis=-1)))
        pw_out = apow[s_idx + 1, di] if di == 0 else apow[lc - s_idx, di]
        vv = cm[di][None] * pw_out[:, :, None, :]
        readout.append(rows_of(jnp.concatenate([vv.real, -vv.imag], axis=-1)))
        kern = jnp.einsum('gcp,jgp,gpi->jgci', cm[di], apow[:lc, di], b_bar[di]).real
        lag = lag_f if di == 0 else -lag_f
        toe = jnp.where((lag >= 0)[:, :, None, None, None], kern[jnp.clip(lag, 0, lc - 1)], 0.0)
        toe = toe.transpose(0, 2, 4, 1, 3)
        if di == 0:
            toe = toe + jnp.einsum('st,gi,ic->sgitc', eye_t, d_g, eye_c)
        intra.append(rows_of(toe.reshape(lc, SSM_GROUPS, SSM_GROUP, LANES)))
    al = apow[lc].reshape(2, LANE_TILES, 1, STATE_W)
    return jnp.stack(inject), jnp.stack(readout), jnp.stack(intra), al.real, al.imag


def _ssm_kernel(u_ref, wi_ref, wo_ref, wk_ref, alr_ref, ali_ref, y_ref,
                bc_scr, wct_scr, a_scr, s_scr, h_scr, st_scr):
    ph = pl.program_id(1)
    ti = pl.program_id(2)
    lc = SCAN_CHUNK
    bsz = u_ref.shape[0]
    n_chunks = u_ref.shape[1] // lc
    n_rows = bsz * n_chunks
    lk = lc * LANES

    @pl.when(ti == 0)
    def _():
        st_scr[...] = jnp.zeros_like(st_scr)
        q = lax.broadcasted_iota(jnp.int32, (LANES, lk), 0)
        c = lax.broadcasted_iota(jnp.int32, (LANES, lk), 1)
        sel_state = (q == ((c >> 9) << 6) + (c & (SSM_STATE - 1))).astype(BF16)
        sel_tok = (q == ((c >> 7) << 4) + (c & (SSM_GROUP - 1))).astype(BF16)
        row_g = (lax.broadcasted_iota(jnp.int32, (lk, lk), 0) >> 4) & (GROUPS_PER_TILE - 1)
        col = lax.broadcasted_iota(jnp.int32, (lk, lk), 1)
        same_state = row_g == ((col >> 6) & (GROUPS_PER_TILE - 1))
        same_tok = row_g == ((col >> 4) & (GROUPS_PER_TILE - 1))

        def spread(tab_ref, sel, same):
            full = jnp.dot(tab_ref[...].astype(BF16), sel, preferred_element_type=F32)
            return jnp.where(same, full, 0.0).astype(BF16)

        bc_scr[...] = spread(wi_ref, sel_state, same_state)
        wct_scr[...] = spread(wo_ref, sel_state, same_state)
        a_scr[...] = spread(wk_ref, sel_tok, same_tok)

    x = jnp.concatenate(
        [jnp.concatenate([u_ref[b, pl.ds(s, n_chunks, stride=lc), :] for s in range(lc)], axis=1)
         for b in range(bsz)], axis=0).astype(BF16)
    r = lax.broadcasted_iota(jnp.int32, (n_rows, n_rows), 0)
    cc = lax.broadcasted_iota(jnp.int32, (n_rows, n_rows), 1)
    sh_b, sh_c = bsz.bit_length() - 1, n_chunks.bit_length() - 1
    to_cb = (cc == ((r & (bsz - 1)) << sh_c) + (r >> sh_b)).astype(BF16)
    to_bc = (cc == ((r & (n_chunks - 1)) << sh_b) + (r >> sh_c)).astype(BF16)
    x_cb = jnp.dot(to_cb, x, preferred_element_type=F32).astype(BF16)
    s_scr[...] = jnp.dot(x_cb, bc_scr[...], preferred_element_type=F32)
    alr, ali = alr_ref[...], ali_ref[...]

    def step(c, carry):
        hr, hi = carry
        ce = jnp.where(ph == 0, c, n_chunks - 1 - c)
        rows = pl.ds(pl.multiple_of(ce * bsz, bsz), bsz)
        h_scr[rows, 0:STATE_W] = hr
        h_scr[rows, STATE_W:2 * STATE_W] = hi
        sr = s_scr[rows, 0:STATE_W]
        si = s_scr[rows, STATE_W:2 * STATE_W]
        return alr * hr - ali * hi + sr, alr * hi + ali * hr + si

    hr, hi = lax.fori_loop(0, n_chunks, step, (st_scr[0], st_scr[1]))
    st_scr[0] = hr
    st_scr[1] = hi
    h_bc = jnp.dot(to_bc, h_scr[...].astype(BF16), preferred_element_type=F32).astype(BF16)
    y = (jnp.dot(x, a_scr[...], preferred_element_type=F32)
         + lax.dot_general(h_bc, wct_scr[...], (((1,), (1,)), ((), ())), preferred_element_type=F32))
    for b in range(bsz):
        for s in range(lc):
            y_ref[b, pl.ds(s, n_chunks, stride=lc), :] = y[b * n_chunks:(b + 1) * n_chunks,
                                                           s * LANES:(s + 1) * LANES]


def _ssm(l, u, tables, n_ctx):
    bsz, t_all, _ = u.shape
    lc = SCAN_CHUNK
    tt = SCAN_TIME_TILE
    n_t = t_all // tt
    n_ctx_t = n_ctx // tt
    rows = bsz * tt // lc
    inject, readout, intra, alr, ali = tables
    lk = lc * LANES
    assert 2 * STATE_W == lk and SSM_STATE == 64 and SSM_GROUP == 16 and LANES == 128
    assert bsz & (bsz - 1) == 0 and (tt // lc) & (tt // lc - 1) == 0

    def tile_of(ph, i):
        rev = jnp.where(i < n_ctx_t, n_ctx_t - 1 - i, n_t - 1 - (i - n_ctx_t))
        return jnp.where(ph == 0, i, rev)

    op_spec = lambda r, c: pl.BlockSpec((None, None, None, r, c), lambda j, ph, i: (l, ph, j, 0, 0))
    return pl.pallas_call(
        _ssm_kernel,
        grid=(LANE_TILES, 2, n_t),
        in_specs=[
            pl.BlockSpec((bsz, tt, LANES), lambda j, ph, i: (0, tile_of(ph, i), j)),
            op_spec(lk, LANES), op_spec(lk, LANES), op_spec(lk, LANES),
            op_spec(1, STATE_W), op_spec(1, STATE_W),
        ],
        out_specs=pl.BlockSpec((None, bsz, tt, LANES), lambda j, ph, i: (ph, 0, tile_of(ph, i), j)),
        out_shape=jax.ShapeDtypeStruct((2, bsz, t_all, SSM_W), F32),
        scratch_shapes=[
            pltpu.VMEM((lk, 2 * STATE_W), BF16),
            pltpu.VMEM((lk, 2 * STATE_W), BF16),
            pltpu.VMEM((lk, lk), BF16),
            pltpu.VMEM((rows, 2 * STATE_W), F32),
            pltpu.VMEM((rows, 2 * STATE_W), F32),
            pltpu.VMEM((2, bsz, STATE_W), F32),
        ],
        compiler_params=pltpu.CompilerParams(
            dimension_semantics=("parallel", "arbitrary", "arbitrary"), vmem_limit_bytes=VMEM_LIMIT),
        name="s5_scan",
    )(u, inject, readout, intra, alr, ali)


def _merge_kernel(attn_ref, y_ref, ga_ref, gs_ref, x_ref, g1_ref, sh2_ref, sc2_ref,
                  wpa_ref, wglu_ref, bglu_ref, wps_ref, wo_ref, lng_ref, lnb_ref, wr_ref, br_ref,
                  x1_ref, h2_ref, ch_ref, cnt_ref, *, alpha):
    a = jnp.dot(attn_ref[...], wpa_ref[...], preferred_element_type=F32)
    ys = y_ref[0] + y_ref[1]
    gl = ys * (0.5 * (1.0 + jnp.tanh(math.sqrt(2.0 / math.pi) * (ys + 0.044715 * (ys * ys * ys)))))
    z = jnp.dot(gl.astype(BF16), wglu_ref[...], preferred_element_type=F32) + bglu_ref[...]
    sg = gl * _sigmoid(z)
    s = jnp.dot(sg.astype(BF16), wps_ref[...], preferred_element_type=F32)
    m = ga_ref[...].astype(F32) * a + gs_ref[...].astype(F32) * s
    y = jnp.dot(m.astype(BF16), wo_ref[...], preferred_element_type=F32)
    x1 = _layer_norm(alpha * x_ref[...] + g1_ref[...] * y, lng_ref[...], lnb_ref[...])
    x1_ref[...] = x1
    h2 = x1 * (1.0 + sc2_ref[...]) + sh2_ref[...]
    h2_ref[...] = _pack_rows(h2)
    h_hi = h2.astype(BF16)
    h_lo = (h2 - h_hi.astype(F32)).astype(BF16)
    w_hi, w_lo = wr_ref[0], wr_ref[1]
    lg = (jnp.dot(h_hi, w_hi, preferred_element_type=F32)
          + (jnp.dot(h_lo, w_hi, preferred_element_type=F32)
             + jnp.dot(h_hi, w_lo, preferred_element_type=F32))) + br_ref[...]
    e1, e2, w1, w2 = _route_choices(lg)
    lane = lax.broadcasted_iota(jnp.int32, lg.shape, 1).astype(F32)
    ch_ref[...] = _record(lane, e1, e2, w1, w2)
    rows_per_expert = jnp.sum((lane == e1).astype(F32) + (lane == e2).astype(F32), axis=0, keepdims=True)

    @pl.when((pl.program_id(0) == 0) & (pl.program_id(1) == 0))
    def _():
        cnt_ref[...] = jnp.zeros_like(cnt_ref)

    cnt_ref[...] += jnp.broadcast_to(rows_per_expert, cnt_ref.shape)


def _merge(l, attn, y, siga, sigs, xall, mods4, wts, n_ctx, tile0, alpha):
    bsz, t_all, d = xall.shape
    tm = ROW_TILE
    n_ctx_tiles = n_ctx // tm
    n_tiles = t_all // tm - tile0
    t_out = n_tiles * tm
    tok = lambda w: pl.BlockSpec((None, tm, w), lambda b, i: (b, i + tile0, 0))
    own = lambda w: pl.BlockSpec((None, tm, w), lambda b, i: (b, i, 0))
    modv = lambda col: _mod_spec(l, col, bsz, n_ctx_tiles, tile0)
    wpa, wglu, bglu, wps, wo, lng, lnb, wr, br = wts
    kern = functools.partial(_merge_kernel, alpha=alpha)
    return pl.pallas_call(
        kern,
        grid=(bsz, n_tiles),
        in_specs=[
            own(ATTN_W),
            pl.BlockSpec((2, None, tm, SSM_W), lambda b, i: (0, b, i + tile0, 0)),
            tok(D_MODEL), tok(D_MODEL), tok(d),
            modv(2), modv(3), modv(4),
            _layer_mat(wpa, l), _layer_mat(wglu, l), _layer_vec(bglu, l), _layer_mat(wps, l), _layer_mat(wo, l),
            _layer_vec(lng, l), _layer_vec(lnb, l),
            pl.BlockSpec((None,) + wr.shape[1:], lambda b, i: (l, 0, 0, 0)), _layer_vec(br, l),
        ],
        out_specs=[
            own(d),
            pl.BlockSpec((tm, d // 2), lambda b, i: (b * n_tiles + i, 0)),
            pl.BlockSpec((tm, LANES), lambda b, i: (b * n_tiles + i, 0)),
            pl.BlockSpec((SUBLANES, LANES), lambda b, i: (0, 0)),
        ],
        out_shape=[
            jax.ShapeDtypeStruct((bsz, t_out, d), F32),
            jax.ShapeDtypeStruct((bsz * t_out, d // 2), jnp.uint32),
            jax.ShapeDtypeStruct((bsz * t_out, LANES), F32),
            jax.ShapeDtypeStruct((SUBLANES, LANES), F32),
        ],
        compiler_params=pltpu.CompilerParams(
            dimension_semantics=("arbitrary", "arbitrary"), vmem_limit_bytes=VMEM_LIMIT),
        name="merge_ln1",
    )(attn, y, siga, sigs, xall, mods4, mods4, mods4, wpa, wglu, bglu, wps, wo, lng, lnb, wr, br)


def _route_choices(lg):
    shape = lg.shape
    lane = lax.broadcasted_iota(jnp.int32, shape, 1).astype(F32)
    far = jnp.full(shape, 1.0e9, F32)

    def first_max(vals, mask):
        vm = jnp.where(mask, vals, NEG_BIG)
        mx = jnp.max(vm, axis=1, keepdims=True)
        idx = jnp.min(jnp.where(mask & (vm == mx), lane, far), axis=1, keepdims=True)
        return mx, idx

    gmask = lane < N_GROUPS
    gmax, gidx = first_max(lg, gmask)
    gtop = 1.0 / jnp.sum(jnp.where(gmask, jnp.exp(lg - gmax), 0.0), axis=1, keepdims=True)
    lo = N_GROUPS + EXPERTS_PER_GROUP * gidx
    emask = (lane >= lo) & (lane < lo + EXPERTS_PER_GROUP)
    v1, i1 = first_max(lg, emask)
    v2, i2 = first_max(lg, emask & (lane != i1))
    e2 = jnp.exp(v2 - v1)
    den = 1.0 + e2
    return i1 - N_GROUPS, i2 - N_GROUPS, (1.0 / den) * gtop, (e2 / den) * gtop


def _record(lane, a, b, c, d):
    return jnp.where(lane == 0.0, a, jnp.where(lane == 1.0, b, jnp.where(lane == 2.0, c, d)))


def _route_kernel(ch_ref, cnt_ref, rec_ref, blk_ref, start_scr, carry_scr):
    i = pl.program_id(0)
    ch = ch_ref[...]
    shape = ch.shape
    lane = lax.broadcasted_iota(jnp.int32, shape, 1).astype(F32)
    oh1 = (lane == ch[:, 0:1]).astype(F32)
    oh2 = (lane == ch[:, 1:2]).astype(F32)
    c1 = jnp.sum(oh1, axis=0, keepdims=True)
    c2 = jnp.sum(oh2, axis=0, keepdims=True)

    @pl.when(i == 0)
    def _():
        cnt = cnt_ref[0:1, :]
        padded = jnp.floor((cnt + (MOE_BLOCK - 1)) * (1.0 / MOE_BLOCK)) * MOE_BLOCK
        r = lax.broadcasted_iota(jnp.int32, (LANES, LANES), 0)
        c = lax.broadcasted_iota(jnp.int32, (LANES, LANES), 1)
        upper = (r < c).astype(F32)
        start = jnp.dot(jnp.broadcast_to(padded, (SUBLANES, LANES)), upper, preferred_element_type=F32,
                        precision=lax.Precision.HIGHEST)[0:1]
        start_scr[...] = start
        carry_scr[...] = jnp.zeros_like(carry_scr)
        nb = blk_ref.shape[0]
        blk_start = (lax.broadcasted_iota(jnp.int32, (nb, LANES), 0) * MOE_BLOCK).astype(F32)
        elane = lax.broadcasted_iota(jnp.int32, (nb, LANES), 1) < N_EXPERTS
        done = jnp.sum(jnp.where(elane & ((start + padded) <= blk_start), 1.0, 0.0), axis=1, keepdims=True)
        blk_ref[...] = jnp.broadcast_to(jnp.minimum(done, N_EXPERTS - 1.0), (nb, LANES)).astype(jnp.int32)

    tr = shape[0]
    r = lax.broadcasted_iota(jnp.int32, (tr, tr), 0)
    c = lax.broadcasted_iota(jnp.int32, (tr, tr), 1)
    tri = (c < r).astype(BF16)
    base = start_scr[...] + carry_scr[...]
    r1 = jnp.dot(tri, oh1.astype(BF16), preferred_element_type=F32)
    r2 = jnp.dot(tri, oh2.astype(BF16), preferred_element_type=F32) + c1
    d1 = jnp.sum(oh1 * (base + r1), axis=1, keepdims=True)
    d2 = jnp.sum(oh2 * (base + r2), axis=1, keepdims=True)
    carry_scr[...] += c1 + c2
    rec_ref[...] = _record(lane, d1, d2, ch[:, 2:3], ch[:, 3:4])


def _route(choices, counts, n_blocks):
    n_tok = choices.shape[0]
    tr = ROUTE_TILE
    nb_pad = -(-n_blocks // SUBLANES) * SUBLANES
    return pl.pallas_call(
        _route_kernel,
        grid=(n_tok // tr,),
        in_specs=[pl.BlockSpec((tr, LANES), lambda i: (i, 0)),
                  pl.BlockSpec(counts.shape, lambda i: (0, 0))],
        out_specs=[
            pl.BlockSpec((tr, LANES), lambda i: (i, 0)),
            pl.BlockSpec((nb_pad, LANES), lambda i: (0, 0)),
        ],
        out_shape=[
            jax.ShapeDtypeStruct((n_tok, LANES), F32),
            jax.ShapeDtypeStruct((nb_pad, LANES), jnp.int32),
        ],
        scratch_shapes=[pltpu.VMEM((1, LANES), F32)] * 2,
        compiler_params=pltpu.CompilerParams(dimension_semantics=("arbitrary",)),
        name="moe_route",
    )(choices, counts)


def _row_copy(src, dst, sem):
    return pltpu.make_async_copy(src, dst, sem)


def _dispatch_kernel(slot_ref, h_ref, xs_in_ref, xs_ref, sem):
    del xs_in_ref
    tg = h_ref.shape[0]

    def issue(r, carry):
        for k in range(2):
            d = slot_ref[0, 0, k * tg + r]
            _row_copy(h_ref.at[pl.ds(r, 1)], xs_ref.at[pl.ds(d, 1)], sem).start(priority=k)
        return carry

    lax.fori_loop(0, tg, issue, 0, unroll=ISSUE_UNROLL)

    def drain(r, carry):
        for k in range(2):
            _row_copy(h_ref.at[pl.ds(0, 1)], xs_ref.at[pl.ds(0, 1)], sem).wait()
        return carry

    lax.fori_loop(0, tg, drain, 0, unroll=ISSUE_UNROLL)


def _dispatch(h2, slots3, cap):
    n_tok, d = h2.shape
    tg = ROW_TILE
    xs0 = jnp.zeros((cap, d), h2.dtype)
    return pl.pallas_call(
        _dispatch_kernel,
        grid=(n_tok // tg,),
        in_specs=[
            pl.BlockSpec((1, 1, 2 * tg), lambda i: (i, 0, 0), memory_space=pltpu.SMEM),
            pl.BlockSpec((tg, d), lambda i: (i, 0)),
            pl.BlockSpec(memory_space=pl.ANY),
        ],
        out_specs=pl.BlockSpec(memory_space=pl.ANY),
        out_shape=jax.ShapeDtypeStruct((cap, d), h2.dtype),
        scratch_shapes=[pltpu.SemaphoreType.DMA(())],
        input_output_aliases={2: 0},
        compiler_params=pltpu.CompilerParams(dimension_semantics=("arbitrary",), has_side_effects=True),
        name="moe_dispatch",
    )(slots3, h2, xs0)


def _expert_kernel(be_ref, xs_ref, w1_hbm, w3_hbm, w2_hbm, ys_ref,
                   w1f, w3f, w2f, w1b, w3b, w2b, sem, slot_scr, *, layer):
    i = pl.program_id(0)
    n = pl.num_programs(0)
    e = be_ref[i]
    fresh = (i == 0) | (e != be_ref[jnp.maximum(i - 1, 0)])

    def fetch(expert, slot):
        return [_row_copy(w1_hbm.at[layer, expert], w1f.at[slot], sem.at[slot]),
                _row_copy(w3_hbm.at[layer, expert], w3f.at[slot], sem.at[slot]),
                _row_copy(w2_hbm.at[layer, expert], w2f.at[slot], sem.at[slot])]

    @pl.when(i == 0)
    def _():
        slot_scr[0] = 0
        for c in fetch(e, 0):
            c.start()

    @pl.when(fresh)
    def _():
        slot = slot_scr[0]
        for c in fetch(e, slot):
            c.wait()
        nxt = lax.while_loop(lambda j: (j < n) & (be_ref[jnp.minimum(j, n - 1)] == e), lambda j: j + 1, i + 1)

        @pl.when(nxt < n)
        def _():
            for c in fetch(be_ref[jnp.minimum(nxt, n - 1)], 1 - slot):
                c.start()

        w1b[...] = w1f[slot].astype(BF16)
        w3b[...] = w3f[slot].astype(BF16)
        w2b[...] = w2f[slot].astype(BF16)
        slot_scr[0] = 1 - slot

    x = _unpack_rows(xs_ref[...]).astype(BF16)
    a = jnp.dot(x, w1b[...], preferred_element_type=F32)
    b = jnp.dot(x, w3b[...], preferred_element_type=F32)
    hid = (a * _sigmoid(a)) * b
    ys_ref[...] = _pack_rows(jnp.dot(hid.astype(BF16), w2b[...], preferred_element_type=F32))


def _experts(l, blk_expert, xs, w1, w3, w2):
    cap, dp = xs.shape
    d, f = w1.shape[-2:]
    n_blocks = cap // MOE_BLOCK
    rows = pl.BlockSpec((MOE_BLOCK, dp), lambda i, be: (i, 0))
    hbm = pl.BlockSpec(memory_space=pl.ANY)
    grid_spec = pltpu.PrefetchScalarGridSpec(
        num_scalar_prefetch=1,
        grid=(n_blocks,),
        in_specs=[rows, hbm, hbm, hbm],
        out_specs=pl.BlockSpec((MOE_BLOCK, dp), lambda i, be: (i, 0)),
        scratch_shapes=[pltpu.VMEM((2, d, f), F32), pltpu.VMEM((2, d, f), F32), pltpu.VMEM((2, f, d), F32),
                        pltpu.VMEM((d, f), BF16), pltpu.VMEM((d, f), BF16), pltpu.VMEM((f, d), BF16),
                        pltpu.SemaphoreType.DMA((2,)), pltpu.SMEM((1,), jnp.int32)],
    )
    return pl.pallas_call(
        functools.partial(_expert_kernel, layer=l),
        grid_spec=grid_spec,
        out_shape=jax.ShapeDtypeStruct((cap, dp), xs.dtype),
        compiler_params=pltpu.CompilerParams(
            dimension_semantics=("arbitrary",), vmem_limit_bytes=VMEM_LIMIT),
        name="moe_experts",
    )(blk_expert, xs, w1, w3, w2)


def _gather_expert_rows(slot_ref, slot_nxt_ref, ys_ref, buf, sem):
    tm = buf.shape[2]
    g = pl.program_id(0) * pl.num_programs(1) + pl.program_id(1)
    total = pl.num_programs(0) * pl.num_programs(1)
    cur = g & 1

    def issue(ref, slot):
        def body(r, carry):
            for k in range(2):
                d = ref[0, 0, k * tm + r]
                _row_copy(ys_ref.at[pl.ds(d, 1)], buf.at[slot, k, pl.ds(r, 1)], sem.at[slot]).start(priority=k)
            return carry

        lax.fori_loop(0, tm, body, 0, unroll=ISSUE_UNROLL)

    @pl.when(g == 0)
    def _():
        issue(slot_ref, 0)

    @pl.when(g + 1 < total)
    def _():
        issue(slot_nxt_ref, 1 - cur)

    def drain(r, carry):
        for k in range(2):
            _row_copy(ys_ref.at[pl.ds(0, 1)], buf.at[cur, 0, pl.ds(0, 1)], sem.at[cur]).wait()
        return carry

    lax.fori_loop(0, tm, drain, 0, unroll=ISSUE_UNROLL)
    return buf[cur, 0], buf[cur, 1]


def _combine_rows(rec, rows1, rows2, x1, g2, lng, lnb, alpha):
    f = rec[:, 2:3] * _unpack_rows(rows1) + rec[:, 3:4] * _unpack_rows(rows2)
    return _layer_norm(alpha * x1 + g2 * f, lng, lnb)


def _combine_kernel(slot_ref, slot_nxt_ref, rec_ref, x_ref, g2_ref, lng_ref, lnb_ref, ys_ref, o_ref,
                    buf, sem, *, alpha):
    rows1, rows2 = _gather_expert_rows(slot_ref, slot_nxt_ref, ys_ref, buf, sem)
    o_ref[...] = _combine_rows(rec_ref[...], rows1, rows2, x_ref[...], g2_ref[...], lng_ref[...], lnb_ref[...],
                               alpha)


def _combine_inproj_kernel(slot_ref, slot_nxt_ref, rec_ref, x_ref, g2_ref, lng_ref, lnb_ref, ys_ref,
                           sh_ref, sc_ref, cos_ref, sa_ref, sb_ref, w_ref,
                           x2_ref, k_ref, v_ref, u_ref, q_ref, ga_ref, gs_ref, buf, sem, *, alpha):
    rows1, rows2 = _gather_expert_rows(slot_ref, slot_nxt_ref, ys_ref, buf, sem)
    x2 = _combine_rows(rec_ref[...], rows1, rows2, x_ref[...], g2_ref[...], lng_ref[...], lnb_ref[...], alpha)
    x2_ref[...] = x2
    _inproj_body(x2, sh_ref, sc_ref, cos_ref, sa_ref, sb_ref, w_ref, k_ref, v_ref, u_ref, q_ref, ga_ref, gs_ref)


def _combine_specs(l, slots3, rec, x1, mods4, lng, lnb, ys, n_ctx, tile0):
    bsz, t_out, d = x1.shape
    tm = ROW_TILE
    n_tiles = t_out // tm
    total = bsz * n_tiles
    step = lambda b, i: b * n_tiles + i
    in_specs = [
        pl.BlockSpec((1, 1, 2 * tm), lambda b, i: (step(b, i), 0, 0), memory_space=pltpu.SMEM),
        pl.BlockSpec((1, 1, 2 * tm), lambda b, i: (jnp.minimum(step(b, i) + 1, total - 1), 0, 0),
                     memory_space=pltpu.SMEM),
        pl.BlockSpec((tm, LANES), lambda b, i: (step(b, i), 0)),
        pl.BlockSpec((None, tm, d), lambda b, i: (b, i, 0)),
        _mod_spec(l, 5, bsz, n_ctx // tm, tile0),
        _layer_vec(lng, l), _layer_vec(lnb, l),
        pl.BlockSpec(memory_space=pl.ANY),
    ]
    scratch = [pltpu.VMEM((2, 2, tm) + ys.shape[1:], ys.dtype), pltpu.SemaphoreType.DMA((2,))]
    return in_specs, scratch, (slots3, slots3, rec, x1, mods4, lng, lnb, ys)


def _combine(l, slots3, rec, x1, mods4, lng, lnb, ys, n_ctx, tile0, alpha):
    bsz, t_out, d = x1.shape
    tm = ROW_TILE
    in_specs, scratch, args = _combine_specs(l, slots3, rec, x1, mods4, lng, lnb, ys, n_ctx, tile0)
    return pl.pallas_call(
        functools.partial(_combine_kernel, alpha=alpha),
        grid=(bsz, t_out // tm),
        in_specs=in_specs,
        out_specs=pl.BlockSpec((None, tm, d), lambda b, i: (b, i, 0)),
        out_shape=jax.ShapeDtypeStruct((bsz, t_out, d), F32),
        scratch_shapes=scratch,
        compiler_params=pltpu.CompilerParams(dimension_semantics=("arbitrary", "arbitrary")),
        name="moe_combine_ln2",
    )(*args)


def _combine_inproj(l, slots3, rec, x1, mods4, lng, lnb, ys, tabs, w_in_bf, n_ctx, alpha):
    bsz, t_all, d = x1.shape
    tm = ROW_TILE
    in_specs, scratch, args = _combine_specs(l, slots3, rec, x1, mods4, lng, lnb, ys, n_ctx, 0)
    p_specs, p_out_specs, p_out_shape, p_args = _inproj_specs(l + 1, bsz, t_all, mods4, tabs, w_in_bf, n_ctx)
    return pl.pallas_call(
        functools.partial(_combine_inproj_kernel, alpha=alpha),
        grid=(bsz, t_all // tm),
        in_specs=in_specs + p_specs,
        out_specs=[pl.BlockSpec((None, tm, d), lambda b, i: (b, i, 0))] + p_out_specs,
        out_shape=[jax.ShapeDtypeStruct((bsz, t_all, d), F32)] + p_out_shape,
        scratch_shapes=scratch,
        compiler_params=pltpu.CompilerParams(
            dimension_semantics=("arbitrary", "arbitrary"), vmem_limit_bytes=VMEM_LIMIT),
        name="moe_combine_ln2_inproj",
    )(*args, *p_args)


def _rope_tables(n_ctx, n_lat):
    rows = n_lat // GRID_W
    row = jnp.repeat(jnp.arange(rows, dtype=F32), GRID_W)
    col = jnp.tile(jnp.arange(GRID_W, dtype=F32), rows)
    inv = 1.0 / (ROPE_BASE ** (jnp.arange(0, ROPE_AXIS_DIM, 2, dtype=F32) / ROPE_AXIS_DIM))
    half = ROPE_AXIS_DIM // 2
    zeros = jnp.zeros((n_lat, half), F32)
    cos, sa, sb = [], [], []
    for pos in (row, col):
        ang = pos[:, None] * inv
        c, s = jnp.cos(ang), jnp.sin(ang)
        cos += [c, c]
        sa += [-s, zeros]
        sb += [zeros, s]
    reps = LANES // HEAD_DIM

    def full(parts, ctx_fill):
        lat = jnp.tile(jnp.concatenate(parts, axis=1), (1, reps))
        return jnp.concatenate([jnp.full((n_ctx, LANES), ctx_fill, F32), lat], axis=0)

    return full(cos, 1.0), full(sa, 0.0), full(sb, 0.0)


def kernel(x, c, ctx, c_ctx, w_mod, b_mod, w_in, lam_q1, lam_k1, lam_q2, lam_k2, subln_g,
           ssm_a_re, ssm_a_im, ssm_log_dt, ssm_b_re, ssm_b_im, ssm_c_re, ssm_c_im, ssm_d,
           w_glu, b_glu, w_pa, w_ps, w_o, ln1_g, ln1_b,
           router_g_w, router_g_b, router_e_w, router_e_b, moe_w1, moe_w3, moe_w2, ln2_g, ln2_b):
    bsz, n_lat, d = x.shape
    n_ctx = ctx.shape[1]
    depth = w_mod.shape[0]
    assert d == D_MODEL and bsz % SUBLANES == 0
    assert n_ctx % ROW_TILE == 0 and n_lat % ROW_TILE == 0 and n_ctx % SCAN_TIME_TILE == 0
    alpha = (2.0 * depth) ** 0.25
    n_ctx_tiles = n_ctx // ROW_TILE

    mod_rows = -(-(bsz + 1) // SUBLANES) * SUBLANES
    cvec = jnp.concatenate([c, c_ctx[None, :], jnp.zeros((mod_rows - bsz - 1, d), F32)], axis=0)
    mods4 = _modulation(cvec, w_mod, b_mod).reshape(depth, mod_rows, 1, N_MOD * d)
    tabs = _rope_tables(n_ctx, n_lat)
    xall = jnp.concatenate([ctx, x], axis=1)
    vecs = lambda a: a.reshape(depth, 1, a.shape[-1])
    ops = jax.vmap(_ssm_tables)(ssm_a_re, ssm_a_im, ssm_log_dt, ssm_b_re, ssm_b_im, ssm_c_re, ssm_c_im, ssm_d)
    n_pad = LANES - N_GROUPS - N_EXPERTS
    wr = jnp.concatenate([router_g_w, router_e_w, jnp.zeros((depth, d, n_pad), F32)], axis=2)
    br = jnp.concatenate([router_g_b, router_e_b, jnp.zeros((depth, n_pad), F32)], axis=1)
    wr_hi = wr.astype(BF16)
    wr_split = jnp.stack([wr_hi, (wr - wr_hi.astype(F32)).astype(BF16)], axis=1)
    w_in_bf = w_in.astype(BF16)
    wts = (w_pa.astype(BF16), w_glu.astype(BF16), vecs(b_glu), w_ps.astype(BF16), w_o.astype(BF16),
           vecs(ln1_g), vecs(ln1_b), wr_split, vecs(br))
    lam_vecs = (vecs(lam_q1), vecs(lam_k1), vecs(lam_q2), vecs(lam_k2))
    ln2 = (vecs(ln2_g), vecs(ln2_b))
    subln = vecs(subln_g)

    proj = _inproj(0, xall, mods4, tabs, w_in_bf, n_ctx)
    for l in range(depth):
        last = l == depth - 1
        tile0 = n_ctx_tiles if last else 0
        lam_init = 0.8 - 0.6 * math.exp(-0.3 * l)
        k, v, u, q, siga, sigs = proj
        attn = _attention(l, q, k, v, lam_vecs, subln, n_ctx, tile0, lam_init)
        y = _ssm(l, u, ops, n_ctx)
        x1, h2, choices, counts = _merge(l, attn, y, siga, sigs, xall, mods4, wts, n_ctx, tile0, alpha)

        n_tok = h2.shape[0]
        n_blocks = -(-(2 * n_tok + N_EXPERTS * (MOE_BLOCK - 1)) // MOE_BLOCK)
        rec, blk = _route(choices, counts, n_blocks)
        n_tiles_tok = n_tok // ROW_TILE
        slots3 = (rec[:, 0:2].astype(jnp.int32).reshape(n_tiles_tok, ROW_TILE, 2)
                  .transpose(0, 2, 1).reshape(n_tiles_tok, 1, 2 * ROW_TILE))
        xs = _dispatch(h2, slots3, n_blocks * MOE_BLOCK)
        ys = _experts(l, blk[:n_blocks, 0], xs, moe_w1, moe_w3, moe_w2)
        if last:
            xall = _combine(l, slots3, rec, x1, mods4, ln2[0], ln2[1], ys, n_ctx, tile0, alpha)
        else:
            xall, *proj = _combine_inproj(l, slots3, rec, x1, mods4, ln2[0], ln2[1], ys, tabs, w_in_bf, n_ctx,
                                          alpha)
    return xall
```

```python
import functools
import math

import jax
import jax.numpy as jnp
from jax import lax
from jax.experimental import pallas as pl
from jax.experimental.pallas import tpu as pltpu

F32 = jnp.float32
BF16 = jnp.bfloat16

D_MODEL = 1024
N_HEADS = 8
HEAD_DIM = 64
V_DIM = 2 * HEAD_DIM
QK_W = N_HEADS * 2 * HEAD_DIM
ATTN_W = N_HEADS * V_DIM
SSM_W = D_MODEL // 2
SSM_GROUP = 16
SSM_GROUPS = SSM_W // SSM_GROUP
SSM_STATE = 64
N_GROUPS = 4
EXPERTS_PER_GROUP = 8
N_EXPERTS = N_GROUPS * EXPERTS_PER_GROUP
EXPERT_HIDDEN = D_MODEL // 2
KVU_W = QK_W + ATTN_W + SSM_W
IN_W = KVU_W + QK_W + 2 * D_MODEL
N_MOD = 6
GRID_W = 64
ROPE_BASE = 10000.0
ROPE_AXIS_DIM = HEAD_DIM // 2
LN_EPS = 1e-5

LANES = 128
SUBLANES = 8
VMEM_LIMIT = 56 * 1024 * 1024

ROW_TILE = 256
SCAN_CHUNK = 8
SCAN_TIME_TILE = 256
MOE_BLOCK = 256
ROUTE_TILE = 512
HEADS_PER_STEP = 4
SCORE_LOOKAHEAD = 1
GATHER_PIECES = 16
ISSUE_UNROLL = 8
LANE_TILES = SSM_W // LANES
GROUPS_PER_TILE = LANES // SSM_GROUP
STATE_W = GROUPS_PER_TILE * SSM_STATE
NEG_BIG = -3.0e38


def _sigmoid(x):
    return 1.0 / (1.0 + jnp.exp(-x))


def _pack_rows(x):
    w = x.shape[1] // 2
    bits = lax.bitcast_convert_type(x.astype(BF16).astype(F32), jnp.uint32)
    return bits[:, :w] | (bits[:, w:] >> 16)


def _unpack_rows(p):
    hi = lax.bitcast_convert_type(p & jnp.uint32(0xFFFF0000), F32)
    lo = lax.bitcast_convert_type(p << 16, F32)
    return jnp.concatenate([hi, lo], axis=1)


def _layer_norm(x, g, b):
    xc = x - jnp.mean(x, axis=-1, keepdims=True)
    var = jnp.mean(xc * xc, axis=-1, keepdims=True)
    return xc * lax.rsqrt(var + LN_EPS) * g + b


def _layer_vec(arr, l):
    return pl.BlockSpec((None, 1, arr.shape[-1]), lambda b, i: (l, 0, 0))


def _layer_mat(arr, l):
    return pl.BlockSpec((None,) + arr.shape[1:], lambda b, i: (l, 0, 0))


def _mod_spec(l, col, bsz, n_ctx_tiles, tile0):
    return pl.BlockSpec((None, None, 1, D_MODEL),
                        lambda b, i: (l, jnp.where(i + tile0 < n_ctx_tiles, bsz, b), 0, col))


def _mod_kernel(c_ref, w_ref, b_ref, o_ref):
    c = c_ref[...]
    s = c * _sigmoid(c)
    o_ref[...] = jnp.dot(s, w_ref[...], preferred_element_type=F32, precision=lax.Precision.HIGHEST) + b_ref[...]


def _modulation(cvec, w_mod, b_mod):
    depth, d, w6 = w_mod.shape
    rows = cvec.shape[0]
    tn = 1024
    return pl.pallas_call(
        _mod_kernel,
        grid=(depth, w6 // tn),
        in_specs=[
            pl.BlockSpec((rows, d), lambda l, j: (0, 0)),
            pl.BlockSpec((None, d, tn), lambda l, j: (l, 0, j)),
            pl.BlockSpec((None, 1, tn), lambda l, j: (l, 0, j)),
        ],
        out_specs=pl.BlockSpec((None, rows, tn), lambda l, j: (l, 0, j)),
        out_shape=jax.ShapeDtypeStruct((depth, rows, w6), F32),
        name="modulation",
    )(cvec, w_mod, b_mod.reshape(depth, 1, w6))


def _inproj_kernel(x_ref, sh_ref, sc_ref, cos_ref, sa_ref, sb_ref, w_ref,
                   k_ref, v_ref, u_ref, q_ref, ga_ref, gs_ref):
    _inproj_body(x_ref[...], sh_ref, sc_ref, cos_ref, sa_ref, sb_ref, w_ref,
                 k_ref, v_ref, u_ref, q_ref, ga_ref, gs_ref)


def _inproj_body(x, sh_ref, sc_ref, cos_ref, sa_ref, sb_ref, w_ref, k_ref, v_ref, u_ref, q_ref, ga_ref, gs_ref,
                 between=None):
    def stage_done():
        if between is not None:
            piece = next(between, None)
            if piece is not None:
                piece()

    h = (x * (1.0 + sc_ref[...]) + sh_ref[...]).astype(BF16)
    cos, sa, sb = cos_ref[...], sa_ref[...], sb_ref[...]

    def rope(t):
        return t * cos + pltpu.roll(t, LANES - 16, 1) * sa + pltpu.roll(t, 16, 1) * sb

    def proj(lo, hi):
        return jnp.dot(h, w_ref[:, lo:hi], preferred_element_type=F32)

    kk = proj(0, QK_W)
    for c in range(QK_W // LANES):
        k_ref[:, c * LANES:(c + 1) * LANES] = rope(kk[:, c * LANES:(c + 1) * LANES]).astype(BF16)
        stage_done()
    v_ref[...] = proj(QK_W, QK_W + ATTN_W).astype(BF16)
    stage_done()
    u_ref[...] = proj(QK_W + ATTN_W, KVU_W)
    stage_done()
    qq = proj(KVU_W, KVU_W + QK_W)
    scale = HEAD_DIM ** -0.5 * math.log2(math.e)
    for c in range(QK_W // LANES):
        q_ref[:, c * LANES:(c + 1) * LANES] = (rope(qq[:, c * LANES:(c + 1) * LANES]) * scale).astype(BF16)
        stage_done()
    ga_ref[...] = _sigmoid(proj(KVU_W + QK_W, KVU_W + QK_W + D_MODEL)).astype(BF16)
    stage_done()
    gs_ref[...] = _sigmoid(proj(KVU_W + QK_W + D_MODEL, IN_W)).astype(BF16)
    if between is not None:
        for piece in between:
            piece()


def _inproj_specs(l, bsz, t_all, mods4, tabs, w_in_bf, n_ctx):
    tm = ROW_TILE
    n_ctx_tiles = n_ctx // tm
    tok_spec = lambda w: pl.BlockSpec((None, tm, w), lambda b, i: (b, i, 0))
    tab_spec = pl.BlockSpec((tm, LANES), lambda b, i: (i, 0))
    big = lambda w, dt: jax.ShapeDtypeStruct((bsz, t_all, w), dt)
    in_specs = [_mod_spec(l, 0, bsz, n_ctx_tiles, 0), _mod_spec(l, 1, bsz, n_ctx_tiles, 0),
                tab_spec, tab_spec, tab_spec, _layer_mat(w_in_bf, l)]
    out_specs = [tok_spec(QK_W), tok_spec(ATTN_W), tok_spec(SSM_W),
                 tok_spec(QK_W), tok_spec(D_MODEL), tok_spec(D_MODEL)]
    out_shape = [big(QK_W, BF16), big(ATTN_W, BF16), big(SSM_W, F32),
                 big(QK_W, BF16), big(D_MODEL, BF16), big(D_MODEL, BF16)]
    return in_specs, out_specs, out_shape, (mods4, mods4, tabs[0], tabs[1], tabs[2], w_in_bf)


def _inproj(l, xall, mods4, tabs, w_in_bf, n_ctx):
    bsz, t_all, d = xall.shape
    tm = ROW_TILE
    in_specs, out_specs, out_shape, args = _inproj_specs(l, bsz, t_all, mods4, tabs, w_in_bf, n_ctx)
    return pl.pallas_call(
        _inproj_kernel,
        grid=(bsz, t_all // tm),
        in_specs=[pl.BlockSpec((None, tm, d), lambda b, i: (b, i, 0))] + in_specs,
        out_specs=out_specs,
        out_shape=out_shape,
        compiler_params=pltpu.CompilerParams(
            dimension_semantics=("parallel", "arbitrary"), vmem_limit_bytes=VMEM_LIMIT),
        name="inproj",
    )(xall, *args)


def _attn_kernel(q_ref, k_ref, v_ref, lq1_ref, lk1_ref, lq2_ref, lk2_ref, g_ref, o_ref,
                 *, n_ctx, n_ctx_tiles, tile0, lam_init):
    i = pl.program_id(2) + tile0
    lam = (jnp.exp(jnp.sum(lq1_ref[...] * lk1_ref[...], axis=1, keepdims=True))
           - jnp.exp(jnp.sum(lq2_ref[...] * lk2_ref[...], axis=1, keepdims=True)) + lam_init)
    g = g_ref[...]
    lane = lax.broadcasted_iota(jnp.int32, (q_ref.shape[0], V_DIM), 1)

    def scores(hh, n_kv):
        cols = slice(hh * V_DIM, (hh + 1) * V_DIM)
        q = q_ref[:, cols]
        zero = jnp.zeros_like(q)
        k = k_ref[0:n_kv, cols]
        nt = (((1,), (1,)), ((), ()))
        s1 = lax.dot_general(jnp.where(lane < HEAD_DIM, q, zero), k, nt, preferred_element_type=F32)
        s2 = lax.dot_general(jnp.where(lane < HEAD_DIM, zero, q), k, nt, preferred_element_type=F32)
        return s1, s2

    def finish(hh, n_kv, s1, s2):
        cols = slice(hh * V_DIM, (hh + 1) * V_DIM)

        def probs(s):
            p = jnp.exp2(s - jnp.max(s, axis=-1, keepdims=True))
            return p, 1.0 / jnp.sum(p, axis=-1, keepdims=True)

        p1, r1 = probs(s1)
        p2, r2 = probs(s2)
        a = p1 - p2 * (lam * r2 / r1)
        o = jnp.dot(a.astype(BF16), v_ref[0:n_kv, cols], preferred_element_type=F32) * r1
        o = o * lax.rsqrt(jnp.mean(o * o, axis=-1, keepdims=True) + LN_EPS) * g * (1.0 - lam_init)
        o_ref[:, cols] = o.astype(BF16)

    def attend(n_kv):
        pending = []
        for hh in range(HEADS_PER_STEP):
            pending.append((hh, scores(hh, n_kv)))
            if len(pending) > SCORE_LOOKAHEAD:
                h0, s0 = pending.pop(0)
                finish(h0, n_kv, *s0)
        for h0, s0 in pending:
            finish(h0, n_kv, *s0)

    if n_ctx_tiles > tile0:
        @pl.when(i < n_ctx_tiles)
        def _():
            attend(n_ctx)

        @pl.when(i >= n_ctx_tiles)
        def _():
            attend(k_ref.shape[0])
    else:
        attend(k_ref.shape[0])


def _attention(l, q, k, v, lam_vecs, subln_g, n_ctx, tile0, lam_init):
    bsz, t_all, _ = q.shape
    tq = ROW_TILE
    hw = HEADS_PER_STEP * V_DIM
    n_tiles = t_all // tq - tile0
    vec = lambda arr: pl.BlockSpec((None, 1, arr.shape[-1]), lambda b, h, i: (l, 0, 0))
    kern = functools.partial(_attn_kernel, n_ctx=n_ctx, n_ctx_tiles=n_ctx // tq, tile0=tile0, lam_init=lam_init)
    return pl.pallas_call(
        kern,
        grid=(bsz, N_HEADS // HEADS_PER_STEP, n_tiles),
        in_specs=[
            pl.BlockSpec((None, tq, hw), lambda b, h, i: (b, i + tile0, h)),
            pl.BlockSpec((None, t_all, hw), lambda b, h, i: (b, 0, h)),
            pl.BlockSpec((None, t_all, hw), lambda b, h, i: (b, 0, h)),
            vec(lam_vecs[0]), vec(lam_vecs[1]), vec(lam_vecs[2]), vec(lam_vecs[3]), vec(subln_g),
        ],
        out_specs=pl.BlockSpec((None, tq, hw), lambda b, h, i: (b, i, h)),
        out_shape=jax.ShapeDtypeStruct((bsz, n_tiles * tq, ATTN_W), BF16),
        compiler_params=pltpu.CompilerParams(
            dimension_semantics=("parallel", "parallel", "arbitrary"), vmem_limit_bytes=VMEM_LIMIT),
        name="diff_attention",
    )(q, k, v, *lam_vecs, subln_g)


def _ssm_tables(a_re, a_im, log_dt, b_re, b_im, c_re, c_im, d_skip):
    lc = SCAN_CHUNK
    lam = lax.complex(a_re.astype(F32), a_im.astype(F32))
    dt = jnp.exp(log_dt.astype(F32))[..., None]
    ldt = lam * dt
    a_bar = jnp.exp(ldt)
    b_bar = ((a_bar - 1.0) / lam)[..., None] * lax.complex(b_re.astype(F32), b_im.astype(F32))
    cm = lax.complex(c_re.astype(F32), c_im.astype(F32))
    steps = jnp.arange(lc + 1, dtype=F32)
    apow = jnp.exp(ldt[None] * steps[:, None, None, None])
    s_idx = jnp.arange(lc)
    lag_f = s_idx[None, :] - s_idx[:, None]
    lk = lc * LANES
    eye_c = jnp.eye(SSM_GROUP, dtype=F32)
    eye_t = jnp.eye(lc, dtype=F32)
    d_g = d_skip.astype(F32).reshape(SSM_GROUPS, SSM_GROUP)

    def rows_of(t):
        t = t.reshape(lc, LANE_TILES, GROUPS_PER_TILE, SSM_GROUP, LANES)
        return t.transpose(1, 0, 2, 3, 4).reshape(LANE_TILES, lk, LANES)

    inject, readout, intra = [], [], []
    for di in range(2):
        pw_in = apow[lc - 1 - s_idx, di] if di == 0 else apow[s_idx, di]
        w = jnp.einsum('sgp,gpi->sgip', pw_in, b_bar[di])
        inject.append(rows_of(jnp.concatenate([w.real, w.imag], axis=-1)))
        pw_out = apow[s_idx + 1, di] if di == 0 else apow[lc - s_idx, di]
        vv = cm[di][None] * pw_out[:, :, None, :]
        readout.append(rows_of(jnp.concatenate([vv.real, -vv.imag], axis=-1)))
        kern = jnp.einsum('gcp,jgp,gpi->jgci', cm[di], apow[:lc, di], b_bar[di]).real
        lag = lag_f if di == 0 else -lag_f
        toe = jnp.where((lag >= 0)[:, :, None, None, None], kern[jnp.clip(lag, 0, lc - 1)], 0.0)
        toe = toe.transpose(0, 2, 4, 1, 3)
        if di == 0:
            toe = toe + jnp.einsum('st,gi,ic->sgitc', eye_t, d_g, eye_c)
        intra.append(rows_of(toe.reshape(lc, SSM_GROUPS, SSM_GROUP, LANES)))
    al = apow[lc].reshape(2, LANE_TILES, 1, STATE_W)
    return jnp.stack(inject), jnp.stack(readout), jnp.stack(intra), al.real, al.imag


def _ssm_kernel(u_ref, wi_ref, wo_ref, wk_ref, alr_ref, ali_ref, y_ref,
                bc_scr, wct_scr, a_scr, s_scr, h_scr, st_scr):
    ph = pl.program_id(1)
    ti = pl.program_id(2)
    lc = SCAN_CHUNK
    bsz = u_ref.shape[0]
    n_chunks = u_ref.shape[1] // lc
    n_rows = bsz * n_chunks
    lk = lc * LANES

    @pl.when(ti == 0)
    def _():
        st_scr[...] = jnp.zeros_like(st_scr)
        q = lax.broadcasted_iota(jnp.int32, (LANES, lk), 0)
        c = lax.broadcasted_iota(jnp.int32, (LANES, lk), 1)
        sel_state = (q == ((c >> 9) << 6) + (c & (SSM_STATE - 1))).astype(BF16)
        sel_tok = (q == ((c >> 7) << 4) + (c & (SSM_GROUP - 1))).astype(BF16)
        row_g = (lax.broadcasted_iota(jnp.int32, (lk, lk), 0) >> 4) & (GROUPS_PER_TILE - 1)
        col = lax.broadcasted_iota(jnp.int32, (lk, lk), 1)
        same_state = row_g == ((col >> 6) & (GROUPS_PER_TILE - 1))
        same_tok = row_g == ((col >> 4) & (GROUPS_PER_TILE - 1))

        def spread(tab_ref, sel, same):
            full = jnp.dot(tab_ref[...].astype(BF16), sel, preferred_element_type=F32)
            return jnp.where(same, full, 0.0).astype(BF16)

        bc_scr[...] = spread(wi_ref, sel_state, same_state)
        wct_scr[...] = spread(wo_ref, sel_state, same_state)
        a_scr[...] = spread(wk_ref, sel_tok, same_tok)

    x = jnp.concatenate(
        [jnp.concatenate([u_ref[b, pl.ds(s, n_chunks, stride=lc), :] for s in range(lc)], axis=1)
         for b in range(bsz)], axis=0).astype(BF16)
    r = lax.broadcasted_iota(jnp.int32, (n_rows, n_rows), 0)
    cc = lax.broadcasted_iota(jnp.int32, (n_rows, n_rows), 1)
    sh_b, sh_c = bsz.bit_length() - 1, n_chunks.bit_length() - 1
    to_cb = (cc == ((r & (bsz - 1)) << sh_c) + (r >> sh_b)).astype(BF16)
    to_bc = (cc == ((r & (n_chunks - 1)) << sh_b) + (r >> sh_c)).astype(BF16)
    x_cb = jnp.dot(to_cb, x, preferred_element_type=F32).astype(BF16)
    s_scr[...] = jnp.dot(x_cb, bc_scr[...], preferred_element_type=F32)
    alr, ali = alr_ref[...], ali_ref[...]

    def step(c, carry):
        hr, hi = carry
        ce = jnp.where(ph == 0, c, n_chunks - 1 - c)
        rows = pl.ds(pl.multiple_of(ce * bsz, bsz), bsz)
        h_scr[rows, 0:STATE_W] = hr
        h_scr[rows, STATE_W:2 * STATE_W] = hi
        sr = s_scr[rows, 0:STATE_W]
        si = s_scr[rows, STATE_W:2 * STATE_W]
        return alr * hr - ali * hi + sr, alr * hi + ali * hr + si

    hr, hi = lax.fori_loop(0, n_chunks, step, (st_scr[0], st_scr[1]))
    st_scr[0] = hr
    st_scr[1] = hi
    h_bc = jnp.dot(to_bc, h_scr[...].astype(BF16), preferred_element_type=F32).astype(BF16)
    y = (jnp.dot(x, a_scr[...], preferred_element_type=F32)
         + lax.dot_general(h_bc, wct_scr[...], (((1,), (1,)), ((), ())), preferred_element_type=F32))
    for b in range(bsz):
        for s in range(lc):
            y_ref[b, pl.ds(s, n_chunks, stride=lc), :] = y[b * n_chunks:(b + 1) * n_chunks,
                                                           s * LANES:(s + 1) * LANES]


def _ssm(l, u, tables, n_ctx):
    bsz, t_all, _ = u.shape
    lc = SCAN_CHUNK
    tt = SCAN_TIME_TILE
    n_t = t_all // tt
    n_ctx_t = n_ctx // tt
    rows = bsz * tt // lc
    inject, readout, intra, alr, ali = tables
    lk = lc * LANES
    assert 2 * STATE_W == lk and SSM_STATE == 64 and SSM_GROUP == 16 and LANES == 128
    assert bsz & (bsz - 1) == 0 and (tt // lc) & (tt // lc - 1) == 0

    def tile_of(ph, i):
        rev = jnp.where(i < n_ctx_t, n_ctx_t - 1 - i, n_t - 1 - (i - n_ctx_t))
        return jnp.where(ph == 0, i, rev)

    op_spec = lambda r, c: pl.BlockSpec((None, None, None, r, c), lambda j, ph, i: (l, ph, j, 0, 0))
    return pl.pallas_call(
        _ssm_kernel,
        grid=(LANE_TILES, 2, n_t),
        in_specs=[
            pl.BlockSpec((bsz, tt, LANES), lambda j, ph, i: (0, tile_of(ph, i), j)),
            op_spec(lk, LANES), op_spec(lk, LANES), op_spec(lk, LANES),
            op_spec(1, STATE_W), op_spec(1, STATE_W),
        ],
        out_specs=pl.BlockSpec((None, bsz, tt, LANES), lambda j, ph, i: (ph, 0, tile_of(ph, i), j)),
        out_shape=jax.ShapeDtypeStruct((2, bsz, t_all, SSM_W), F32),
        scratch_shapes=[
            pltpu.VMEM((lk, 2 * STATE_W), BF16),
            pltpu.VMEM((lk, 2 * STATE_W), BF16),
            pltpu.VMEM((lk, lk), BF16),
            pltpu.VMEM((rows, 2 * STATE_W), F32),
            pltpu.VMEM((rows, 2 * STATE_W), F32),
            pltpu.VMEM((2, bsz, STATE_W), F32),
        ],
        compiler_params=pltpu.CompilerParams(
            dimension_semantics=("parallel", "arbitrary", "arbitrary"), vmem_limit_bytes=VMEM_LIMIT),
        name="s5_scan",
    )(u, inject, readout, intra, alr, ali)


def _merge_kernel(attn_ref, y_ref, ga_ref, gs_ref, x_ref, g1_ref, sh2_ref, sc2_ref,
                  wpa_ref, wglu_ref, bglu_ref, wps_ref, wo_ref, lng_ref, lnb_ref, wr_ref, br_ref,
                  x1_ref, h2_ref, ch_ref, cnt_ref, *, alpha):
    a = jnp.dot(attn_ref[...], wpa_ref[...], preferred_element_type=F32)
    ys = y_ref[0] + y_ref[1]
    gl = ys * (0.5 * (1.0 + jnp.tanh(math.sqrt(2.0 / math.pi) * (ys + 0.044715 * (ys * ys * ys)))))
    z = jnp.dot(gl.astype(BF16), wglu_ref[...], preferred_element_type=F32) + bglu_ref[...]
    sg = gl * _sigmoid(z)
    s = jnp.dot(sg.astype(BF16), wps_ref[...], preferred_element_type=F32)
    m = ga_ref[...].astype(F32) * a + gs_ref[...].astype(F32) * s
    y = jnp.dot(m.astype(BF16), wo_ref[...], preferred_element_type=F32)
    x1 = _layer_norm(alpha * x_ref[...] + g1_ref[...] * y, lng_ref[...], lnb_ref[...])
    x1_ref[...] = x1
    h2 = x1 * (1.0 + sc2_ref[...]) + sh2_ref[...]
    h2_ref[...] = _pack_rows(h2)
    h_hi = h2.astype(BF16)
    h_lo = (h2 - h_hi.astype(F32)).astype(BF16)
    w_hi, w_lo = wr_ref[0], wr_ref[1]
    lg = (jnp.dot(h_hi, w_hi, preferred_element_type=F32)
          + (jnp.dot(h_lo, w_hi, preferred_element_type=F32)
             + jnp.dot(h_hi, w_lo, preferred_element_type=F32))) + br_ref[...]
    e1, e2, w1, w2 = _route_choices(lg)
    lane = lax.broadcasted_iota(jnp.int32, lg.shape, 1).astype(F32)
    ch_ref[...] = _record(lane, e1, e2, w1, w2)
    rows_per_expert = jnp.sum((lane == e1).astype(F32) + (lane == e2).astype(F32), axis=0, keepdims=True)

    @pl.when((pl.program_id(0) == 0) & (pl.program_id(1) == 0))
    def _():
        cnt_ref[...] = jnp.zeros_like(cnt_ref)

    cnt_ref[...] += jnp.broadcast_to(rows_per_expert, cnt_ref.shape)


def _merge(l, attn, y, siga, sigs, xall, mods4, wts, n_ctx, tile0, alpha):
    bsz, t_all, d = xall.shape
    tm = ROW_TILE
    n_ctx_tiles = n_ctx // tm
    n_tiles = t_all // tm - tile0
    t_out = n_tiles * tm
    tok = lambda w: pl.BlockSpec((None, tm, w), lambda b, i: (b, i + tile0, 0))
    own = lambda w: pl.BlockSpec((None, tm, w), lambda b, i: (b, i, 0))
    modv = lambda col: _mod_spec(l, col, bsz, n_ctx_tiles, tile0)
    wpa, wglu, bglu, wps, wo, lng, lnb, wr, br = wts
    kern = functools.partial(_merge_kernel, alpha=alpha)
    return pl.pallas_call(
        kern,
        grid=(bsz, n_tiles),
        in_specs=[
            own(ATTN_W),
            pl.BlockSpec((2, None, tm, SSM_W), lambda b, i: (0, b, i + tile0, 0)),
            tok(D_MODEL), tok(D_MODEL), tok(d),
            modv(2), modv(3), modv(4),
            _layer_mat(wpa, l), _layer_mat(wglu, l), _layer_vec(bglu, l), _layer_mat(wps, l), _layer_mat(wo, l),
            _layer_vec(lng, l), _layer_vec(lnb, l),
            pl.BlockSpec((None,) + wr.shape[1:], lambda b, i: (l, 0, 0, 0)), _layer_vec(br, l),
        ],
        out_specs=[
            own(d),
            pl.BlockSpec((tm, d // 2), lambda b, i: (b * n_tiles + i, 0)),
            pl.BlockSpec((tm, LANES), lambda b, i: (b * n_tiles + i, 0)),
            pl.BlockSpec((SUBLANES, LANES), lambda b, i: (0, 0)),
        ],
        out_shape=[
            jax.ShapeDtypeStruct((bsz, t_out, d), F32),
            jax.ShapeDtypeStruct((bsz * t_out, d // 2), jnp.uint32),
            jax.ShapeDtypeStruct((bsz * t_out, LANES), F32),
            jax.ShapeDtypeStruct((SUBLANES, LANES), F32),
        ],
        compiler_params=pltpu.CompilerParams(
            dimension_semantics=("arbitrary", "arbitrary"), vmem_limit_bytes=VMEM_LIMIT),
        name="merge_ln1",
    )(attn, y, siga, sigs, xall, mods4, mods4, mods4, wpa, wglu, bglu, wps, wo, lng, lnb, wr, br)


def _route_choices(lg):
    shape = lg.shape
    lane = lax.broadcasted_iota(jnp.int32, shape, 1).astype(F32)
    far = jnp.full(shape, 1.0e9, F32)

    def first_max(vals, mask):
        vm = jnp.where(mask, vals, NEG_BIG)
        mx = jnp.max(vm, axis=1, keepdims=True)
        idx = jnp.min(jnp.where(mask & (vm == mx), lane, far), axis=1, keepdims=True)
        return mx, idx

    gmask = lane < N_GROUPS
    gmax, gidx = first_max(lg, gmask)
    gtop = 1.0 / jnp.sum(jnp.where(gmask, jnp.exp(lg - gmax), 0.0), axis=1, keepdims=True)
    lo = N_GROUPS + EXPERTS_PER_GROUP * gidx
    emask = (lane >= lo) & (lane < lo + EXPERTS_PER_GROUP)
    v1, i1 = first_max(lg, emask)
    v2, i2 = first_max(lg, emask & (lane != i1))
    e2 = jnp.exp(v2 - v1)
    den = 1.0 + e2
    return i1 - N_GROUPS, i2 - N_GROUPS, (1.0 / den) * gtop, (e2 / den) * gtop


def _record(lane, a, b, c, d):
    return jnp.where(lane == 0.0, a, jnp.where(lane == 1.0, b, jnp.where(lane == 2.0, c, d)))


def _route_kernel(ch_ref, cnt_ref, rec_ref, blk_ref, start_scr, carry_scr):
    i = pl.program_id(0)
    ch = ch_ref[...]
    shape = ch.shape
    lane = lax.broadcasted_iota(jnp.int32, shape, 1).astype(F32)
    oh1 = (lane == ch[:, 0:1]).astype(F32)
    oh2 = (lane == ch[:, 1:2]).astype(F32)
    c1 = jnp.sum(oh1, axis=0, keepdims=True)
    c2 = jnp.sum(oh2, axis=0, keepdims=True)

    @pl.when(i == 0)
    def _():
        cnt = cnt_ref[0:1, :]
        padded = jnp.floor((cnt + (MOE_BLOCK - 1)) * (1.0 / MOE_BLOCK)) * MOE_BLOCK
        r = lax.broadcasted_iota(jnp.int32, (LANES, LANES), 0)
        c = lax.broadcasted_iota(jnp.int32, (LANES, LANES), 1)
        upper = (r < c).astype(F32)
        start = jnp.dot(jnp.broadcast_to(padded, (SUBLANES, LANES)), upper, preferred_element_type=F32,
                        precision=lax.Precision.HIGHEST)[0:1]
        start_scr[...] = start
        carry_scr[...] = jnp.zeros_like(carry_scr)
        nb = blk_ref.shape[0]
        blk_start = (lax.broadcasted_iota(jnp.int32, (nb, LANES), 0) * MOE_BLOCK).astype(F32)
        elane = lax.broadcasted_iota(jnp.int32, (nb, LANES), 1) < N_EXPERTS
        done = jnp.sum(jnp.where(elane & ((start + padded) <= blk_start), 1.0, 0.0), axis=1, keepdims=True)
        blk_ref[...] = jnp.broadcast_to(jnp.minimum(done, N_EXPERTS - 1.0), (nb, LANES)).astype(jnp.int32)

    tr = shape[0]
    r = lax.broadcasted_iota(jnp.int32, (tr, tr), 0)
    c = lax.broadcasted_iota(jnp.int32, (tr, tr), 1)
    tri = (c < r).astype(BF16)
    base = start_scr[...] + carry_scr[...]
    r1 = jnp.dot(tri, oh1.astype(BF16), preferred_element_type=F32)
    r2 = jnp.dot(tri, oh2.astype(BF16), preferred_element_type=F32) + c1
    d1 = jnp.sum(oh1 * (base + r1), axis=1, keepdims=True)
    d2 = jnp.sum(oh2 * (base + r2), axis=1, keepdims=True)
    carry_scr[...] += c1 + c2
    rec_ref[...] = _record(lane, d1, d2, ch[:, 2:3], ch[:, 3:4])


def _route(choices, counts, n_blocks):
    n_tok = choices.shape[0]
    tr = ROUTE_TILE
    nb_pad = -(-n_blocks // SUBLANES) * SUBLANES
    return pl.pallas_call(
        _route_kernel,
        grid=(n_tok // tr,),
        in_specs=[pl.BlockSpec((tr, LANES), lambda i: (i, 0)),
                  pl.BlockSpec(counts.shape, lambda i: (0, 0))],
        out_specs=[
            pl.BlockSpec((tr, LANES), lambda i: (i, 0)),
            pl.BlockSpec((nb_pad, LANES), lambda i: (0, 0)),
        ],
        out_shape=[
            jax.ShapeDtypeStruct((n_tok, LANES), F32),
            jax.ShapeDtypeStruct((nb_pad, LANES), jnp.int32),
        ],
        scratch_shapes=[pltpu.VMEM((1, LANES), F32)] * 2,
        compiler_params=pltpu.CompilerParams(dimension_semantics=("arbitrary",)),
        name="moe_route",
    )(choices, counts)


def _row_copy(src, dst, sem):
    return pltpu.make_async_copy(src, dst, sem)


def _dispatch_kernel(slot_ref, h_ref, xs_in_ref, xs_ref, sem):
    del xs_in_ref
    tg = h_ref.shape[0]

    for r in range(tg):
        for k in range(2):
            d = slot_ref[0, 0, k * tg + r]
            _row_copy(h_ref.at[pl.ds(r, 1)], xs_ref.at[pl.ds(d, 1)], sem).start(priority=k)

    def drain(r, carry):
        for k in range(2):
            _row_copy(h_ref.at[pl.ds(0, 1)], xs_ref.at[pl.ds(0, 1)], sem).wait()
        return carry

    lax.fori_loop(0, tg, drain, 0, unroll=ISSUE_UNROLL)


def _dispatch(h2, slots3, cap):
    n_tok, d = h2.shape
    tg = ROW_TILE
    xs0 = jnp.zeros((cap, d), h2.dtype)
    return pl.pallas_call(
        _dispatch_kernel,
        grid=(n_tok // tg,),
        in_specs=[
            pl.BlockSpec((1, 1, 2 * tg), lambda i: (i, 0, 0), memory_space=pltpu.SMEM),
            pl.BlockSpec((tg, d), lambda i: (i, 0)),
            pl.BlockSpec(memory_space=pl.ANY),
        ],
        out_specs=pl.BlockSpec(memory_space=pl.ANY),
        out_shape=jax.ShapeDtypeStruct((cap, d), h2.dtype),
        scratch_shapes=[pltpu.SemaphoreType.DMA(())],
        input_output_aliases={2: 0},
        compiler_params=pltpu.CompilerParams(dimension_semantics=("arbitrary",), has_side_effects=True),
        name="moe_dispatch",
    )(slots3, h2, xs0)


def _expert_kernel(be_ref, xs_ref, w1_hbm, w3_hbm, w2_hbm, ys_ref,
                   w1f, w3f, w2f, w1b, w3b, w2b, sem, slot_scr, *, layer):
    i = pl.program_id(0)
    n = pl.num_programs(0)
    e = be_ref[i]
    fresh = (i == 0) | (e != be_ref[jnp.maximum(i - 1, 0)])

    def fetch(expert, slot):
        return [_row_copy(w1_hbm.at[layer, expert], w1f.at[slot], sem.at[slot]),
                _row_copy(w3_hbm.at[layer, expert], w3f.at[slot], sem.at[slot]),
                _row_copy(w2_hbm.at[layer, expert], w2f.at[slot], sem.at[slot])]

    @pl.when(i == 0)
    def _():
        slot_scr[0] = 0
        for c in fetch(e, 0):
            c.start()

    @pl.when(fresh)
    def _():
        slot = slot_scr[0]
        for c in fetch(e, slot):
            c.wait()
        nxt = lax.while_loop(lambda j: (j < n) & (be_ref[jnp.minimum(j, n - 1)] == e), lambda j: j + 1, i + 1)

        @pl.when(nxt < n)
        def _():
            for c in fetch(be_ref[jnp.minimum(nxt, n - 1)], 1 - slot):
                c.start()

        w1b[...] = w1f[slot].astype(BF16)
        w3b[...] = w3f[slot].astype(BF16)
        w2b[...] = w2f[slot].astype(BF16)
        slot_scr[0] = 1 - slot

    x = _unpack_rows(xs_ref[...]).astype(BF16)
    a = jnp.dot(x, w1b[...], preferred_element_type=F32)
    b = jnp.dot(x, w3b[...], preferred_element_type=F32)
    hid = (a * _sigmoid(a)) * b
    ys_ref[...] = _pack_rows(jnp.dot(hid.astype(BF16), w2b[...], preferred_element_type=F32))


def _experts(l, blk_expert, xs, w1, w3, w2):
    cap, dp = xs.shape
    d, f = w1.shape[-2:]
    n_blocks = cap // MOE_BLOCK
    rows = pl.BlockSpec((MOE_BLOCK, dp), lambda i, be: (i, 0))
    hbm = pl.BlockSpec(memory_space=pl.ANY)
    grid_spec = pltpu.PrefetchScalarGridSpec(
        num_scalar_prefetch=1,
        grid=(n_blocks,),
        in_specs=[rows, hbm, hbm, hbm],
        out_specs=pl.BlockSpec((MOE_BLOCK, dp), lambda i, be: (i, 0)),
        scratch_shapes=[pltpu.VMEM((2, d, f), F32), pltpu.VMEM((2, d, f), F32), pltpu.VMEM((2, f, d), F32),
                        pltpu.VMEM((d, f), BF16), pltpu.VMEM((d, f), BF16), pltpu.VMEM((f, d), BF16),
                        pltpu.SemaphoreType.DMA((2,)), pltpu.SMEM((1,), jnp.int32)],
    )
    return pl.pallas_call(
        functools.partial(_expert_kernel, layer=l),
        grid_spec=grid_spec,
        out_shape=jax.ShapeDtypeStruct((cap, dp), xs.dtype),
        compiler_params=pltpu.CompilerParams(
            dimension_semantics=("arbitrary",), vmem_limit_bytes=VMEM_LIMIT),
        name="moe_experts",
    )(blk_expert, xs, w1, w3, w2)


def _gather_expert_rows(slot_ref, slot_nxt_ref, ys_ref, buf, sem):
    tm = buf.shape[2]
    g = pl.program_id(0) * pl.num_programs(1) + pl.program_id(1)
    cur = g & 1

    def issue(ref, slot, rows):
        for r in rows:
            for k in range(2):
                d = ref[0, 0, k * tm + r]
                _row_copy(ys_ref.at[pl.ds(d, 1)], buf.at[slot, k, pl.ds(r, 1)], sem.at[slot]).start(priority=k)

    @pl.when(g == 0)
    def _():
        issue(slot_ref, 0, range(tm))

    _wait_gathers(ys_ref, buf, sem, cur)
    per = tm // GATHER_PIECES
    pieces = [functools.partial(issue, slot_nxt_ref, 1 - cur, range(p * per, (p + 1) * per))
              for p in range(GATHER_PIECES)]
    return buf[cur, 0], buf[cur, 1], pieces


def _wait_gathers(ys_ref, buf, sem, slot):
    tm = buf.shape[2]

    def drain(r, carry):
        for k in range(2):
            _row_copy(ys_ref.at[pl.ds(0, 1)], buf.at[slot, 0, pl.ds(0, 1)], sem.at[slot]).wait()
        return carry

    lax.fori_loop(0, tm, drain, 0, unroll=ISSUE_UNROLL)


def _finish_gathers(ys_ref, buf, sem):
    g = pl.program_id(0) * pl.num_programs(1) + pl.program_id(1)

    @pl.when(g == pl.num_programs(0) * pl.num_programs(1) - 1)
    def _():
        _wait_gathers(ys_ref, buf, sem, 1 - (g & 1))


def _combine_rows(rec, rows1, rows2, x1, g2, lng, lnb, alpha):
    f = rec[:, 2:3] * _unpack_rows(rows1) + rec[:, 3:4] * _unpack_rows(rows2)
    return _layer_norm(alpha * x1 + g2 * f, lng, lnb)


def _combine_kernel(slot_ref, slot_nxt_ref, rec_ref, x_ref, g2_ref, lng_ref, lnb_ref, ys_ref, o_ref,
                    buf, sem, *, alpha):
    rows1, rows2, pieces = _gather_expert_rows(slot_ref, slot_nxt_ref, ys_ref, buf, sem)
    for piece in pieces:
        piece()
    o_ref[...] = _combine_rows(rec_ref[...], rows1, rows2, x_ref[...], g2_ref[...], lng_ref[...], lnb_ref[...],
                               alpha)
    _finish_gathers(ys_ref, buf, sem)


def _combine_inproj_kernel(slot_ref, slot_nxt_ref, rec_ref, x_ref, g2_ref, lng_ref, lnb_ref, ys_ref,
                           sh_ref, sc_ref, cos_ref, sa_ref, sb_ref, w_ref,
                           x2_ref, k_ref, v_ref, u_ref, q_ref, ga_ref, gs_ref, buf, sem, *, alpha):
    rows1, rows2, pieces = _gather_expert_rows(slot_ref, slot_nxt_ref, ys_ref, buf, sem)
    x2 = _combine_rows(rec_ref[...], rows1, rows2, x_ref[...], g2_ref[...], lng_ref[...], lnb_ref[...], alpha)
    x2_ref[...] = x2
    _inproj_body(x2, sh_ref, sc_ref, cos_ref, sa_ref, sb_ref, w_ref, k_ref, v_ref, u_ref, q_ref, ga_ref, gs_ref,
                 between=iter(pieces))
    _finish_gathers(ys_ref, buf, sem)


def _combine_specs(l, slots3, rec, x1, mods4, lng, lnb, ys, n_ctx, tile0):
    bsz, t_out, d = x1.shape
    tm = ROW_TILE
    n_tiles = t_out // tm
    total = bsz * n_tiles
    step = lambda b, i: b * n_tiles + i
    in_specs = [
        pl.BlockSpec((1, 1, 2 * tm), lambda b, i: (step(b, i), 0, 0), memory_space=pltpu.SMEM),
        pl.BlockSpec((1, 1, 2 * tm), lambda b, i: (jnp.minimum(step(b, i) + 1, total - 1), 0, 0),
                     memory_space=pltpu.SMEM),
        pl.BlockSpec((tm, LANES), lambda b, i: (step(b, i), 0)),
        pl.BlockSpec((None, tm, d), lambda b, i: (b, i, 0)),
        _mod_spec(l, 5, bsz, n_ctx // tm, tile0),
        _layer_vec(lng, l), _layer_vec(lnb, l),
        pl.BlockSpec(memory_space=pl.ANY),
    ]
    scratch = [pltpu.VMEM((2, 2, tm) + ys.shape[1:], ys.dtype), pltpu.SemaphoreType.DMA((2,))]
    return in_specs, scratch, (slots3, slots3, rec, x1, mods4, lng, lnb, ys)


def _combine(l, slots3, rec, x1, mods4, lng, lnb, ys, n_ctx, tile0, alpha):
    bsz, t_out, d = x1.shape
    tm = ROW_TILE
    in_specs, scratch, args = _combine_specs(l, slots3, rec, x1, mods4, lng, lnb, ys, n_ctx, tile0)
    return pl.pallas_call(
        functools.partial(_combine_kernel, alpha=alpha),
        grid=(bsz, t_out // tm),
        in_specs=in_specs,
        out_specs=pl.BlockSpec((None, tm, d), lambda b, i: (b, i, 0)),
        out_shape=jax.ShapeDtypeStruct((bsz, t_out, d), F32),
        scratch_shapes=scratch,
        compiler_params=pltpu.CompilerParams(dimension_semantics=("arbitrary", "arbitrary")),
        name="moe_combine_ln2",
    )(*args)


def _combine_inproj(l, slots3, rec, x1, mods4, lng, lnb, ys, tabs, w_in_bf, n_ctx, alpha):
    bsz, t_all, d = x1.shape
    tm = ROW_TILE
    in_specs, scratch, args = _combine_specs(l, slots3, rec, x1, mods4, lng, lnb, ys, n_ctx, 0)
    p_specs, p_out_specs, p_out_shape, p_args = _inproj_specs(l + 1, bsz, t_all, mods4, tabs, w_in_bf, n_ctx)
    return pl.pallas_call(
        functools.partial(_combine_inproj_kernel, alpha=alpha),
        grid=(bsz, t_all // tm),
        in_specs=in_specs + p_specs,
        out_specs=[pl.BlockSpec((None, tm, d), lambda b, i: (b, i, 0))] + p_out_specs,
        out_shape=[jax.ShapeDtypeStruct((bsz, t_all, d), F32)] + p_out_shape,
        scratch_shapes=scratch,
        compiler_params=pltpu.CompilerParams(
            dimension_semantics=("arbitrary", "arbitrary"), vmem_limit_bytes=VMEM_LIMIT),
        name="moe_combine_ln2_inproj",
    )(*args, *p_args)


def _rope_tables(n_ctx, n_lat):
    rows = n_lat // GRID_W
    row = jnp.repeat(jnp.arange(rows, dtype=F32), GRID_W)
    col = jnp.tile(jnp.arange(GRID_W, dtype=F32), rows)
    inv = 1.0 / (ROPE_BASE ** (jnp.arange(0, ROPE_AXIS_DIM, 2, dtype=F32) / ROPE_AXIS_DIM))
    half = ROPE_AXIS_DIM // 2
    zeros = jnp.zeros((n_lat, half), F32)
    cos, sa, sb = [], [], []
    for pos in (row, col):
        ang = pos[:, None] * inv
        c, s = jnp.cos(ang), jnp.sin(ang)
        cos += [c, c]
        sa += [-s, zeros]
        sb += [zeros, s]
    reps = LANES // HEAD_DIM

    def full(parts, ctx_fill):
        lat = jnp.tile(jnp.concatenate(parts, axis=1), (1, reps))
        return jnp.concatenate([jnp.full((n_ctx, LANES), ctx_fill, F32), lat], axis=0)

    return full(cos, 1.0), full(sa, 0.0), full(sb, 0.0)


def kernel(x, c, ctx, c_ctx, w_mod, b_mod, w_in, lam_q1, lam_k1, lam_q2, lam_k2, subln_g,
           ssm_a_re, ssm_a_im, ssm_log_dt, ssm_b_re, ssm_b_im, ssm_c_re, ssm_c_im, ssm_d,
           w_glu, b_glu, w_pa, w_ps, w_o, ln1_g, ln1_b,
           router_g_w, router_g_b, router_e_w, router_e_b, moe_w1, moe_w3, moe_w2, ln2_g, ln2_b):
    bsz, n_lat, d = x.shape
    n_ctx = ctx.shape[1]
    depth = w_mod.shape[0]
    assert d == D_MODEL and bsz % SUBLANES == 0
    assert n_ctx % ROW_TILE == 0 and n_lat % ROW_TILE == 0 and n_ctx % SCAN_TIME_TILE == 0
    alpha = (2.0 * depth) ** 0.25
    n_ctx_tiles = n_ctx // ROW_TILE

    mod_rows = -(-(bsz + 1) // SUBLANES) * SUBLANES
    cvec = jnp.concatenate([c, c_ctx[None, :], jnp.zeros((mod_rows - bsz - 1, d), F32)], axis=0)
    mods4 = _modulation(cvec, w_mod, b_mod).reshape(depth, mod_rows, 1, N_MOD * d)
    tabs = _rope_tables(n_ctx, n_lat)
    xall = jnp.concatenate([ctx, x], axis=1)
    vecs = lambda a: a.reshape(depth, 1, a.shape[-1])
    ops = jax.vmap(_ssm_tables)(ssm_a_re, ssm_a_im, ssm_log_dt, ssm_b_re, ssm_b_im, ssm_c_re, ssm_c_im, ssm_d)
    n_pad = LANES - N_GROUPS - N_EXPERTS
    wr = jnp.concatenate([router_g_w, router_e_w, jnp.zeros((depth, d, n_pad), F32)], axis=2)
    br = jnp.concatenate([router_g_b, router_e_b, jnp.zeros((depth, n_pad), F32)], axis=1)
    wr_hi = wr.astype(BF16)
    wr_split = jnp.stack([wr_hi, (wr - wr_hi.astype(F32)).astype(BF16)], axis=1)
    w_in_bf = w_in.astype(BF16)
    wts = (w_pa.astype(BF16), w_glu.astype(BF16), vecs(b_glu), w_ps.astype(BF16), w_o.astype(BF16),
           vecs(ln1_g), vecs(ln1_b), wr_split, vecs(br))
    lam_vecs = (vecs(lam_q1), vecs(lam_k1), vecs(lam_q2), vecs(lam_k2))
    ln2 = (vecs(ln2_g), vecs(ln2_b))
    subln = vecs(subln_g)

    proj = _inproj(0, xall, mods4, tabs, w_in_bf, n_ctx)
    for l in range(depth):
        last = l == depth - 1
        tile0 = n_ctx_tiles if last else 0
        lam_init = 0.8 - 0.6 * math.exp(-0.3 * l)
        k, v, u, q, siga, sigs = proj
        attn = _attention(l, q, k, v, lam_vecs, subln, n_ctx, tile0, lam_init)
        y = _ssm(l, u, ops, n_ctx)
        x1, h2, choices, counts = _merge(l, attn, y, siga, sigs, xall, mods4, wts, n_ctx, tile0, alpha)

        n_tok = h2.shape[0]
        n_blocks = -(-(2 * n_tok + N_EXPERTS * (MOE_BLOCK - 1)) // MOE_BLOCK)
        rec, blk = _route(choices, counts, n_blocks)
        n_tiles_tok = n_tok // ROW_TILE
        slots3 = (rec[:, 0:2].astype(jnp.int32).reshape(n_tiles_tok, ROW_TILE, 2)
                  .transpose(0, 2, 1).reshape(n_tiles_tok, 1, 2 * ROW_TILE))
        xs = _dispatch(h2, slots3, n_blocks * MOE_BLOCK)
        ys = _experts(l, blk[:n_blocks, 0], xs, moe_w1, moe_w3, moe_w2)
        if last:
            xall = _combine(l, slots3, rec, x1, mods4, ln2[0], ln2[1], ys, n_ctx, tile0, alpha)
        else:
            xall, *proj = _combine_inproj(l, slots3, rec, x1, mods4, ln2[0], ln2[1], ys, tabs, w_in_bf, n_ctx,
                                          alpha)
    return xall
```

```python
import functools
import math

import jax
import jax.numpy as jnp
from jax import lax
from jax.experimental import pallas as pl
from jax.experimental.pallas import tpu as pltpu

F32 = jnp.float32
BF16 = jnp.bfloat16

D_MODEL = 1024
N_HEADS = 8
HEAD_DIM = 64
V_DIM = 2 * HEAD_DIM
QK_W = N_HEADS * 2 * HEAD_DIM
ATTN_W = N_HEADS * V_DIM
SSM_W = D_MODEL // 2
SSM_GROUP = 16
SSM_GROUPS = SSM_W // SSM_GROUP
SSM_STATE = 64
N_GROUPS = 4
EXPERTS_PER_GROUP = 8
N_EXPERTS = N_GROUPS * EXPERTS_PER_GROUP
EXPERT_HIDDEN = D_MODEL // 2
KVU_W = QK_W + ATTN_W + SSM_W
IN_W = KVU_W + QK_W + 2 * D_MODEL
N_MOD = 6
GRID_W = 64
ROPE_BASE = 10000.0
ROPE_AXIS_DIM = HEAD_DIM // 2
LN_EPS = 1e-5

LANES = 128
SUBLANES = 8
VMEM_LIMIT = 56 * 1024 * 1024

ROW_TILE = 256
SCAN_CHUNK = 8
SCAN_TIME_TILE = 256
MOE_BLOCK = 256
ROUTE_TILE = 512
HEADS_PER_STEP = 4
SCORE_LOOKAHEAD = 2
GATHER_PIECES = 16
ISSUE_UNROLL = 8
LANE_TILES = SSM_W // LANES
GROUPS_PER_TILE = LANES // SSM_GROUP
STATE_W = GROUPS_PER_TILE * SSM_STATE
NEG_BIG = -3.0e38


def _sigmoid(x):
    return 1.0 / (1.0 + jnp.exp(-x))


def _pack_rows(x):
    w = x.shape[1] // 2
    bits = lax.bitcast_convert_type(x.astype(BF16).astype(F32), jnp.uint32)
    return bits[:, :w] | (bits[:, w:] >> 16)


def _unpack_rows(p):
    hi = lax.bitcast_convert_type(p & jnp.uint32(0xFFFF0000), F32)
    lo = lax.bitcast_convert_type(p << 16, F32)
    return jnp.concatenate([hi, lo], axis=1)


def _layer_norm(x, g, b):
    xc = x - jnp.mean(x, axis=-1, keepdims=True)
    var = jnp.mean(xc * xc, axis=-1, keepdims=True)
    return xc * lax.rsqrt(var + LN_EPS) * g + b


def _layer_vec(arr, l):
    return pl.BlockSpec((None, 1, arr.shape[-1]), lambda b, i: (l, 0, 0))


def _layer_mat(arr, l):
    return pl.BlockSpec((None,) + arr.shape[1:], lambda b, i: (l, 0, 0))


def _mod_spec(l, col, bsz, n_ctx_tiles, tile0):
    return pl.BlockSpec((None, None, 1, D_MODEL),
                        lambda b, i: (l, jnp.where(i + tile0 < n_ctx_tiles, bsz, b), 0, col))


def _mod_kernel(c_ref, w_ref, b_ref, o_ref):
    c = c_ref[...]
    s = c * _sigmoid(c)
    o_ref[...] = jnp.dot(s, w_ref[...], preferred_element_type=F32, precision=lax.Precision.HIGHEST) + b_ref[...]


def _modulation(cvec, w_mod, b_mod):
    depth, d, w6 = w_mod.shape
    rows = cvec.shape[0]
    tn = 1024
    return pl.pallas_call(
        _mod_kernel,
        grid=(depth, w6 // tn),
        in_specs=[
            pl.BlockSpec((rows, d), lambda l, j: (0, 0)),
            pl.BlockSpec((None, d, tn), lambda l, j: (l, 0, j)),
            pl.BlockSpec((None, 1, tn), lambda l, j: (l, 0, j)),
        ],
        out_specs=pl.BlockSpec((None, rows, tn), lambda l, j: (l, 0, j)),
        out_shape=jax.ShapeDtypeStruct((depth, rows, w6), F32),
        name="modulation",
    )(cvec, w_mod, b_mod.reshape(depth, 1, w6))


def _inproj_kernel(x_ref, sh_ref, sc_ref, cos_ref, sa_ref, sb_ref, w_ref,
                   k_ref, v_ref, u_ref, q_ref, ga_ref, gs_ref):
    _inproj_body(x_ref[...], sh_ref, sc_ref, cos_ref, sa_ref, sb_ref, w_ref,
                 k_ref, v_ref, u_ref, q_ref, ga_ref, gs_ref)


def _inproj_body(x, sh_ref, sc_ref, cos_ref, sa_ref, sb_ref, w_ref, k_ref, v_ref, u_ref, q_ref, ga_ref, gs_ref,
                 between=None):
    def stage_done():
        if between is not None:
            piece = next(between, None)
            if piece is not None:
                piece()

    h = (x * (1.0 + sc_ref[...]) + sh_ref[...]).astype(BF16)
    cos, sa, sb = cos_ref[...], sa_ref[...], sb_ref[...]

    def rope(t):
        return t * cos + pltpu.roll(t, LANES - 16, 1) * sa + pltpu.roll(t, 16, 1) * sb

    def proj(lo, hi):
        return jnp.dot(h, w_ref[:, lo:hi], preferred_element_type=F32)

    kk = proj(0, QK_W)
    for c in range(QK_W // LANES):
        k_ref[:, c * LANES:(c + 1) * LANES] = rope(kk[:, c * LANES:(c + 1) * LANES]).astype(BF16)
        stage_done()
    v_ref[...] = proj(QK_W, QK_W + ATTN_W).astype(BF16)
    stage_done()
    u_ref[...] = proj(QK_W + ATTN_W, KVU_W)
    stage_done()
    qq = proj(KVU_W, KVU_W + QK_W)
    scale = HEAD_DIM ** -0.5 * math.log2(math.e)
    for c in range(QK_W // LANES):
        q_ref[:, c * LANES:(c + 1) * LANES] = (rope(qq[:, c * LANES:(c + 1) * LANES]) * scale).astype(BF16)
        stage_done()
    ga_ref[...] = _sigmoid(proj(KVU_W + QK_W, KVU_W + QK_W + D_MODEL)).astype(BF16)
    stage_done()
    gs_ref[...] = _sigmoid(proj(KVU_W + QK_W + D_MODEL, IN_W)).astype(BF16)
    if between is not None:
        for piece in between:
            piece()


def _inproj_specs(l, bsz, t_all, mods4, tabs, w_in_bf, n_ctx):
    tm = ROW_TILE
    n_ctx_tiles = n_ctx // tm
    tok_spec = lambda w: pl.BlockSpec((None, tm, w), lambda b, i: (b, i, 0))
    tab_spec = pl.BlockSpec((tm, LANES), lambda b, i: (i, 0))
    big = lambda w, dt: jax.ShapeDtypeStruct((bsz, t_all, w), dt)
    in_specs = [_mod_spec(l, 0, bsz, n_ctx_tiles, 0), _mod_spec(l, 1, bsz, n_ctx_tiles, 0),
                tab_spec, tab_spec, tab_spec, _layer_mat(w_in_bf, l)]
    out_specs = [tok_spec(QK_W), tok_spec(ATTN_W), tok_spec(SSM_W),
                 tok_spec(QK_W), tok_spec(D_MODEL), tok_spec(D_MODEL)]
    out_shape = [big(QK_W, BF16), big(ATTN_W, BF16), big(SSM_W, F32),
                 big(QK_W, BF16), big(D_MODEL, BF16), big(D_MODEL, BF16)]
    return in_specs, out_specs, out_shape, (mods4, mods4, tabs[0], tabs[1], tabs[2], w_in_bf)


def _inproj(l, xall, mods4, tabs, w_in_bf, n_ctx):
    bsz, t_all, d = xall.shape
    tm = ROW_TILE
    in_specs, out_specs, out_shape, args = _inproj_specs(l, bsz, t_all, mods4, tabs, w_in_bf, n_ctx)
    return pl.pallas_call(
        _inproj_kernel,
        grid=(bsz, t_all // tm),
        in_specs=[pl.BlockSpec((None, tm, d), lambda b, i: (b, i, 0))] + in_specs,
        out_specs=out_specs,
        out_shape=out_shape,
        compiler_params=pltpu.CompilerParams(
            dimension_semantics=("parallel", "arbitrary"), vmem_limit_bytes=VMEM_LIMIT),
        name="inproj",
    )(xall, *args)


def _attn_kernel(q_ref, k_ref, v_ref, lq1_ref, lk1_ref, lq2_ref, lk2_ref, g_ref, o_ref,
                 *, n_ctx, n_ctx_tiles, tile0, lam_init):
    i = pl.program_id(2) + tile0
    lam = (jnp.exp(jnp.sum(lq1_ref[...] * lk1_ref[...], axis=1, keepdims=True))
           - jnp.exp(jnp.sum(lq2_ref[...] * lk2_ref[...], axis=1, keepdims=True)) + lam_init)
    g = g_ref[...]
    lane = lax.broadcasted_iota(jnp.int32, (q_ref.shape[0], V_DIM), 1)

    def scores(hh, n_kv):
        cols = slice(hh * V_DIM, (hh + 1) * V_DIM)
        q = q_ref[:, cols]
        zero = jnp.zeros_like(q)
        k = k_ref[0:n_kv, cols]
        nt = (((1,), (1,)), ((), ()))
        s1 = lax.dot_general(jnp.where(lane < HEAD_DIM, q, zero), k, nt, preferred_element_type=F32)
        s2 = lax.dot_general(jnp.where(lane < HEAD_DIM, zero, q), k, nt, preferred_element_type=F32)
        return s1, s2

    def finish(hh, n_kv, s1, s2):
        cols = slice(hh * V_DIM, (hh + 1) * V_DIM)

        def probs(s):
            p = jnp.exp2(s - jnp.max(s, axis=-1, keepdims=True))
            return p, 1.0 / jnp.sum(p, axis=-1, keepdims=True)

        p1, r1 = probs(s1)
        p2, r2 = probs(s2)
        a = p1 - p2 * (lam * r2 / r1)
        o = jnp.dot(a.astype(BF16), v_ref[0:n_kv, cols], preferred_element_type=F32) * r1
        o = o * lax.rsqrt(jnp.mean(o * o, axis=-1, keepdims=True) + LN_EPS) * g * (1.0 - lam_init)
        o_ref[:, cols] = o.astype(BF16)

    def attend(n_kv):
        pending = []
        for hh in range(HEADS_PER_STEP):
            pending.append((hh, scores(hh, n_kv)))
            if len(pending) > SCORE_LOOKAHEAD:
                h0, s0 = pending.pop(0)
                finish(h0, n_kv, *s0)
        for h0, s0 in pending:
            finish(h0, n_kv, *s0)

    if n_ctx_tiles > tile0:
        @pl.when(i < n_ctx_tiles)
        def _():
            attend(n_ctx)

        @pl.when(i >= n_ctx_tiles)
        def _():
            attend(k_ref.shape[0])
    else:
        attend(k_ref.shape[0])


def _attention(l, q, k, v, lam_vecs, subln_g, n_ctx, tile0, lam_init):
    bsz, t_all, _ = q.shape
    tq = ROW_TILE
    hw = HEADS_PER_STEP * V_DIM
    n_tiles = t_all // tq - tile0
    vec = lambda arr: pl.BlockSpec((None, 1, arr.shape[-1]), lambda b, h, i: (l, 0, 0))
    kern = functools.partial(_attn_kernel, n_ctx=n_ctx, n_ctx_tiles=n_ctx // tq, tile0=tile0, lam_init=lam_init)
    return pl.pallas_call(
        kern,
        grid=(bsz, N_HEADS // HEADS_PER_STEP, n_tiles),
        in_specs=[
            pl.BlockSpec((None, tq, hw), lambda b, h, i: (b, i + tile0, h)),
            pl.BlockSpec((None, t_all, hw), lambda b, h, i: (b, 0, h)),
            pl.BlockSpec((None, t_all, hw), lambda b, h, i: (b, 0, h)),
            vec(lam_vecs[0]), vec(lam_vecs[1]), vec(lam_vecs[2]), vec(lam_vecs[3]), vec(subln_g),
        ],
        out_specs=pl.BlockSpec((None, tq, hw), lambda b, h, i: (b, i, h)),
        out_shape=jax.ShapeDtypeStruct((bsz, n_tiles * tq, ATTN_W), BF16),
        compiler_params=pltpu.CompilerParams(
            dimension_semantics=("parallel", "parallel", "arbitrary"), vmem_limit_bytes=VMEM_LIMIT),
        name="diff_attention",
    )(q, k, v, *lam_vecs, subln_g)


def _ssm_tables(a_re, a_im, log_dt, b_re, b_im, c_re, c_im, d_skip):
    lc = SCAN_CHUNK
    lam = lax.complex(a_re.astype(F32), a_im.astype(F32))
    dt = jnp.exp(log_dt.astype(F32))[..., None]
    ldt = lam * dt
    a_bar = jnp.exp(ldt)
    b_bar = ((a_bar - 1.0) / lam)[..., None] * lax.complex(b_re.astype(F32), b_im.astype(F32))
    cm = lax.complex(c_re.astype(F32), c_im.astype(F32))
    steps = jnp.arange(lc + 1, dtype=F32)
    apow = jnp.exp(ldt[None] * steps[:, None, None, None])
    s_idx = jnp.arange(lc)
    lag_f = s_idx[None, :] - s_idx[:, None]
    lk = lc * LANES
    eye_c = jnp.eye(SSM_GROUP, dtype=F32)
    eye_t = jnp.eye(lc, dtype=F32)
    d_g = d_skip.astype(F32).reshape(SSM_GROUPS, SSM_GROUP)

    def rows_of(t):
        t = t.reshape(lc, LANE_TILES, GROUPS_PER_TILE, SSM_GROUP, LANES)
        return t.transpose(1, 0, 2, 3, 4).reshape(LANE_TILES, lk, LANES)

    inject, readout, intra = [], [], []
    for di in range(2):
        pw_in = apow[lc - 1 - s_idx, di] if di == 0 else apow[s_idx, di]
        w = jnp.einsum('sgp,gpi->sgip', pw_in, b_bar[di])
        inject.append(rows_of(jnp.concatenate([w.real, w.imag], axis=-1)))
        pw_out = apow[s_idx + 1, di] if di == 0 else apow[lc - s_idx, di]
        vv = cm[di][None] * pw_out[:, :, None, :]
        readout.append(rows_of(jnp.concatenate([vv.real, -vv.imag], axis=-1)))
        kern = jnp.einsum('gcp,jgp,gpi->jgci', cm[di], apow[:lc, di], b_bar[di]).real
        lag = lag_f if di == 0 else -lag_f
        toe = jnp.where((lag >= 0)[:, :, None, None, None], kern[jnp.clip(lag, 0, lc - 1)], 0.0)
        toe = toe.transpose(0, 2, 4, 1, 3)
        if di == 0:
            toe = toe + jnp.einsum('st,gi,ic->sgitc', eye_t, d_g, eye_c)
        intra.append(rows_of(toe.reshape(lc, SSM_GROUPS, SSM_GROUP, LANES)))
    al = apow[lc].reshape(2, LANE_TILES, 1, STATE_W)
    return jnp.stack(inject), jnp.stack(readout), jnp.stack(intra), al.real, al.imag


def _ssm_kernel(u_ref, wi_ref, wo_ref, wk_ref, alr_ref, ali_ref, y_ref,
                bc_scr, wct_scr, a_scr, s_scr, h_scr, st_scr):
    ph = pl.program_id(1)
    ti = pl.program_id(2)
    lc = SCAN_CHUNK
    bsz = u_ref.shape[0]
    n_chunks = u_ref.shape[1] // lc
    n_rows = bsz * n_chunks
    lk = lc * LANES

    @pl.when(ti == 0)
    def _():
        st_scr[...] = jnp.zeros_like(st_scr)
        q = lax.broadcasted_iota(jnp.int32, (LANES, lk), 0)
        c = lax.broadcasted_iota(jnp.int32, (LANES, lk), 1)
        sel_state = (q == ((c >> 9) << 6) + (c & (SSM_STATE - 1))).astype(BF16)
        sel_tok = (q == ((c >> 7) << 4) + (c & (SSM_GROUP - 1))).astype(BF16)
        row_g = (lax.broadcasted_iota(jnp.int32, (lk, lk), 0) >> 4) & (GROUPS_PER_TILE - 1)
        col = lax.broadcasted_iota(jnp.int32, (lk, lk), 1)
        same_state = row_g == ((col >> 6) & (GROUPS_PER_TILE - 1))
        same_tok = row_g == ((col >> 4) & (GROUPS_PER_TILE - 1))

        def spread(tab_ref, sel, same):
            full = jnp.dot(tab_ref[...].astype(BF16), sel, preferred_element_type=F32)
            return jnp.where(same, full, 0.0).astype(BF16)

        bc_scr[...] = spread(wi_ref, sel_state, same_state)
        wct_scr[...] = spread(wo_ref, sel_state, same_state)
        a_scr[...] = spread(wk_ref, sel_tok, same_tok)

    x = jnp.concatenate(
        [jnp.concatenate([u_ref[b, pl.ds(s, n_chunks, stride=lc), :] for s in range(lc)], axis=1)
         for b in range(bsz)], axis=0).astype(BF16)
    r = lax.broadcasted_iota(jnp.int32, (n_rows, n_rows), 0)
    cc = lax.broadcasted_iota(jnp.int32, (n_rows, n_rows), 1)
    sh_b, sh_c = bsz.bit_length() - 1, n_chunks.bit_length() - 1
    to_cb = (cc == ((r & (bsz - 1)) << sh_c) + (r >> sh_b)).astype(BF16)
    to_bc = (cc == ((r & (n_chunks - 1)) << sh_b) + (r >> sh_c)).astype(BF16)
    x_cb = jnp.dot(to_cb, x, preferred_element_type=F32).astype(BF16)
    s_scr[...] = jnp.dot(x_cb, bc_scr[...], preferred_element_type=F32)
    alr, ali = alr_ref[...], ali_ref[...]

    def step(c, carry):
        hr, hi = carry
        ce = jnp.where(ph == 0, c, n_chunks - 1 - c)
        rows = pl.ds(pl.multiple_of(ce * bsz, bsz), bsz)
        h_scr[rows, 0:STATE_W] = hr
        h_scr[rows, STATE_W:2 * STATE_W] = hi
        sr = s_scr[rows, 0:STATE_W]
        si = s_scr[rows, STATE_W:2 * STATE_W]
        return alr * hr - ali * hi + sr, alr * hi + ali * hr + si

    hr, hi = lax.fori_loop(0, n_chunks, step, (st_scr[0], st_scr[1]))
    st_scr[0] = hr
    st_scr[1] = hi
    h_bc = jnp.dot(to_bc, h_scr[...].astype(BF16), preferred_element_type=F32).astype(BF16)
    y = (jnp.dot(x, a_scr[...], preferred_element_type=F32)
         + lax.dot_general(h_bc, wct_scr[...], (((1,), (1,)), ((), ())), preferred_element_type=F32))
    for b in range(bsz):
        for s in range(lc):
            y_ref[b, pl.ds(s, n_chunks, stride=lc), :] = y[b * n_chunks:(b + 1) * n_chunks,
                                                           s * LANES:(s + 1) * LANES]


def _ssm(l, u, tables, n_ctx):
    bsz, t_all, _ = u.shape
    lc = SCAN_CHUNK
    tt = SCAN_TIME_TILE
    n_t = t_all // tt
    n_ctx_t = n_ctx // tt
    rows = bsz * tt // lc
    inject, readout, intra, alr, ali = tables
    lk = lc * LANES
    assert 2 * STATE_W == lk and SSM_STATE == 64 and SSM_GROUP == 16 and LANES == 128
    assert bsz & (bsz - 1) == 0 and (tt // lc) & (tt // lc - 1) == 0

    def tile_of(ph, i):
        rev = jnp.where(i < n_ctx_t, n_ctx_t - 1 - i, n_t - 1 - (i - n_ctx_t))
        return jnp.where(ph == 0, i, rev)

    op_spec = lambda r, c: pl.BlockSpec((None, None, None, r, c), lambda j, ph, i: (l, ph, j, 0, 0))
    return pl.pallas_call(
        _ssm_kernel,
        grid=(LANE_TILES, 2, n_t),
        in_specs=[
            pl.BlockSpec((bsz, tt, LANES), lambda j, ph, i: (0, tile_of(ph, i), j)),
            op_spec(lk, LANES), op_spec(lk, LANES), op_spec(lk, LANES),
            op_spec(1, STATE_W), op_spec(1, STATE_W),
        ],
        out_specs=pl.BlockSpec((None, bsz, tt, LANES), lambda j, ph, i: (ph, 0, tile_of(ph, i), j)),
        out_shape=jax.ShapeDtypeStruct((2, bsz, t_all, SSM_W), F32),
        scratch_shapes=[
            pltpu.VMEM((lk, 2 * STATE_W), BF16),
            pltpu.VMEM((lk, 2 * STATE_W), BF16),
            pltpu.VMEM((lk, lk), BF16),
            pltpu.VMEM((rows, 2 * STATE_W), F32),
            pltpu.VMEM((rows, 2 * STATE_W), F32),
            pltpu.VMEM((2, bsz, STATE_W), F32),
        ],
        compiler_params=pltpu.CompilerParams(
            dimension_semantics=("parallel", "arbitrary", "arbitrary"), vmem_limit_bytes=VMEM_LIMIT),
        name="s5_scan",
    )(u, inject, readout, intra, alr, ali)


def _merge_kernel(attn_ref, y_ref, ga_ref, gs_ref, x_ref, g1_ref, sh2_ref, sc2_ref,
                  wpa_ref, wglu_ref, bglu_ref, wps_ref, wo_ref, lng_ref, lnb_ref, wr_ref, br_ref,
                  x1_ref, h2_ref, ch_ref, cnt_ref, *, alpha):
    a = jnp.dot(attn_ref[...], wpa_ref[...], preferred_element_type=F32)
    ys = y_ref[0] + y_ref[1]
    gl = ys * (0.5 * (1.0 + jnp.tanh(math.sqrt(2.0 / math.pi) * (ys + 0.044715 * (ys * ys * ys)))))
    z = jnp.dot(gl.astype(BF16), wglu_ref[...], preferred_element_type=F32) + bglu_ref[...]
    sg = gl * _sigmoid(z)
    s = jnp.dot(sg.astype(BF16), wps_ref[...], preferred_element_type=F32)
    m = ga_ref[...].astype(F32) * a + gs_ref[...].astype(F32) * s
    y = jnp.dot(m.astype(BF16), wo_ref[...], preferred_element_type=F32)
    x1 = _layer_norm(alpha * x_ref[...] + g1_ref[...] * y, lng_ref[...], lnb_ref[...])
    x1_ref[...] = x1
    h2 = x1 * (1.0 + sc2_ref[...]) + sh2_ref[...]
    h2_ref[...] = _pack_rows(h2)
    h_hi = h2.astype(BF16)
    h_lo = (h2 - h_hi.astype(F32)).astype(BF16)
    w_hi, w_lo = wr_ref[0], wr_ref[1]
    lg = (jnp.dot(h_hi, w_hi, preferred_element_type=F32)
          + (jnp.dot(h_lo, w_hi, preferred_element_type=F32)
             + jnp.dot(h_hi, w_lo, preferred_element_type=F32))) + br_ref[...]
    e1, e2, w1, w2 = _route_choices(lg)
    lane = lax.broadcasted_iota(jnp.int32, lg.shape, 1).astype(F32)
    ch_ref[...] = _record(lane, e1, e2, w1, w2)
    rows_per_expert = jnp.sum((lane == e1).astype(F32) + (lane == e2).astype(F32), axis=0, keepdims=True)

    @pl.when((pl.program_id(0) == 0) & (pl.program_id(1) == 0))
    def _():
        cnt_ref[...] = jnp.zeros_like(cnt_ref)

    cnt_ref[...] += jnp.broadcast_to(rows_per_expert, cnt_ref.shape)


def _merge(l, attn, y, siga, sigs, xall, mods4, wts, n_ctx, tile0, alpha):
    bsz, t_all, d = xall.shape
    tm = ROW_TILE
    n_ctx_tiles = n_ctx // tm
    n_tiles = t_all // tm - tile0
    t_out = n_tiles * tm
    tok = lambda w: pl.BlockSpec((None, tm, w), lambda b, i: (b, i + tile0, 0))
    own = lambda w: pl.BlockSpec((None, tm, w), lambda b, i: (b, i, 0))
    modv = lambda col: _mod_spec(l, col, bsz, n_ctx_tiles, tile0)
    wpa, wglu, bglu, wps, wo, lng, lnb, wr, br = wts
    kern = functools.partial(_merge_kernel, alpha=alpha)
    return pl.pallas_call(
        kern,
        grid=(bsz, n_tiles),
        in_specs=[
            own(ATTN_W),
            pl.BlockSpec((2, None, tm, SSM_W), lambda b, i: (0, b, i + tile0, 0)),
            tok(D_MODEL), tok(D_MODEL), tok(d),
            modv(2), modv(3), modv(4),
            _layer_mat(wpa, l), _layer_mat(wglu, l), _layer_vec(bglu, l), _layer_mat(wps, l), _layer_mat(wo, l),
            _layer_vec(lng, l), _layer_vec(lnb, l),
            pl.BlockSpec((None,) + wr.shape[1:], lambda b, i: (l, 0, 0, 0)), _layer_vec(br, l),
        ],
        out_specs=[
            own(d),
            pl.BlockSpec((tm, d // 2), lambda b, i: (b * n_tiles + i, 0)),
            pl.BlockSpec((tm, LANES), lambda b, i: (b * n_tiles + i, 0)),
            pl.BlockSpec((SUBLANES, LANES), lambda b, i: (0, 0)),
        ],
        out_shape=[
            jax.ShapeDtypeStruct((bsz, t_out, d), F32),
            jax.ShapeDtypeStruct((bsz * t_out, d // 2), jnp.uint32),
            jax.ShapeDtypeStruct((bsz * t_out, LANES), F32),
            jax.ShapeDtypeStruct((SUBLANES, LANES), F32),
        ],
        compiler_params=pltpu.CompilerParams(
            dimension_semantics=("arbitrary", "arbitrary"), vmem_limit_bytes=VMEM_LIMIT),
        name="merge_ln1",
    )(attn, y, siga, sigs, xall, mods4, mods4, mods4, wpa, wglu, bglu, wps, wo, lng, lnb, wr, br)


def _route_choices(lg):
    shape = lg.shape
    lane = lax.broadcasted_iota(jnp.int32, shape, 1).astype(F32)
    far = jnp.full(shape, 1.0e9, F32)

    def first_max(vals, mask):
        vm = jnp.where(mask, vals, NEG_BIG)
        mx = jnp.max(vm, axis=1, keepdims=True)
        idx = jnp.min(jnp.where(mask & (vm == mx), lane, far), axis=1, keepdims=True)
        return mx, idx

    gmask = lane < N_GROUPS
    gmax, gidx = first_max(lg, gmask)
    gtop = 1.0 / jnp.sum(jnp.where(gmask, jnp.exp(lg - gmax), 0.0), axis=1, keepdims=True)
    lo = N_GROUPS + EXPERTS_PER_GROUP * gidx
    emask = (lane >= lo) & (lane < lo + EXPERTS_PER_GROUP)
    v1, i1 = first_max(lg, emask)
    v2, i2 = first_max(lg, emask & (lane != i1))
    e2 = jnp.exp(v2 - v1)
    den = 1.0 + e2
    return i1 - N_GROUPS, i2 - N_GROUPS, (1.0 / den) * gtop, (e2 / den) * gtop


def _record(lane, a, b, c, d):
    return jnp.where(lane == 0.0, a, jnp.where(lane == 1.0, b, jnp.where(lane == 2.0, c, d)))


def _route_kernel(ch_ref, cnt_ref, rec_ref, blk_ref, start_scr, carry_scr):
    i = pl.program_id(0)
    ch = ch_ref[...]
    shape = ch.shape
    lane = lax.broadcasted_iota(jnp.int32, shape, 1).astype(F32)
    oh1 = (lane == ch[:, 0:1]).astype(F32)
    oh2 = (lane == ch[:, 1:2]).astype(F32)
    c1 = jnp.sum(oh1, axis=0, keepdims=True)
    c2 = jnp.sum(oh2, axis=0, keepdims=True)

    @pl.when(i == 0)
    def _():
        cnt = cnt_ref[0:1, :]
        padded = jnp.floor((cnt + (MOE_BLOCK - 1)) * (1.0 / MOE_BLOCK)) * MOE_BLOCK
        r = lax.broadcasted_iota(jnp.int32, (LANES, LANES), 0)
        c = lax.broadcasted_iota(jnp.int32, (LANES, LANES), 1)
        upper = (r < c).astype(F32)
        start = jnp.dot(jnp.broadcast_to(padded, (SUBLANES, LANES)), upper, preferred_element_type=F32,
                        precision=lax.Precision.HIGHEST)[0:1]
        start_scr[...] = start
        carry_scr[...] = jnp.zeros_like(carry_scr)
        nb = blk_ref.shape[0]
        blk_start = (lax.broadcasted_iota(jnp.int32, (nb, LANES), 0) * MOE_BLOCK).astype(F32)
        elane = lax.broadcasted_iota(jnp.int32, (nb, LANES), 1) < N_EXPERTS
        done = jnp.sum(jnp.where(elane & ((start + padded) <= blk_start), 1.0, 0.0), axis=1, keepdims=True)
        blk_ref[...] = jnp.broadcast_to(jnp.minimum(done, N_EXPERTS - 1.0), (nb, LANES)).astype(jnp.int32)

    tr = shape[0]
    r = lax.broadcasted_iota(jnp.int32, (tr, tr), 0)
    c = lax.broadcasted_iota(jnp.int32, (tr, tr), 1)
    tri = (c < r).astype(BF16)
    base = start_scr[...] + carry_scr[...]
    r1 = jnp.dot(tri, oh1.astype(BF16), preferred_element_type=F32)
    r2 = jnp.dot(tri, oh2.astype(BF16), preferred_element_type=F32) + c1
    d1 = jnp.sum(oh1 * (base + r1), axis=1, keepdims=True)
    d2 = jnp.sum(oh2 * (base + r2), axis=1, keepdims=True)
    carry_scr[...] += c1 + c2
    rec_ref[...] = _record(lane, d1, d2, ch[:, 2:3], ch[:, 3:4])


def _route(choices, counts, n_blocks):
    n_tok = choices.shape[0]
    tr = ROUTE_TILE
    nb_pad = -(-n_blocks // SUBLANES) * SUBLANES
    return pl.pallas_call(
        _route_kernel,
        grid=(n_tok // tr,),
        in_specs=[pl.BlockSpec((tr, LANES), lambda i: (i, 0)),
                  pl.BlockSpec(counts.shape, lambda i: (0, 0))],
        out_specs=[
            pl.BlockSpec((tr, LANES), lambda i: (i, 0)),
            pl.BlockSpec((nb_pad, LANES), lambda i: (0, 0)),
        ],
        out_shape=[
            jax.ShapeDtypeStruct((n_tok, LANES), F32),
            jax.ShapeDtypeStruct((nb_pad, LANES), jnp.int32),
        ],
        scratch_shapes=[pltpu.VMEM((1, LANES), F32)] * 2,
        compiler_params=pltpu.CompilerParams(dimension_semantics=("arbitrary",)),
        name="moe_route",
    )(choices, counts)


def _row_copy(src, dst, sem):
    return pltpu.make_async_copy(src, dst, sem)


def _dispatch_kernel(slot_ref, h_ref, xs_in_ref, xs_ref, sem):
    del xs_in_ref
    tg = h_ref.shape[0]

    for r in range(tg):
        for k in range(2):
            d = slot_ref[0, 0, k * tg + r]
            _row_copy(h_ref.at[pl.ds(r, 1)], xs_ref.at[pl.ds(d, 1)], sem).start(priority=k)

    def drain(r, carry):
        for k in range(2):
            _row_copy(h_ref.at[pl.ds(0, 1)], xs_ref.at[pl.ds(0, 1)], sem).wait()
        return carry

    lax.fori_loop(0, tg, drain, 0, unroll=ISSUE_UNROLL)


def _dispatch(h2, slots3, cap):
    n_tok, d = h2.shape
    tg = ROW_TILE
    xs0 = jnp.zeros((cap, d), h2.dtype)
    return pl.pallas_call(
        _dispatch_kernel,
        grid=(n_tok // tg,),
        in_specs=[
            pl.BlockSpec((1, 1, 2 * tg), lambda i: (i, 0, 0), memory_space=pltpu.SMEM),
            pl.BlockSpec((tg, d), lambda i: (i, 0)),
            pl.BlockSpec(memory_space=pl.ANY),
        ],
        out_specs=pl.BlockSpec(memory_space=pl.ANY),
        out_shape=jax.ShapeDtypeStruct((cap, d), h2.dtype),
        scratch_shapes=[pltpu.SemaphoreType.DMA(())],
        input_output_aliases={2: 0},
        compiler_params=pltpu.CompilerParams(dimension_semantics=("arbitrary",), has_side_effects=True),
        name="moe_dispatch",
    )(slots3, h2, xs0)


def _expert_kernel(be_ref, xs_ref, w1_hbm, w3_hbm, w2_hbm, ys_ref,
                   w1f, w3f, w2f, w1b, w3b, w2b, sem, slot_scr, *, layer):
    i = pl.program_id(0)
    n = pl.num_programs(0)
    e = be_ref[i]
    fresh = (i == 0) | (e != be_ref[jnp.maximum(i - 1, 0)])

    def fetch(expert, slot):
        return [_row_copy(w1_hbm.at[layer, expert], w1f.at[slot], sem.at[slot]),
                _row_copy(w3_hbm.at[layer, expert], w3f.at[slot], sem.at[slot]),
                _row_copy(w2_hbm.at[layer, expert], w2f.at[slot], sem.at[slot])]

    @pl.when(i == 0)
    def _():
        slot_scr[0] = 0
        for c in fetch(e, 0):
            c.start()

    @pl.when(fresh)
    def _():
        slot = slot_scr[0]
        for c in fetch(e, slot):
            c.wait()
        nxt = lax.while_loop(lambda j: (j < n) & (be_ref[jnp.minimum(j, n - 1)] == e), lambda j: j + 1, i + 1)

        @pl.when(nxt < n)
        def _():
            for c in fetch(be_ref[jnp.minimum(nxt, n - 1)], 1 - slot):
                c.start()

        w1b[...] = w1f[slot].astype(BF16)
        w3b[...] = w3f[slot].astype(BF16)
        w2b[...] = w2f[slot].astype(BF16)
        slot_scr[0] = 1 - slot

    x = _unpack_rows(xs_ref[...]).astype(BF16)
    a = jnp.dot(x, w1b[...], preferred_element_type=F32)
    b = jnp.dot(x, w3b[...], preferred_element_type=F32)
    hid = (a * _sigmoid(a)) * b
    ys_ref[...] = _pack_rows(jnp.dot(hid.astype(BF16), w2b[...], preferred_element_type=F32))


def _experts(l, blk_expert, xs, w1, w3, w2):
    cap, dp = xs.shape
    d, f = w1.shape[-2:]
    n_blocks = cap // MOE_BLOCK
    rows = pl.BlockSpec((MOE_BLOCK, dp), lambda i, be: (i, 0))
    hbm = pl.BlockSpec(memory_space=pl.ANY)
    grid_spec = pltpu.PrefetchScalarGridSpec(
        num_scalar_prefetch=1,
        grid=(n_blocks,),
        in_specs=[rows, hbm, hbm, hbm],
        out_specs=pl.BlockSpec((MOE_BLOCK, dp), lambda i, be: (i, 0)),
        scratch_shapes=[pltpu.VMEM((2, d, f), F32), pltpu.VMEM((2, d, f), F32), pltpu.VMEM((2, f, d), F32),
                        pltpu.VMEM((d, f), BF16), pltpu.VMEM((d, f), BF16), pltpu.VMEM((f, d), BF16),
                        pltpu.SemaphoreType.DMA((2,)), pltpu.SMEM((1,), jnp.int32)],
    )
    return pl.pallas_call(
        functools.partial(_expert_kernel, layer=l),
        grid_spec=grid_spec,
        out_shape=jax.ShapeDtypeStruct((cap, dp), xs.dtype),
        compiler_params=pltpu.CompilerParams(
            dimension_semantics=("arbitrary",), vmem_limit_bytes=VMEM_LIMIT),
        name="moe_experts",
    )(blk_expert, xs, w1, w3, w2)


def _gather_expert_rows(slot_ref, slot_nxt_ref, ys_ref, buf, sem):
    tm = buf.shape[2]
    g = pl.program_id(0) * pl.num_programs(1) + pl.program_id(1)
    cur = g & 1

    def issue(ref, slot, rows):
        for r in rows:
            for k in range(2):
                d = ref[0, 0, k * tm + r]
                _row_copy(ys_ref.at[pl.ds(d, 1)], buf.at[slot, k, pl.ds(r, 1)], sem.at[slot]).start(priority=k)

    @pl.when(g == 0)
    def _():
        issue(slot_ref, 0, range(tm))

    _wait_gathers(ys_ref, buf, sem, cur)
    per = tm // GATHER_PIECES
    pieces = [functools.partial(issue, slot_nxt_ref, 1 - cur, range(p * per, (p + 1) * per))
              for p in range(GATHER_PIECES)]
    return buf[cur, 0], buf[cur, 1], pieces


def _wait_gathers(ys_ref, buf, sem, slot):
    tm = buf.shape[2]

    def drain(r, carry):
        for k in range(2):
            _row_copy(ys_ref.at[pl.ds(0, 1)], buf.at[slot, 0, pl.ds(0, 1)], sem.at[slot]).wait()
        return carry

    lax.fori_loop(0, tm, drain, 0, unroll=ISSUE_UNROLL)


def _finish_gathers(ys_ref, buf, sem):
    g = pl.program_id(0) * pl.num_programs(1) + pl.program_id(1)

    @pl.when(g == pl.num_programs(0) * pl.num_programs(1) - 1)
    def _():
        _wait_gathers(ys_ref, buf, sem, 1 - (g & 1))


def _combine_rows(rec, rows1, rows2, x1, g2, lng, lnb, alpha):
    f = rec[:, 2:3] * _unpack_rows(rows1) + rec[:, 3:4] * _unpack_rows(rows2)
    return _layer_norm(alpha * x1 + g2 * f, lng, lnb)


def _combine_kernel(slot_ref, slot_nxt_ref, rec_ref, x_ref, g2_ref, lng_ref, lnb_ref, ys_ref, o_ref,
                    buf, sem, *, alpha):
    rows1, rows2, pieces = _gather_expert_rows(slot_ref, slot_nxt_ref, ys_ref, buf, sem)
    for piece in pieces:
        piece()
    o_ref[...] = _combine_rows(rec_ref[...], rows1, rows2, x_ref[...], g2_ref[...], lng_ref[...], lnb_ref[...],
                               alpha)
    _finish_gathers(ys_ref, buf, sem)


def _combine_inproj_kernel(slot_ref, slot_nxt_ref, rec_ref, x_ref, g2_ref, lng_ref, lnb_ref, ys_ref,
                           sh_ref, sc_ref, cos_ref, sa_ref, sb_ref, w_ref,
                           x2_ref, k_ref, v_ref, u_ref, q_ref, ga_ref, gs_ref, buf, sem, *, alpha):
    rows1, rows2, pieces = _gather_expert_rows(slot_ref, slot_nxt_ref, ys_ref, buf, sem)
    x2 = _combine_rows(rec_ref[...], rows1, rows2, x_ref[...], g2_ref[...], lng_ref[...], lnb_ref[...], alpha)
    x2_ref[...] = x2
    _inproj_body(x2, sh_ref, sc_ref, cos_ref, sa_ref, sb_ref, w_ref, k_ref, v_ref, u_ref, q_ref, ga_ref, gs_ref,
                 between=iter(pieces))
    _finish_gathers(ys_ref, buf, sem)


def _combine_specs(l, slots3, rec, x1, mods4, lng, lnb, ys, n_ctx, tile0):
    bsz, t_out, d = x1.shape
    tm = ROW_TILE
    n_tiles = t_out // tm
    total = bsz * n_tiles
    step = lambda b, i: b * n_tiles + i
    in_specs = [
        pl.BlockSpec((1, 1, 2 * tm), lambda b, i: (step(b, i), 0, 0), memory_space=pltpu.SMEM),
        pl.BlockSpec((1, 1, 2 * tm), lambda b, i: (jnp.minimum(step(b, i) + 1, total - 1), 0, 0),
                     memory_space=pltpu.SMEM),
        pl.BlockSpec((tm, LANES), lambda b, i: (step(b, i), 0)),
        pl.BlockSpec((None, tm, d), lambda b, i: (b, i, 0)),
        _mod_spec(l, 5, bsz, n_ctx // tm, tile0),
        _layer_vec(lng, l), _layer_vec(lnb, l),
        pl.BlockSpec(memory_space=pl.ANY),
    ]
    scratch = [pltpu.VMEM((2, 2, tm) + ys.shape[1:], ys.dtype), pltpu.SemaphoreType.DMA((2,))]
    return in_specs, scratch, (slots3, slots3, rec, x1, mods4, lng, lnb, ys)


def _combine(l, slots3, rec, x1, mods4, lng, lnb, ys, n_ctx, tile0, alpha):
    bsz, t_out, d = x1.shape
    tm = ROW_TILE
    in_specs, scratch, args = _combine_specs(l, slots3, rec, x1, mods4, lng, lnb, ys, n_ctx, tile0)
    return pl.pallas_call(
        functools.partial(_combine_kernel, alpha=alpha),
        grid=(bsz, t_out // tm),
        in_specs=in_specs,
        out_specs=pl.BlockSpec((None, tm, d), lambda b, i: (b, i, 0)),
        out_shape=jax.ShapeDtypeStruct((bsz, t_out, d), F32),
        scratch_shapes=scratch,
        compiler_params=pltpu.CompilerParams(dimension_semantics=("arbitrary", "arbitrary")),
        name="moe_combine_ln2",
    )(*args)


def _combine_inproj(l, slots3, rec, x1, mods4, lng, lnb, ys, tabs, w_in_bf, n_ctx, alpha):
    bsz, t_all, d = x1.shape
    tm = ROW_TILE
    in_specs, scratch, args = _combine_specs(l, slots3, rec, x1, mods4, lng, lnb, ys, n_ctx, 0)
    p_specs, p_out_specs, p_out_shape, p_args = _inproj_specs(l + 1, bsz, t_all, mods4, tabs, w_in_bf, n_ctx)
    return pl.pallas_call(
        functools.partial(_combine_inproj_kernel, alpha=alpha),
        grid=(bsz, t_all // tm),
        in_specs=in_specs + p_specs,
        out_specs=[pl.BlockSpec((None, tm, d), lambda b, i: (b, i, 0))] + p_out_specs,
        out_shape=[jax.ShapeDtypeStruct((bsz, t_all, d), F32)] + p_out_shape,
        scratch_shapes=scratch,
        compiler_params=pltpu.CompilerParams(
            dimension_semantics=("arbitrary", "arbitrary"), vmem_limit_bytes=VMEM_LIMIT),
        name="moe_combine_ln2_inproj",
    )(*args, *p_args)


def _rope_tables(n_ctx, n_lat):
    rows = n_lat // GRID_W
    row = jnp.repeat(jnp.arange(rows, dtype=F32), GRID_W)
    col = jnp.tile(jnp.arange(GRID_W, dtype=F32), rows)
    inv = 1.0 / (ROPE_BASE ** (jnp.arange(0, ROPE_AXIS_DIM, 2, dtype=F32) / ROPE_AXIS_DIM))
    half = ROPE_AXIS_DIM // 2
    zeros = jnp.zeros((n_lat, half), F32)
    cos, sa, sb = [], [], []
    for pos in (row, col):
        ang = pos[:, None] * inv
        c, s = jnp.cos(ang), jnp.sin(ang)
        cos += [c, c]
        sa += [-s, zeros]
        sb += [zeros, s]
    reps = LANES // HEAD_DIM

    def full(parts, ctx_fill):
        lat = jnp.tile(jnp.concatenate(parts, axis=1), (1, reps))
        return jnp.concatenate([jnp.full((n_ctx, LANES), ctx_fill, F32), lat], axis=0)

    return full(cos, 1.0), full(sa, 0.0), full(sb, 0.0)


def kernel(x, c, ctx, c_ctx, w_mod, b_mod, w_in, lam_q1, lam_k1, lam_q2, lam_k2, subln_g,
           ssm_a_re, ssm_a_im, ssm_log_dt, ssm_b_re, ssm_b_im, ssm_c_re, ssm_c_im, ssm_d,
           w_glu, b_glu, w_pa, w_ps, w_o, ln1_g, ln1_b,
           router_g_w, router_g_b, router_e_w, router_e_b, moe_w1, moe_w3, moe_w2, ln2_g, ln2_b):
    bsz, n_lat, d = x.shape
    n_ctx = ctx.shape[1]
    depth = w_mod.shape[0]
    assert d == D_MODEL and bsz % SUBLANES == 0
    assert n_ctx % ROW_TILE == 0 and n_lat % ROW_TILE == 0 and n_ctx % SCAN_TIME_TILE == 0
    alpha = (2.0 * depth) ** 0.25
    n_ctx_tiles = n_ctx // ROW_TILE

    mod_rows = -(-(bsz + 1) // SUBLANES) * SUBLANES
    cvec = jnp.concatenate([c, c_ctx[None, :], jnp.zeros((mod_rows - bsz - 1, d), F32)], axis=0)
    mods4 = _modulation(cvec, w_mod, b_mod).reshape(depth, mod_rows, 1, N_MOD * d)
    tabs = _rope_tables(n_ctx, n_lat)
    xall = jnp.concatenate([ctx, x], axis=1)
    vecs = lambda a: a.reshape(depth, 1, a.shape[-1])
    ops = jax.vmap(_ssm_tables)(ssm_a_re, ssm_a_im, ssm_log_dt, ssm_b_re, ssm_b_im, ssm_c_re, ssm_c_im, ssm_d)
    n_pad = LANES - N_GROUPS - N_EXPERTS
    wr = jnp.concatenate([router_g_w, router_e_w, jnp.zeros((depth, d, n_pad), F32)], axis=2)
    br = jnp.concatenate([router_g_b, router_e_b, jnp.zeros((depth, n_pad), F32)], axis=1)
    wr_hi = wr.astype(BF16)
    wr_split = jnp.stack([wr_hi, (wr - wr_hi.astype(F32)).astype(BF16)], axis=1)
    w_in_bf = w_in.astype(BF16)
    wts = (w_pa.astype(BF16), w_glu.astype(BF16), vecs(b_glu), w_ps.astype(BF16), w_o.astype(BF16),
           vecs(ln1_g), vecs(ln1_b), wr_split, vecs(br))
    lam_vecs = (vecs(lam_q1), vecs(lam_k1), vecs(lam_q2), vecs(lam_k2))
    ln2 = (vecs(ln2_g), vecs(ln2_b))
    subln = vecs(subln_g)

    proj = _inproj(0, xall, mods4, tabs, w_in_bf, n_ctx)
    for l in range(depth):
        last = l == depth - 1
        tile0 = n_ctx_tiles if last else 0
        lam_init = 0.8 - 0.6 * math.exp(-0.3 * l)
        k, v, u, q, siga, sigs = proj
        attn = _attention(l, q, k, v, lam_vecs, subln, n_ctx, tile0, lam_init)
        y = _ssm(l, u, ops, n_ctx)
        x1, h2, choices, counts = _merge(l, attn, y, siga, sigs, xall, mods4, wts, n_ctx, tile0, alpha)

        n_tok = h2.shape[0]
        n_blocks = -(-(2 * n_tok + N_EXPERTS * (MOE_BLOCK - 1)) // MOE_BLOCK)
        rec, blk = _route(choices, counts, n_blocks)
        n_tiles_tok = n_tok // ROW_TILE
        slots3 = (rec[:, 0:2].astype(jnp.int32).reshape(n_tiles_tok, ROW_TILE, 2)
                  .transpose(0, 2, 1).reshape(n_tiles_tok, 1, 2 * ROW_TILE))
        xs = _dispatch(h2, slots3, n_blocks * MOE_BLOCK)
        ys = _experts(l, blk[:n_blocks, 0], xs, moe_w1, moe_w3, moe_w2)
        if last:
            xall = _combine(l, slots3, rec, x1, mods4, ln2[0], ln2[1], ys, n_ctx, tile0, alpha)
        else:
            xall, *proj = _combine_inproj(l, slots3, rec, x1, mods4, ln2[0], ln2[1], ys, tabs, w_in_bf, n_ctx,
                                          alpha)
    return xall
```

```python
import functools
import math

import jax
import jax.numpy as jnp
from jax import lax
from jax.experimental import pallas as pl
from jax.experimental.pallas import tpu as pltpu

F32 = jnp.float32
BF16 = jnp.bfloat16

D_MODEL = 1024
N_HEADS = 8
HEAD_DIM = 64
V_DIM = 2 * HEAD_DIM
QK_W = N_HEADS * 2 * HEAD_DIM
ATTN_W = N_HEADS * V_DIM
SSM_W = D_MODEL // 2
SSM_GROUP = 16
SSM_GROUPS = SSM_W // SSM_GROUP
SSM_STATE = 64
N_GROUPS = 4
EXPERTS_PER_GROUP = 8
N_EXPERTS = N_GROUPS * EXPERTS_PER_GROUP
EXPERT_HIDDEN = D_MODEL // 2
KVU_W = QK_W + ATTN_W + SSM_W
IN_W = KVU_W + QK_W + 2 * D_MODEL
N_MOD = 6
GRID_W = 64
ROPE_BASE = 10000.0
ROPE_AXIS_DIM = HEAD_DIM // 2
LN_EPS = 1e-5

LANES = 128
SUBLANES = 8
VMEM_LIMIT = 56 * 1024 * 1024

ROW_TILE = 256
SCAN_CHUNK = 8
SCAN_TIME_TILE = 256
MOE_BLOCK = 256
ROUTE_TILE = 512
HEADS_PER_STEP = 4
SCORE_LOOKAHEAD = 3
GATHER_PIECES = 16
ISSUE_UNROLL = 8
LANE_TILES = SSM_W // LANES
GROUPS_PER_TILE = LANES // SSM_GROUP
STATE_W = GROUPS_PER_TILE * SSM_STATE
NEG_BIG = -3.0e38


def _sigmoid(x):
    return 1.0 / (1.0 + jnp.exp(-x))


def _pack_rows(x):
    w = x.shape[1] // 2
    bits = lax.bitcast_convert_type(x.astype(BF16).astype(F32), jnp.uint32)
    return bits[:, :w] | (bits[:, w:] >> 16)


def _unpack_rows(p):
    hi = lax.bitcast_convert_type(p & jnp.uint32(0xFFFF0000), F32)
    lo = lax.bitcast_convert_type(p << 16, F32)
    return jnp.concatenate([hi, lo], axis=1)


def _layer_norm(x, g, b):
    xc = x - jnp.mean(x, axis=-1, keepdims=True)
    var = jnp.mean(xc * xc, axis=-1, keepdims=True)
    return xc * lax.rsqrt(var + LN_EPS) * g + b


def _layer_vec(arr, l):
    return pl.BlockSpec((None, 1, arr.shape[-1]), lambda b, i: (l, 0, 0))


def _layer_mat(arr, l):
    return pl.BlockSpec((None,) + arr.shape[1:], lambda b, i: (l, 0, 0))


def _mod_spec(l, col, bsz, n_ctx_tiles, tile0):
    return pl.BlockSpec((None, None, 1, D_MODEL),
                        lambda b, i: (l, jnp.where(i + tile0 < n_ctx_tiles, bsz, b), 0, col))


def _mod_kernel(c_ref, w_ref, b_ref, o_ref):
    c = c_ref[...]
    s = c * _sigmoid(c)
    o_ref[...] = jnp.dot(s, w_ref[...], preferred_element_type=F32, precision=lax.Precision.HIGHEST) + b_ref[...]


def _modulation(cvec, w_mod, b_mod):
    depth, d, w6 = w_mod.shape
    rows = cvec.shape[0]
    tn = 1024
    return pl.pallas_call(
        _mod_kernel,
        grid=(depth, w6 // tn),
        in_specs=[
            pl.BlockSpec((rows, d), lambda l, j: (0, 0)),
            pl.BlockSpec((None, d, tn), lambda l, j: (l, 0, j)),
            pl.BlockSpec((None, 1, tn), lambda l, j: (l, 0, j)),
        ],
        out_specs=pl.BlockSpec((None, rows, tn), lambda l, j: (l, 0, j)),
        out_shape=jax.ShapeDtypeStruct((depth, rows, w6), F32),
        name="modulation",
    )(cvec, w_mod, b_mod.reshape(depth, 1, w6))


def _inproj_kernel(x_ref, sh_ref, sc_ref, cos_ref, sa_ref, sb_ref, w_ref,
                   k_ref, v_ref, u_ref, q_ref, ga_ref, gs_ref):
    _inproj_body(x_ref[...], sh_ref, sc_ref, cos_ref, sa_ref, sb_ref, w_ref,
                 k_ref, v_ref, u_ref, q_ref, ga_ref, gs_ref)


def _inproj_body(x, sh_ref, sc_ref, cos_ref, sa_ref, sb_ref, w_ref, k_ref, v_ref, u_ref, q_ref, ga_ref, gs_ref,
                 between=None):
    def stage_done():
        if between is not None:
            piece = next(between, None)
            if piece is not None:
                piece()

    h = (x * (1.0 + sc_ref[...]) + sh_ref[...]).astype(BF16)
    cos, sa, sb = cos_ref[...], sa_ref[...], sb_ref[...]

    def rope(t):
        return t * cos + pltpu.roll(t, LANES - 16, 1) * sa + pltpu.roll(t, 16, 1) * sb

    def proj(lo, hi):
        return jnp.dot(h, w_ref[:, lo:hi], preferred_element_type=F32)

    kk = proj(0, QK_W)
    for c in range(QK_W // LANES):
        k_ref[:, c * LANES:(c + 1) * LANES] = rope(kk[:, c * LANES:(c + 1) * LANES]).astype(BF16)
        stage_done()
    v_ref[...] = proj(QK_W, QK_W + ATTN_W).astype(BF16)
    stage_done()
    u_ref[...] = proj(QK_W + ATTN_W, KVU_W)
    stage_done()
    qq = proj(KVU_W, KVU_W + QK_W)
    scale = HEAD_DIM ** -0.5 * math.log2(math.e)
    for c in range(QK_W // LANES):
        q_ref[:, c * LANES:(c + 1) * LANES] = (rope(qq[:, c * LANES:(c + 1) * LANES]) * scale).astype(BF16)
        stage_done()
    ga_ref[...] = _sigmoid(proj(KVU_W + QK_W, KVU_W + QK_W + D_MODEL)).astype(BF16)
    stage_done()
    gs_ref[...] = _sigmoid(proj(KVU_W + QK_W + D_MODEL, IN_W)).astype(BF16)
    if between is not None:
        for piece in between:
            piece()


def _inproj_specs(l, bsz, t_all, mods4, tabs, w_in_bf, n_ctx):
    tm = ROW_TILE
    n_ctx_tiles = n_ctx // tm
    tok_spec = lambda w: pl.BlockSpec((None, tm, w), lambda b, i: (b, i, 0))
    tab_spec = pl.BlockSpec((tm, LANES), lambda b, i: (i, 0))
    big = lambda w, dt: jax.ShapeDtypeStruct((bsz, t_all, w), dt)
    in_specs = [_mod_spec(l, 0, bsz, n_ctx_tiles, 0), _mod_spec(l, 1, bsz, n_ctx_tiles, 0),
                tab_spec, tab_spec, tab_spec, _layer_mat(w_in_bf, l)]
    out_specs = [tok_spec(QK_W), tok_spec(ATTN_W), tok_spec(SSM_W),
                 tok_spec(QK_W), tok_spec(D_MODEL), tok_spec(D_MODEL)]
    out_shape = [big(QK_W, BF16), big(ATTN_W, BF16), big(SSM_W, F32),
                 big(QK_W, BF16), big(D_MODEL, BF16), big(D_MODEL, BF16)]
    return in_specs, out_specs, out_shape, (mods4, mods4, tabs[0], tabs[1], tabs[2], w_in_bf)


def _inproj(l, xall, mods4, tabs, w_in_bf, n_ctx):
    bsz, t_all, d = xall.shape
    tm = ROW_TILE
    in_specs, out_specs, out_shape, args = _inproj_specs(l, bsz, t_all, mods4, tabs, w_in_bf, n_ctx)
    return pl.pallas_call(
        _inproj_kernel,
        grid=(bsz, t_all // tm),
        in_specs=[pl.BlockSpec((None, tm, d), lambda b, i: (b, i, 0))] + in_specs,
        out_specs=out_specs,
        out_shape=out_shape,
        compiler_params=pltpu.CompilerParams(
            dimension_semantics=("parallel", "arbitrary"), vmem_limit_bytes=VMEM_LIMIT),
        name="inproj",
    )(xall, *args)


def _attn_kernel(q_ref, k_ref, v_ref, lq1_ref, lk1_ref, lq2_ref, lk2_ref, g_ref, o_ref,
                 *, n_ctx, n_ctx_tiles, tile0, lam_init):
    i = pl.program_id(2) + tile0
    lam = (jnp.exp(jnp.sum(lq1_ref[...] * lk1_ref[...], axis=1, keepdims=True))
           - jnp.exp(jnp.sum(lq2_ref[...] * lk2_ref[...], axis=1, keepdims=True)) + lam_init)
    g = g_ref[...]
    lane = lax.broadcasted_iota(jnp.int32, (q_ref.shape[0], V_DIM), 1)

    def scores(hh, n_kv):
        cols = slice(hh * V_DIM, (hh + 1) * V_DIM)
        q = q_ref[:, cols]
        zero = jnp.zeros_like(q)
        k = k_ref[0:n_kv, cols]
        nt = (((1,), (1,)), ((), ()))
        s1 = lax.dot_general(jnp.where(lane < HEAD_DIM, q, zero), k, nt, preferred_element_type=F32)
        s2 = lax.dot_general(jnp.where(lane < HEAD_DIM, zero, q), k, nt, preferred_element_type=F32)
        return s1, s2

    def finish(hh, n_kv, s1, s2):
        cols = slice(hh * V_DIM, (hh + 1) * V_DIM)

        def probs(s):
            p = jnp.exp2(s - jnp.max(s, axis=-1, keepdims=True))
            return p, 1.0 / jnp.sum(p, axis=-1, keepdims=True)

        p1, r1 = probs(s1)
        p2, r2 = probs(s2)
        a = p1 - p2 * (lam * r2 / r1)
        o = jnp.dot(a.astype(BF16), v_ref[0:n_kv, cols], preferred_element_type=F32) * r1
        o = o * lax.rsqrt(jnp.mean(o * o, axis=-1, keepdims=True) + LN_EPS) * g * (1.0 - lam_init)
        o_ref[:, cols] = o.astype(BF16)

    def attend(n_kv):
        pending = []
        for hh in range(HEADS_PER_STEP):
            pending.append((hh, scores(hh, n_kv)))
            if len(pending) > SCORE_LOOKAHEAD:
                h0, s0 = pending.pop(0)
                finish(h0, n_kv, *s0)
        for h0, s0 in pending:
            finish(h0, n_kv, *s0)

    if n_ctx_tiles > tile0:
        @pl.when(i < n_ctx_tiles)
        def _():
            attend(n_ctx)

        @pl.when(i >= n_ctx_tiles)
        def _():
            attend(k_ref.shape[0])
    else:
        attend(k_ref.shape[0])


def _attention(l, q, k, v, lam_vecs, subln_g, n_ctx, tile0, lam_init):
    bsz, t_all, _ = q.shape
    tq = ROW_TILE
    hw = HEADS_PER_STEP * V_DIM
    n_tiles = t_all // tq - tile0
    vec = lambda arr: pl.BlockSpec((None, 1, arr.shape[-1]), lambda b, h, i: (l, 0, 0))
    kern = functools.partial(_attn_kernel, n_ctx=n_ctx, n_ctx_tiles=n_ctx // tq, tile0=tile0, lam_init=lam_init)
    return pl.pallas_call(
        kern,
        grid=(bsz, N_HEADS // HEADS_PER_STEP, n_tiles),
        in_specs=[
            pl.BlockSpec((None, tq, hw), lambda b, h, i: (b, i + tile0, h)),
            pl.BlockSpec((None, t_all, hw), lambda b, h, i: (b, 0, h)),
            pl.BlockSpec((None, t_all, hw), lambda b, h, i: (b, 0, h)),
            vec(lam_vecs[0]), vec(lam_vecs[1]), vec(lam_vecs[2]), vec(lam_vecs[3]), vec(subln_g),
        ],
        out_specs=pl.BlockSpec((None, tq, hw), lambda b, h, i: (b, i, h)),
        out_shape=jax.ShapeDtypeStruct((bsz, n_tiles * tq, ATTN_W), BF16),
        compiler_params=pltpu.CompilerParams(
            dimension_semantics=("parallel", "parallel", "arbitrary"), vmem_limit_bytes=VMEM_LIMIT),
        name="diff_attention",
    )(q, k, v, *lam_vecs, subln_g)


def _ssm_tables(a_re, a_im, log_dt, b_re, b_im, c_re, c_im, d_skip):
    lc = SCAN_CHUNK
    lam = lax.complex(a_re.astype(F32), a_im.astype(F32))
    dt = jnp.exp(log_dt.astype(F32))[..., None]
    ldt = lam * dt
    a_bar = jnp.exp(ldt)
    b_bar = ((a_bar - 1.0) / lam)[..., None] * lax.complex(b_re.astype(F32), b_im.astype(F32))
    cm = lax.complex(c_re.astype(F32), c_im.astype(F32))
    steps = jnp.arange(lc + 1, dtype=F32)
    apow = jnp.exp(ldt[None] * steps[:, None, None, None])
    s_idx = jnp.arange(lc)
    lag_f = s_idx[None, :] - s_idx[:, None]
    lk = lc * LANES
    eye_c = jnp.eye(SSM_GROUP, dtype=F32)
    eye_t = jnp.eye(lc, dtype=F32)
    d_g = d_skip.astype(F32).reshape(SSM_GROUPS, SSM_GROUP)

    def rows_of(t):
        t = t.reshape(lc, LANE_TILES, GROUPS_PER_TILE, SSM_GROUP, LANES)
        return t.transpose(1, 0, 2, 3, 4).reshape(LANE_TILES, lk, LANES)

    inject, readout, intra = [], [], []
    for di in range(2):
        pw_in = apow[lc - 1 - s_idx, di] if di == 0 else apow[s_idx, di]
        w = jnp.einsum('sgp,gpi->sgip', pw_in, b_bar[di])
        inject.append(rows_of(jnp.concatenate([w.real, w.imag], axis=-1)))
        pw_out = apow[s_idx + 1, di] if di == 0 else apow[lc - s_idx, di]
        vv = cm[di][None] * pw_out[:, :, None, :]
        readout.append(rows_of(jnp.concatenate([vv.real, -vv.imag], axis=-1)))
        kern = jnp.einsum('gcp,jgp,gpi->jgci', cm[di], apow[:lc, di], b_bar[di]).real
        lag = lag_f if di == 0 else -lag_f
        toe = jnp.where((lag >= 0)[:, :, None, None, None], kern[jnp.clip(lag, 0, lc - 1)], 0.0)
        toe = toe.transpose(0, 2, 4, 1, 3)
        if di == 0:
            toe = toe + jnp.einsum('st,gi,ic->sgitc', eye_t, d_g, eye_c)
        intra.append(rows_of(toe.reshape(lc, SSM_GROUPS, SSM_GROUP, LANES)))
    al = apow[lc].reshape(2, LANE_TILES, 1, STATE_W)
    return jnp.stack(inject), jnp.stack(readout), jnp.stack(intra), al.real, al.imag


def _ssm_kernel(u_ref, wi_ref, wo_ref, wk_ref, alr_ref, ali_ref, y_ref,
                bc_scr, wct_scr, a_scr, s_scr, h_scr, st_scr):
    ph = pl.program_id(1)
    ti = pl.program_id(2)
    lc = SCAN_CHUNK
    bsz = u_ref.shape[0]
    n_chunks = u_ref.shape[1] // lc
    n_rows = bsz * n_chunks
    lk = lc * LANES

    @pl.when(ti == 0)
    def _():
        st_scr[...] = jnp.zeros_like(st_scr)
        q = lax.broadcasted_iota(jnp.int32, (LANES, lk), 0)
        c = lax.broadcasted_iota(jnp.int32, (LANES, lk), 1)
        sel_state = (q == ((c >> 9) << 6) + (c & (SSM_STATE - 1))).astype(BF16)
        sel_tok = (q == ((c >> 7) << 4) + (c & (SSM_GROUP - 1))).astype(BF16)
        row_g = (lax.broadcasted_iota(jnp.int32, (lk, lk), 0) >> 4) & (GROUPS_PER_TILE - 1)
        col = lax.broadcasted_iota(jnp.int32, (lk, lk), 1)
        same_state = row_g == ((col >> 6) & (GROUPS_PER_TILE - 1))
        same_tok = row_g == ((col >> 4) & (GROUPS_PER_TILE - 1))

        def spread(tab_ref, sel, same):
            full = jnp.dot(tab_ref[...].astype(BF16), sel, preferred_element_type=F32)
            return jnp.where(same, full, 0.0).astype(BF16)

        bc_scr[...] = spread(wi_ref, sel_state, same_state)
        wct_scr[...] = spread(wo_ref, sel_state, same_state)
        a_scr[...] = spread(wk_ref, sel_tok, same_tok)

    x = jnp.concatenate(
        [jnp.concatenate([u_ref[b, pl.ds(s, n_chunks, stride=lc), :] for s in range(lc)], axis=1)
         for b in range(bsz)], axis=0).astype(BF16)
    r = lax.broadcasted_iota(jnp.int32, (n_rows, n_rows), 0)
    cc = lax.broadcasted_iota(jnp.int32, (n_rows, n_rows), 1)
    sh_b, sh_c = bsz.bit_length() - 1, n_chunks.bit_length() - 1
    to_cb = (cc == ((r & (bsz - 1)) << sh_c) + (r >> sh_b)).astype(BF16)
    to_bc = (cc == ((r & (n_chunks - 1)) << sh_b) + (r >> sh_c)).astype(BF16)
    x_cb = jnp.dot(to_cb, x, preferred_element_type=F32).astype(BF16)
    s_scr[...] = jnp.dot(x_cb, bc_scr[...], preferred_element_type=F32)
    alr, ali = alr_ref[...], ali_ref[...]

    def step(c, carry):
        hr, hi = carry
        ce = jnp.where(ph == 0, c, n_chunks - 1 - c)
        rows = pl.ds(pl.multiple_of(ce * bsz, bsz), bsz)
        h_scr[rows, 0:STATE_W] = hr
        h_scr[rows, STATE_W:2 * STATE_W] = hi
        sr = s_scr[rows, 0:STATE_W]
        si = s_scr[rows, STATE_W:2 * STATE_W]
        return alr * hr - ali * hi + sr, alr * hi + ali * hr + si

    hr, hi = lax.fori_loop(0, n_chunks, step, (st_scr[0], st_scr[1]))
    st_scr[0] = hr
    st_scr[1] = hi
    h_bc = jnp.dot(to_bc, h_scr[...].astype(BF16), preferred_element_type=F32).astype(BF16)
    y = (jnp.dot(x, a_scr[...], preferred_element_type=F32)
         + lax.dot_general(h_bc, wct_scr[...], (((1,), (1,)), ((), ())), preferred_element_type=F32))
    for b in range(bsz):
        for s in range(lc):
            y_ref[b, pl.ds(s, n_chunks, stride=lc), :] = y[b * n_chunks:(b + 1) * n_chunks,
                                                           s * LANES:(s + 1) * LANES]


def _ssm(l, u, tables, n_ctx):
    bsz, t_all, _ = u.shape
    lc = SCAN_CHUNK
    tt = SCAN_TIME_TILE
    n_t = t_all // tt
    n_ctx_t = n_ctx // tt
    rows = bsz * tt // lc
    inject, readout, intra, alr, ali = tables
    lk = lc * LANES
    assert 2 * STATE_W == lk and SSM_STATE == 64 and SSM_GROUP == 16 and LANES == 128
    assert bsz & (bsz - 1) == 0 and (tt // lc) & (tt // lc - 1) == 0

    def tile_of(ph, i):
        rev = jnp.where(i < n_ctx_t, n_ctx_t - 1 - i, n_t - 1 - (i - n_ctx_t))
        return jnp.where(ph == 0, i, rev)

    op_spec = lambda r, c: pl.BlockSpec((None, None, None, r, c), lambda j, ph, i: (l, ph, j, 0, 0))
    return pl.pallas_call(
        _ssm_kernel,
        grid=(LANE_TILES, 2, n_t),
        in_specs=[
            pl.BlockSpec((bsz, tt, LANES), lambda j, ph, i: (0, tile_of(ph, i), j)),
            op_spec(lk, LANES), op_spec(lk, LANES), op_spec(lk, LANES),
            op_spec(1, STATE_W), op_spec(1, STATE_W),
        ],
        out_specs=pl.BlockSpec((None, bsz, tt, LANES), lambda j, ph, i: (ph, 0, tile_of(ph, i), j)),
        out_shape=jax.ShapeDtypeStruct((2, bsz, t_all, SSM_W), F32),
        scratch_shapes=[
            pltpu.VMEM((lk, 2 * STATE_W), BF16),
            pltpu.VMEM((lk, 2 * STATE_W), BF16),
            pltpu.VMEM((lk, lk), BF16),
            pltpu.VMEM((rows, 2 * STATE_W), F32),
            pltpu.VMEM((rows, 2 * STATE_W), F32),
            pltpu.VMEM((2, bsz, STATE_W), F32),
        ],
        compiler_params=pltpu.CompilerParams(
            dimension_semantics=("parallel", "arbitrary", "arbitrary"), vmem_limit_bytes=VMEM_LIMIT),
        name="s5_scan",
    )(u, inject, readout, intra, alr, ali)


def _merge_kernel(attn_ref, y_ref, ga_ref, gs_ref, x_ref, g1_ref, sh2_ref, sc2_ref,
                  wpa_ref, wglu_ref, bglu_ref, wps_ref, wo_ref, lng_ref, lnb_ref, wr_ref, br_ref,
                  x1_ref, h2_ref, ch_ref, cnt_ref, *, alpha):
    a = jnp.dot(attn_ref[...], wpa_ref[...], preferred_element_type=F32)
    ys = y_ref[0] + y_ref[1]
    gl = ys * (0.5 * (1.0 + jnp.tanh(math.sqrt(2.0 / math.pi) * (ys + 0.044715 * (ys * ys * ys)))))
    z = jnp.dot(gl.astype(BF16), wglu_ref[...], preferred_element_type=F32) + bglu_ref[...]
    sg = gl * _sigmoid(z)
    s = jnp.dot(sg.astype(BF16), wps_ref[...], preferred_element_type=F32)
    m = ga_ref[...].astype(F32) * a + gs_ref[...].astype(F32) * s
    y = jnp.dot(m.astype(BF16), wo_ref[...], preferred_element_type=F32)
    x1 = _layer_norm(alpha * x_ref[...] + g1_ref[...] * y, lng_ref[...], lnb_ref[...])
    x1_ref[...] = x1
    h2 = x1 * (1.0 + sc2_ref[...]) + sh2_ref[...]
    h2_ref[...] = _pack_rows(h2)
    h_hi = h2.astype(BF16)
    h_lo = (h2 - h_hi.astype(F32)).astype(BF16)
    w_hi, w_lo = wr_ref[0], wr_ref[1]
    lg = (jnp.dot(h_hi, w_hi, preferred_element_type=F32)
          + (jnp.dot(h_lo, w_hi, preferred_element_type=F32)
             + jnp.dot(h_hi, w_lo, preferred_element_type=F32))) + br_ref[...]
    e1, e2, w1, w2 = _route_choices(lg)
    lane = lax.broadcasted_iota(jnp.int32, lg.shape, 1).astype(F32)
    ch_ref[...] = _record(lane, e1, e2, w1, w2)
    rows_per_expert = jnp.sum((lane == e1).astype(F32) + (lane == e2).astype(F32), axis=0, keepdims=True)

    @pl.when((pl.program_id(0) == 0) & (pl.program_id(1) == 0))
    def _():
        cnt_ref[...] = jnp.zeros_like(cnt_ref)

    cnt_ref[...] += jnp.broadcast_to(rows_per_expert, cnt_ref.shape)


def _merge(l, attn, y, siga, sigs, xall, mods4, wts, n_ctx, tile0, alpha):
    bsz, t_all, d = xall.shape
    tm = ROW_TILE
    n_ctx_tiles = n_ctx // tm
    n_tiles = t_all // tm - tile0
    t_out = n_tiles * tm
    tok = lambda w: pl.BlockSpec((None, tm, w), lambda b, i: (b, i + tile0, 0))
    own = lambda w: pl.BlockSpec((None, tm, w), lambda b, i: (b, i, 0))
    modv = lambda col: _mod_spec(l, col, bsz, n_ctx_tiles, tile0)
    wpa, wglu, bglu, wps, wo, lng, lnb, wr, br = wts
    kern = functools.partial(_merge_kernel, alpha=alpha)
    return pl.pallas_call(
        kern,
        grid=(bsz, n_tiles),
        in_specs=[
            own(ATTN_W),
            pl.BlockSpec((2, None, tm, SSM_W), lambda b, i: (0, b, i + tile0, 0)),
            tok(D_MODEL), tok(D_MODEL), tok(d),
            modv(2), modv(3), modv(4),
            _layer_mat(wpa, l), _layer_mat(wglu, l), _layer_vec(bglu, l), _layer_mat(wps, l), _layer_mat(wo, l),
            _layer_vec(lng, l), _layer_vec(lnb, l),
            pl.BlockSpec((None,) + wr.shape[1:], lambda b, i: (l, 0, 0, 0)), _layer_vec(br, l),
        ],
        out_specs=[
            own(d),
            pl.BlockSpec((tm, d // 2), lambda b, i: (b * n_tiles + i, 0)),
            pl.BlockSpec((tm, LANES), lambda b, i: (b * n_tiles + i, 0)),
            pl.BlockSpec((SUBLANES, LANES), lambda b, i: (0, 0)),
        ],
        out_shape=[
            jax.ShapeDtypeStruct((bsz, t_out, d), F32),
            jax.ShapeDtypeStruct((bsz * t_out, d // 2), jnp.uint32),
            jax.ShapeDtypeStruct((bsz * t_out, LANES), F32),
            jax.ShapeDtypeStruct((SUBLANES, LANES), F32),
        ],
        compiler_params=pltpu.CompilerParams(
            dimension_semantics=("arbitrary", "arbitrary"), vmem_limit_bytes=VMEM_LIMIT),
        name="merge_ln1",
    )(attn, y, siga, sigs, xall, mods4, mods4, mods4, wpa, wglu, bglu, wps, wo, lng, lnb, wr, br)


def _route_choices(lg):
    shape = lg.shape
    lane = lax.broadcasted_iota(jnp.int32, shape, 1).astype(F32)
    far = jnp.full(shape, 1.0e9, F32)

    def first_max(vals, mask):
        vm = jnp.where(mask, vals, NEG_BIG)
        mx = jnp.max(vm, axis=1, keepdims=True)
        idx = jnp.min(jnp.where(mask & (vm == mx), lane, far), axis=1, keepdims=True)
        return mx, idx

    gmask = lane < N_GROUPS
    gmax, gidx = first_max(lg, gmask)
    gtop = 1.0 / jnp.sum(jnp.where(gmask, jnp.exp(lg - gmax), 0.0), axis=1, keepdims=True)
    lo = N_GROUPS + EXPERTS_PER_GROUP * gidx
    emask = (lane >= lo) & (lane < lo + EXPERTS_PER_GROUP)
    v1, i1 = first_max(lg, emask)
    v2, i2 = first_max(lg, emask & (lane != i1))
    e2 = jnp.exp(v2 - v1)
    den = 1.0 + e2
    return i1 - N_GROUPS, i2 - N_GROUPS, (1.0 / den) * gtop, (e2 / den) * gtop


def _record(lane, a, b, c, d):
    return jnp.where(lane == 0.0, a, jnp.where(lane == 1.0, b, jnp.where(lane == 2.0, c, d)))


def _route_kernel(ch_ref, cnt_ref, rec_ref, blk_ref, start_scr, carry_scr):
    i = pl.program_id(0)
    ch = ch_ref[...]
    shape = ch.shape
    lane = lax.broadcasted_iota(jnp.int32, shape, 1).astype(F32)
    oh1 = (lane == ch[:, 0:1]).astype(F32)
    oh2 = (lane == ch[:, 1:2]).astype(F32)
    c1 = jnp.sum(oh1, axis=0, keepdims=True)
    c2 = jnp.sum(oh2, axis=0, keepdims=True)

    @pl.when(i == 0)
    def _():
        cnt = cnt_ref[0:1, :]
        padded = jnp.floor((cnt + (MOE_BLOCK - 1)) * (1.0 / MOE_BLOCK)) * MOE_BLOCK
        r = lax.broadcasted_iota(jnp.int32, (LANES, LANES), 0)
        c = lax.broadcasted_iota(jnp.int32, (LANES, LANES), 1)
        upper = (r < c).astype(F32)
        start = jnp.dot(jnp.broadcast_to(padded, (SUBLANES, LANES)), upper, preferred_element_type=F32,
                        precision=lax.Precision.HIGHEST)[0:1]
        start_scr[...] = start
        carry_scr[...] = jnp.zeros_like(carry_scr)
        nb = blk_ref.shape[0]
        blk_start = (lax.broadcasted_iota(jnp.int32, (nb, LANES), 0) * MOE_BLOCK).astype(F32)
        elane = lax.broadcasted_iota(jnp.int32, (nb, LANES), 1) < N_EXPERTS
        done = jnp.sum(jnp.where(elane & ((start + padded) <= blk_start), 1.0, 0.0), axis=1, keepdims=True)
        blk_ref[...] = jnp.broadcast_to(jnp.minimum(done, N_EXPERTS - 1.0), (nb, LANES)).astype(jnp.int32)

    tr = shape[0]
    r = lax.broadcasted_iota(jnp.int32, (tr, tr), 0)
    c = lax.broadcasted_iota(jnp.int32, (tr, tr), 1)
    tri = (c < r).astype(BF16)
    base = start_scr[...] + carry_scr[...]
    r1 = jnp.dot(tri, oh1.astype(BF16), preferred_element_type=F32)
    r2 = jnp.dot(tri, oh2.astype(BF16), preferred_element_type=F32) + c1
    d1 = jnp.sum(oh1 * (base + r1), axis=1, keepdims=True)
    d2 = jnp.sum(oh2 * (base + r2), axis=1, keepdims=True)
    carry_scr[...] += c1 + c2
    rec_ref[...] = _record(lane, d1, d2, ch[:, 2:3], ch[:, 3:4])


def _route(choices, counts, n_blocks):
    n_tok = choices.shape[0]
    tr = ROUTE_TILE
    nb_pad = -(-n_blocks // SUBLANES) * SUBLANES
    return pl.pallas_call(
        _route_kernel,
        grid=(n_tok // tr,),
        in_specs=[pl.BlockSpec((tr, LANES), lambda i: (i, 0)),
                  pl.BlockSpec(counts.shape, lambda i: (0, 0))],
        out_specs=[
            pl.BlockSpec((tr, LANES), lambda i: (i, 0)),
            pl.BlockSpec((nb_pad, LANES), lambda i: (0, 0)),
        ],
        out_shape=[
            jax.ShapeDtypeStruct((n_tok, LANES), F32),
            jax.ShapeDtypeStruct((nb_pad, LANES), jnp.int32),
        ],
        scratch_shapes=[pltpu.VMEM((1, LANES), F32)] * 2,
        compiler_params=pltpu.CompilerParams(dimension_semantics=("arbitrary",)),
        name="moe_route",
    )(choices, counts)


def _row_copy(src, dst, sem):
    return pltpu.make_async_copy(src, dst, sem)


def _dispatch_kernel(slot_ref, h_ref, xs_in_ref, xs_ref, sem):
    del xs_in_ref
    tg = h_ref.shape[0]

    for r in range(tg):
        for k in range(2):
            d = slot_ref[0, 0, k * tg + r]
            _row_copy(h_ref.at[pl.ds(r, 1)], xs_ref.at[pl.ds(d, 1)], sem).start(priority=k)

    def drain(r, carry):
        for k in range(2):
            _row_copy(h_ref.at[pl.ds(0, 1)], xs_ref.at[pl.ds(0, 1)], sem).wait()
        return carry

    lax.fori_loop(0, tg, drain, 0, unroll=ISSUE_UNROLL)


def _dispatch(h2, slots3, cap):
    n_tok, d = h2.shape
    tg = ROW_TILE
    xs0 = jnp.zeros((cap, d), h2.dtype)
    return pl.pallas_call(
        _dispatch_kernel,
        grid=(n_tok // tg,),
        in_specs=[
            pl.BlockSpec((1, 1, 2 * tg), lambda i: (i, 0, 0), memory_space=pltpu.SMEM),
            pl.BlockSpec((tg, d), lambda i: (i, 0)),
            pl.BlockSpec(memory_space=pl.ANY),
        ],
        out_specs=pl.BlockSpec(memory_space=pl.ANY),
        out_shape=jax.ShapeDtypeStruct((cap, d), h2.dtype),
        scratch_shapes=[pltpu.SemaphoreType.DMA(())],
        input_output_aliases={2: 0},
        compiler_params=pltpu.CompilerParams(dimension_semantics=("arbitrary",), has_side_effects=True),
        name="moe_dispatch",
    )(slots3, h2, xs0)


def _expert_kernel(be_ref, xs_ref, w1_hbm, w3_hbm, w2_hbm, ys_ref,
                   w1f, w3f, w2f, w1b, w3b, w2b, sem, slot_scr, *, layer):
    i = pl.program_id(0)
    n = pl.num_programs(0)
    e = be_ref[i]
    fresh = (i == 0) | (e != be_ref[jnp.maximum(i - 1, 0)])

    def fetch(expert, slot):
        return [_row_copy(w1_hbm.at[layer, expert], w1f.at[slot], sem.at[slot]),
                _row_copy(w3_hbm.at[layer, expert], w3f.at[slot], sem.at[slot]),
                _row_copy(w2_hbm.at[layer, expert], w2f.at[slot], sem.at[slot])]

    @pl.when(i == 0)
    def _():
        slot_scr[0] = 0
        for c in fetch(e, 0):
            c.start()

    @pl.when(fresh)
    def _():
        slot = slot_scr[0]
        for c in fetch(e, slot):
            c.wait()
        nxt = lax.while_loop(lambda j: (j < n) & (be_ref[jnp.minimum(j, n - 1)] == e), lambda j: j + 1, i + 1)

        @pl.when(nxt < n)
        def _():
            for c in fetch(be_ref[jnp.minimum(nxt, n - 1)], 1 - slot):
                c.start()

        w1b[...] = w1f[slot].astype(BF16)
        w3b[...] = w3f[slot].astype(BF16)
        w2b[...] = w2f[slot].astype(BF16)
        slot_scr[0] = 1 - slot

    x = _unpack_rows(xs_ref[...]).astype(BF16)
    a = jnp.dot(x, w1b[...], preferred_element_type=F32)
    b = jnp.dot(x, w3b[...], preferred_element_type=F32)
    hid = (a * _sigmoid(a)) * b
    ys_ref[...] = _pack_rows(jnp.dot(hid.astype(BF16), w2b[...], preferred_element_type=F32))


def _experts(l, blk_expert, xs, w1, w3, w2):
    cap, dp = xs.shape
    d, f = w1.shape[-2:]
    n_blocks = cap // MOE_BLOCK
    rows = pl.BlockSpec((MOE_BLOCK, dp), lambda i, be: (i, 0))
    hbm = pl.BlockSpec(memory_space=pl.ANY)
    grid_spec = pltpu.PrefetchScalarGridSpec(
        num_scalar_prefetch=1,
        grid=(n_blocks,),
        in_specs=[rows, hbm, hbm, hbm],
        out_specs=pl.BlockSpec((MOE_BLOCK, dp), lambda i, be: (i, 0)),
        scratch_shapes=[pltpu.VMEM((2, d, f), F32), pltpu.VMEM((2, d, f), F32), pltpu.VMEM((2, f, d), F32),
                        pltpu.VMEM((d, f), BF16), pltpu.VMEM((d, f), BF16), pltpu.VMEM((f, d), BF16),
                        pltpu.SemaphoreType.DMA((2,)), pltpu.SMEM((1,), jnp.int32)],
    )
    return pl.pallas_call(
        functools.partial(_expert_kernel, layer=l),
        grid_spec=grid_spec,
        out_shape=jax.ShapeDtypeStruct((cap, dp), xs.dtype),
        compiler_params=pltpu.CompilerParams(
            dimension_semantics=("arbitrary",), vmem_limit_bytes=VMEM_LIMIT),
        name="moe_experts",
    )(blk_expert, xs, w1, w3, w2)


def _gather_expert_rows(slot_ref, slot_nxt_ref, ys_ref, buf, sem):
    tm = buf.shape[2]
    g = pl.program_id(0) * pl.num_programs(1) + pl.program_id(1)
    cur = g & 1

    def issue(ref, slot, rows):
        for r in rows:
            for k in range(2):
                d = ref[0, 0, k * tm + r]
                _row_copy(ys_ref.at[pl.ds(d, 1)], buf.at[slot, k, pl.ds(r, 1)], sem.at[slot]).start(priority=k)

    @pl.when(g == 0)
    def _():
        issue(slot_ref, 0, range(tm))

    _wait_gathers(ys_ref, buf, sem, cur)
    per = tm // GATHER_PIECES
    pieces = [functools.partial(issue, slot_nxt_ref, 1 - cur, range(p * per, (p + 1) * per))
              for p in range(GATHER_PIECES)]
    return buf[cur, 0], buf[cur, 1], pieces


def _wait_gathers(ys_ref, buf, sem, slot):
    tm = buf.shape[2]

    def drain(r, carry):
        for k in range(2):
            _row_copy(ys_ref.at[pl.ds(0, 1)], buf.at[slot, 0, pl.ds(0, 1)], sem.at[slot]).wait()
        return carry

    lax.fori_loop(0, tm, drain, 0, unroll=ISSUE_UNROLL)


def _finish_gathers(ys_ref, buf, sem):
    g = pl.program_id(0) * pl.num_programs(1) + pl.program_id(1)

    @pl.when(g == pl.num_programs(0) * pl.num_programs(1) - 1)
    def _():
        _wait_gathers(ys_ref, buf, sem, 1 - (g & 1))


def _combine_rows(rec, rows1, rows2, x1, g2, lng, lnb, alpha):
    f = rec[:, 2:3] * _unpack_rows(rows1) + rec[:, 3:4] * _unpack_rows(rows2)
    return _layer_norm(alpha * x1 + g2 * f, lng, lnb)


def _combine_kernel(slot_ref, slot_nxt_ref, rec_ref, x_ref, g2_ref, lng_ref, lnb_ref, ys_ref, o_ref,
                    buf, sem, *, alpha):
    rows1, rows2, pieces = _gather_expert_rows(slot_ref, slot_nxt_ref, ys_ref, buf, sem)
    for piece in pieces:
        piece()
    o_ref[...] = _combine_rows(rec_ref[...], rows1, rows2, x_ref[...], g2_ref[...], lng_ref[...], lnb_ref[...],
                               alpha)
    _finish_gathers(ys_ref, buf, sem)


def _combine_inproj_kernel(slot_ref, slot_nxt_ref, rec_ref, x_ref, g2_ref, lng_ref, lnb_ref, ys_ref,
                           sh_ref, sc_ref, cos_ref, sa_ref, sb_ref, w_ref,
                           x2_ref, k_ref, v_ref, u_ref, q_ref, ga_ref, gs_ref, buf, sem, *, alpha):
    rows1, rows2, pieces = _gather_expert_rows(slot_ref, slot_nxt_ref, ys_ref, buf, sem)
    x2 = _combine_rows(rec_ref[...], rows1, rows2, x_ref[...], g2_ref[...], lng_ref[...], lnb_ref[...], alpha)
    x2_ref[...] = x2
    _inproj_body(x2, sh_ref, sc_ref, cos_ref, sa_ref, sb_ref, w_ref, k_ref, v_ref, u_ref, q_ref, ga_ref, gs_ref,
                 between=iter(pieces))
    _finish_gathers(ys_ref, buf, sem)


def _combine_specs(l, slots3, rec, x1, mods4, lng, lnb, ys, n_ctx, tile0):
    bsz, t_out, d = x1.shape
    tm = ROW_TILE
    n_tiles = t_out // tm
    total = bsz * n_tiles
    step = lambda b, i: b * n_tiles + i
    in_specs = [
        pl.BlockSpec((1, 1, 2 * tm), lambda b, i: (step(b, i), 0, 0), memory_space=pltpu.SMEM),
        pl.BlockSpec((1, 1, 2 * tm), lambda b, i: (jnp.minimum(step(b, i) + 1, total - 1), 0, 0),
                     memory_space=pltpu.SMEM),
        pl.BlockSpec((tm, LANES), lambda b, i: (step(b, i), 0)),
        pl.BlockSpec((None, tm, d), lambda b, i: (b, i, 0)),
        _mod_spec(l, 5, bsz, n_ctx // tm, tile0),
        _layer_vec(lng, l), _layer_vec(lnb, l),
        pl.BlockSpec(memory_space=pl.ANY),
    ]
    scratch = [pltpu.VMEM((2, 2, tm) + ys.shape[1:], ys.dtype), pltpu.SemaphoreType.DMA((2,))]
    return in_specs, scratch, (slots3, slots3, rec, x1, mods4, lng, lnb, ys)


def _combine(l, slots3, rec, x1, mods4, lng, lnb, ys, n_ctx, tile0, alpha):
    bsz, t_out, d = x1.shape
    tm = ROW_TILE
    in_specs, scratch, args = _combine_specs(l, slots3, rec, x1, mods4, lng, lnb, ys, n_ctx, tile0)
    return pl.pallas_call(
        functools.partial(_combine_kernel, alpha=alpha),
        grid=(bsz, t_out // tm),
        in_specs=in_specs,
        out_specs=pl.BlockSpec((None, tm, d), lambda b, i: (b, i, 0)),
        out_shape=jax.ShapeDtypeStruct((bsz, t_out, d), F32),
        scratch_shapes=scratch,
        compiler_params=pltpu.CompilerParams(dimension_semantics=("arbitrary", "arbitrary")),
        name="moe_combine_ln2",
    )(*args)


def _combine_inproj(l, slots3, rec, x1, mods4, lng, lnb, ys, tabs, w_in_bf, n_ctx, alpha):
    bsz, t_all, d = x1.shape
    tm = ROW_TILE
    in_specs, scratch, args = _combine_specs(l, slots3, rec, x1, mods4, lng, lnb, ys, n_ctx, 0)
    p_specs, p_out_specs, p_out_shape, p_args = _inproj_specs(l + 1, bsz, t_all, mods4, tabs, w_in_bf, n_ctx)
    return pl.pallas_call(
        functools.partial(_combine_inproj_kernel, alpha=alpha),
        grid=(bsz, t_all // tm),
        in_specs=in_specs + p_specs,
        out_specs=[pl.BlockSpec((None, tm, d), lambda b, i: (b, i, 0))] + p_out_specs,
        out_shape=[jax.ShapeDtypeStruct((bsz, t_all, d), F32)] + p_out_shape,
        scratch_shapes=scratch,
        compiler_params=pltpu.CompilerParams(
            dimension_semantics=("arbitrary", "arbitrary"), vmem_limit_bytes=VMEM_LIMIT),
        name="moe_combine_ln2_inproj",
    )(*args, *p_args)


def _rope_tables(n_ctx, n_lat):
    rows = n_lat // GRID_W
    row = jnp.repeat(jnp.arange(rows, dtype=F32), GRID_W)
    col = jnp.tile(jnp.arange(GRID_W, dtype=F32), rows)
    inv = 1.0 / (ROPE_BASE ** (jnp.arange(0, ROPE_AXIS_DIM, 2, dtype=F32) / ROPE_AXIS_DIM))
    half = ROPE_AXIS_DIM // 2
    zeros = jnp.zeros((n_lat, half), F32)
    cos, sa, sb = [], [], []
    for pos in (row, col):
        ang = pos[:, None] * inv
        c, s = jnp.cos(ang), jnp.sin(ang)
        cos += [c, c]
        sa += [-s, zeros]
        sb += [zeros, s]
    reps = LANES // HEAD_DIM

    def full(parts, ctx_fill):
        lat = jnp.tile(jnp.concatenate(parts, axis=1), (1, reps))
        return jnp.concatenate([jnp.full((n_ctx, LANES), ctx_fill, F32), lat], axis=0)

    return full(cos, 1.0), full(sa, 0.0), full(sb, 0.0)


def kernel(x, c, ctx, c_ctx, w_mod, b_mod, w_in, lam_q1, lam_k1, lam_q2, lam_k2, subln_g,
           ssm_a_re, ssm_a_im, ssm_log_dt, ssm_b_re, ssm_b_im, ssm_c_re, ssm_c_im, ssm_d,
           w_glu, b_glu, w_pa, w_ps, w_o, ln1_g, ln1_b,
           router_g_w, router_g_b, router_e_w, router_e_b, moe_w1, moe_w3, moe_w2, ln2_g, ln2_b):
    bsz, n_lat, d = x.shape
    n_ctx = ctx.shape[1]
    depth = w_mod.shape[0]
    assert d == D_MODEL and bsz % SUBLANES == 0
    assert n_ctx % ROW_TILE == 0 and n_lat % ROW_TILE == 0 and n_ctx % SCAN_TIME_TILE == 0
    alpha = (2.0 * depth) ** 0.25
    n_ctx_tiles = n_ctx // ROW_TILE

    mod_rows = -(-(bsz + 1) // SUBLANES) * SUBLANES
    cvec = jnp.concatenate([c, c_ctx[None, :], jnp.zeros((mod_rows - bsz - 1, d), F32)], axis=0)
    mods4 = _modulation(cvec, w_mod, b_mod).reshape(depth, mod_rows, 1, N_MOD * d)
    tabs = _rope_tables(n_ctx, n_lat)
    xall = jnp.concatenate([ctx, x], axis=1)
    vecs = lambda a: a.reshape(depth, 1, a.shape[-1])
    ops = jax.vmap(_ssm_tables)(ssm_a_re, ssm_a_im, ssm_log_dt, ssm_b_re, ssm_b_im, ssm_c_re, ssm_c_im, ssm_d)
    n_pad = LANES - N_GROUPS - N_EXPERTS
    wr = jnp.concatenate([router_g_w, router_e_w, jnp.zeros((depth, d, n_pad), F32)], axis=2)
    br = jnp.concatenate([router_g_b, router_e_b, jnp.zeros((depth, n_pad), F32)], axis=1)
    wr_hi = wr.astype(BF16)
    wr_split = jnp.stack([wr_hi, (wr - wr_hi.astype(F32)).astype(BF16)], axis=1)
    w_in_bf = w_in.astype(BF16)
    wts = (w_pa.astype(BF16), w_glu.astype(BF16), vecs(b_glu), w_ps.astype(BF16), w_o.astype(BF16),
           vecs(ln1_g), vecs(ln1_b), wr_split, vecs(br))
    lam_vecs = (vecs(lam_q1), vecs(lam_k1), vecs(lam_q2), vecs(lam_k2))
    ln2 = (vecs(ln2_g), vecs(ln2_b))
    subln = vecs(subln_g)

    proj = _inproj(0, xall, mods4, tabs, w_in_bf, n_ctx)
    for l in range(depth):
        last = l == depth - 1
        tile0 = n_ctx_tiles if last else 0
        lam_init = 0.8 - 0.6 * math.exp(-0.3 * l)
        k, v, u, q, siga, sigs = proj
        attn = _attention(l, q, k, v, lam_vecs, subln, n_ctx, tile0, lam_init)
        y = _ssm(l, u, ops, n_ctx)
        x1, h2, choices, counts = _merge(l, attn, y, siga, sigs, xall, mods4, wts, n_ctx, tile0, alpha)

        n_tok = h2.shape[0]
        n_blocks = -(-(2 * n_tok + N_EXPERTS * (MOE_BLOCK - 1)) // MOE_BLOCK)
        rec, blk = _route(choices, counts, n_blocks)
        n_tiles_tok = n_tok // ROW_TILE
        slots3 = (rec[:, 0:2].astype(jnp.int32).reshape(n_tiles_tok, ROW_TILE, 2)
                  .transpose(0, 2, 1).reshape(n_tiles_tok, 1, 2 * ROW_TILE))
        xs = _dispatch(h2, slots3, n_blocks * MOE_BLOCK)
        ys = _experts(l, blk[:n_blocks, 0], xs, moe_w1, moe_w3, moe_w2)
        if last:
            xall = _combine(l, slots3, rec, x1, mods4, ln2[0], ln2[1], ys, n_ctx, tile0, alpha)
        else:
            xall, *proj = _combine_inproj(l, slots3, rec, x1, mods4, ln2[0], ln2[1], ys, tabs, w_in_bf, n_ctx,
                                          alpha)
    return xall
```
